```python
import math
import jax, jax.numpy as jnp
from jax import lax
import numpy as np

D_MODEL = 1024
BATCH = 4
SEQ = 4096
DEPTH = 4
DEC_BATCH = 128
DEC_SEQ = 8
PAST_LEN = 2048
PAGE_SIZE = 128

N_MIXERS = 3
N_NSA = (DEPTH + 2) // 3
N_DIFF = (DEPTH + 1) // 3
N_SB = DEPTH // 3
HEAD_DIM = 64
NSA_HEADS = D_MODEL // HEAD_DIM
NSA_KV_HEADS = 4
NSA_HPG = NSA_HEADS // NSA_KV_HEADS
CMP_LEN = 32
CMP_STRIDE = 16
CMP_HIDDEN = 2 * HEAD_DIM
SEL_BLOCK = 64
N_SEL = 16
SEL_FORCE = 1000.0
WINDOW = 512
NSA_IN = NSA_HEADS * HEAD_DIM + 6 * NSA_KV_HEADS * HEAD_DIM + 3 * NSA_HEADS
DIFF_HEADS = D_MODEL // (2 * HEAD_DIM)
DIFF_DIM = HEAD_DIM
DIFF_IN = 3 * DIFF_HEADS * 2 * DIFF_DIM
SB_HEADS = D_MODEL // HEAD_DIM
SB_IN = 3 * SB_HEADS * HEAD_DIM
D_FF = 4 * D_MODEL
REL_BUCKETS = 32
REL_MAX_DIST = 128
BIAS_HEADS = NSA_HEADS
Q_BLOCK = 128
DN_ALPHA = (2 * DEPTH) ** 0.25
DN_BETA = (8 * DEPTH) ** -0.25
LN_EPS = 1e-5
NEG = -1e30

kernel_name = 'hybrid_nsa_diff_stickbreak_step'


def layer_norm(x, g, b):
    xf = x.astype(jnp.float32)
    mu = jnp.mean(xf, axis=-1, keepdims=True)
    var = jnp.mean(jnp.square(xf - mu), axis=-1, keepdims=True)
    return ((xf - mu) * lax.rsqrt(var + LN_EPS) * g + b).astype(x.dtype)


def sq_relu_mlp(x, w_up, w_down):
    return jnp.square(jax.nn.relu(x @ w_up)) @ w_down


def t5_bucket(dist):
    n = jnp.maximum(dist, 0)
    max_exact = REL_BUCKETS // 2
    nf = jnp.maximum(n, max_exact).astype(jnp.float32)
    large = max_exact + (jnp.log(nf / max_exact) / math.log(REL_MAX_DIST / max_exact)
                         * (REL_BUCKETS - max_exact)).astype(jnp.int32)
    return jnp.where(n < max_exact, n, jnp.minimum(large, REL_BUCKETS - 1))


def masked_softmax(s, mask):
    s = jnp.where(mask, s, NEG)
    m = jnp.max(s, axis=-1, keepdims=True)
    e = jnp.where(mask, jnp.exp(s - m), 0.0)
    return e / jnp.maximum(jnp.sum(e, axis=-1, keepdims=True), 1e-30)


def sweep_query_blocks(fn, q, q_pos0):
    B, T = q.shape[:2]
    qb = math.gcd(T, Q_BLOCK)
    nb = T // qb
    qs = jnp.swapaxes(q.reshape(B, nb, qb, *q.shape[2:]), 0, 1)
    starts = q_pos0 + qb * jnp.arange(nb, dtype=jnp.int32)
    out = lax.map(lambda a: fn(a[0], a[1]), (qs, starts))
    return jnp.swapaxes(out, 0, 1).reshape(B, T, *out.shape[3:])


def gather_pages(pool, page_table):
    g = pool[page_table]
    return g.reshape(page_table.shape[0], -1, *pool.shape[2:])


def compress_blocks(raw, pos, w1, w2):
    B, L, G, dk = raw.shape
    n_cmp = (L - CMP_LEN) // CMP_STRIDE + 1
    idx = np.arange(n_cmp)[:, None] * CMP_STRIDE + np.arange(CMP_LEN)[None, :]
    blk = raw[:, idx] + pos[:, None, :]
    blk = blk.transpose(0, 1, 3, 2, 4).reshape(B, n_cmp, G, CMP_LEN * dk)
    return jax.nn.gelu(blk @ w1) @ w2


def cmp_sel_overlap(n_cmp, n_slc):
    c0 = np.arange(n_cmp)[:, None] * CMP_STRIDE
    j0 = np.arange(n_slc)[None, :] * SEL_BLOCK
    return ((c0 < j0 + SEL_BLOCK) & (c0 + CMP_LEN > j0)).astype(np.float32)


def nsa_mixer(x, kv_past, win_buf, w_in, cmp_pos, cmp_w1, cmp_w2, w_out, rel_bias):
    B, T, _ = x.shape
    H, G, R, dk = NSA_HEADS, NSA_KV_HEADS, NSA_HPG, HEAD_DIM
    f32 = jnp.float32
    scale = HEAD_DIM ** -0.5
    nq, nkv = H * dk, 6 * G * dk
    proj = x @ w_in
    q = proj[..., :nq].reshape(B, T, G, R, dk)
    kv_all = proj[..., nq:nq + nkv].reshape(B, T, 6, G, dk)
    gates = jax.nn.sigmoid(proj[..., nq + nkv:].astype(f32)).reshape(B, T, 3, G, R, 1)
    new_kv = kv_all[:, :, :4]
    new_win = kv_all[:, :, 4:]
    if kv_past is None:
        q_pos0, kv, win, first_win = 0, new_kv, new_win, 0
    else:
        q_pos0 = kv_past.shape[1]
        kv = jnp.concatenate([kv_past, new_kv], axis=1)
        win = jnp.concatenate([win_buf, new_win], axis=1)
        first_win = q_pos0 - win_buf.shape[1]
    L = kv.shape[1]
    rel_g = rel_bias.astype(f32).reshape(REL_BUCKETS, G, R)

    kc = compress_blocks(kv[:, :, 0], cmp_pos[0], cmp_w1[0], cmp_w2[0]).astype(f32)
    vc = compress_blocks(kv[:, :, 1], cmp_pos[1], cmp_w1[1], cmp_w2[1]).astype(f32)
    n_cmp = kc.shape[1]
    cmp_end = jnp.asarray(np.arange(n_cmp) * CMP_STRIDE + CMP_LEN - 1, jnp.int32)
    n_slc = -(-L // SEL_BLOCK)
    overlap = jnp.asarray(cmp_sel_overlap(n_cmp, n_slc))
    k_sel = min(N_SEL, n_slc)
    pad = n_slc * SEL_BLOCK - L

    def to_blocks(a):
        a = jnp.pad(a, ((0, 0), (0, pad), (0, 0), (0, 0))).astype(f32)
        return a.reshape(B, n_slc, SEL_BLOCK, G, dk).transpose(0, 3, 1, 2, 4)

    ks_blk, vs_blk = to_blocks(kv[:, :, 2]), to_blocks(kv[:, :, 3])
    blk_ids = jnp.arange(n_slc, dtype=jnp.int32)
    b_ix = jnp.arange(B)[:, None, None, None]
    g_ix = jnp.arange(G)[None, None, :, None]

    def cmp_slc_block(qb, t0):
        Qb = qb.shape[1]
        t = t0 + jnp.arange(Qb, dtype=jnp.int32)
        qf = qb.astype(f32) * scale
        s_c = jnp.einsum('bqgrd,bngd->bqgrn', qf, kc)
        s_c = s_c + rel_g[t5_bucket(t[:, None] - cmp_end[None, :])].transpose(0, 2, 3, 1)
        p_c = masked_softmax(s_c, (cmp_end[None, :] <= t[:, None])[:, None, None, :])
        o_c = jnp.einsum('bqgrn,bngd->bqgrd', p_c, vc)
        imp = jnp.einsum('bqgn,nj->bqgj', p_c.sum(3), overlap)
        t_blk = (t // SEL_BLOCK)[:, None]
        forced = (blk_ids == 0) | (blk_ids == t_blk) | (blk_ids == t_blk - 1)
        causal = blk_ids * SEL_BLOCK <= t[:, None]
        imp = jnp.where(causal[:, None], imp + SEL_FORCE * forced[:, None], -1.0)
        _, sel = lax.top_k(imp, k_sel)
        k_g = ks_blk[b_ix, g_ix, sel]
        v_g = vs_blk[b_ix, g_ix, sel]
        pos = sel[..., None] * SEL_BLOCK + jnp.arange(SEL_BLOCK, dtype=jnp.int32)
        tq = t[None, :, None, None, None]
        bias_s = rel_g[t5_bucket(tq - pos), g_ix[..., None]]
        s_s = jnp.einsum('bqgrd,bqgksd->bqgrks', qf, k_g) + bias_s.transpose(0, 1, 2, 5, 3, 4)
        mask_s = ((pos <= tq) & (pos < L))[:, :, :, None]
        p_s = masked_softmax(s_s.reshape(*s_s.shape[:4], -1),
                             mask_s.reshape(*mask_s.shape[:4], -1)).reshape(s_s.shape)
        o_s = jnp.einsum('bqgrks,bqgksd->bqgrd', p_s, v_g)
        return jnp.stack([o_c, o_s], axis=2)

    kw_ext = jnp.pad(win[:, :, 0], ((0, 0), (WINDOW, 0), (0, 0), (0, 0))).astype(f32)
    vw_ext = jnp.pad(win[:, :, 1], ((0, 0), (WINDOW, 0), (0, 0), (0, 0))).astype(f32)

    def win_block(qb, t0):
        Qb = qb.shape[1]
        n_k = WINDOW + Qb - 1
        t = t0 + jnp.arange(Qb, dtype=jnp.int32)
        start = t0 + 1 - first_win
        kw = lax.dynamic_slice_in_dim(kw_ext, start, n_k, axis=1)
        vw = lax.dynamic_slice_in_dim(vw_ext, start, n_k, axis=1)
        kpos = t0 + 1 - WINDOW + jnp.arange(n_k, dtype=jnp.int32)
        d = t[:, None] - kpos[None, :]
        s = jnp.einsum('bqgrd,bkgd->bqgrk', qb.astype(f32) * scale, kw)
        s = s + rel_g[t5_bucket(d)].transpose(0, 2, 3, 1)
        mask = ((d >= 0) & (d < WINDOW) & (kpos[None, :] >= 0))[:, None, None, :]
        return jnp.einsum('bqgrk,bkgd->bqgrd', masked_softmax(s, mask), vw)

    o_cs = sweep_query_blocks(cmp_slc_block, q, q_pos0)
    o_w = sweep_query_blocks(win_block, q, q_pos0)
    o = gates[:, :, 0] * o_cs[:, :, 0] + gates[:, :, 1] * o_cs[:, :, 1] + gates[:, :, 2] * o_w
    y = o.reshape(B, T, H * dk).astype(x.dtype) @ w_out
    new_win_state = win[:, -min(WINDOW, win.shape[1]):]
    return y, new_kv, new_win_state


def diff_mixer(x, k_past, v_past, w_in, lam, norm_g, w_out, rel_bias, lam_init):
    B, T, _ = x.shape
    Hd, d = DIFF_HEADS, DIFF_DIM
    f32 = jnp.float32
    scale = d ** -0.5
    nqk = Hd * 2 * d
    proj = x @ w_in
    q = proj[..., :nqk].reshape(B, T, Hd, 2, d)
    k_new = proj[..., nqk:2 * nqk].reshape(B, T, Hd, 2, d)
    v_new = proj[..., 2 * nqk:].reshape(B, T, Hd, 2 * d)
    if k_past is None:
        q_pos0, k, v = 0, k_new, v_new
    else:
        q_pos0 = k_past.shape[1]
        k = jnp.concatenate([k_past, k_new], axis=1)
        v = jnp.concatenate([v_past, v_new], axis=1)
    L = k.shape[1]
    kf, vf = k.astype(f32), v.astype(f32)
    kpos = jnp.arange(L, dtype=jnp.int32)
    rel_m = rel_bias.astype(f32).reshape(REL_BUCKETS, 2, Hd)
    lf = lam.astype(f32)
    lam_full = jnp.exp(jnp.sum(lf[0] * lf[1])) - jnp.exp(jnp.sum(lf[2] * lf[3])) + lam_init

    def block(qb, t0):
        Qb = qb.shape[1]
        t = t0 + jnp.arange(Qb, dtype=jnp.int32)
        dist = t[:, None] - kpos[None, :]
        s = jnp.einsum('bqhmd,bkhmd->bhmqk', qb.astype(f32) * scale, kf)
        s = s + rel_m[t5_bucket(dist)].transpose(3, 2, 0, 1)
        p = masked_softmax(s, dist >= 0)
        a = p[:, :, 0] - lam_full * p[:, :, 1]
        return jnp.einsum('bhqk,bkhe->bqhe', a, vf)

    o = sweep_query_blocks(block, q, q_pos0)
    o = o * lax.rsqrt(jnp.mean(jnp.square(o), axis=-1, keepdims=True) + LN_EPS) * norm_g
    o = o * (1.0 - lam_init)
    y = o.reshape(B, T, Hd * 2 * d).astype(x.dtype) @ w_out
    return y, k_new, v_new


def sb_mixer(x, kv_past, w_in, w_out):
    B, T, _ = x.shape
    H, d = SB_HEADS, HEAD_DIM
    f32 = jnp.float32
    scale = d ** -0.5
    proj = (x @ w_in).reshape(B, T, 3, H, d)
    q = proj[:, :, 0]
    new_kv = proj[:, :, 1:]
    if kv_past is None:
        q_pos0, kv = 0, new_kv
    else:
        q_pos0 = kv_past.shape[1]
        kv = jnp.concatenate([kv_past, new_kv], axis=1)
    L = kv.shape[1]
    kf, vf = kv[:, :, 0].astype(f32), kv[:, :, 1].astype(f32)
    kpos = jnp.arange(L, dtype=jnp.int32)

    def block(qb, t0):
        Qb = qb.shape[1]
        t = t0 + jnp.arange(Qb, dtype=jnp.int32)
        z = jnp.einsum('bqhd,bkhd->bhqk', qb.astype(f32) * scale, kf)
        mask = kpos[None, :] < t[:, None]
        c = jnp.where(mask, jax.nn.softplus(z), 0.0)
        later = lax.cumsum(c, axis=3, reverse=True) - c
        a = jnp.where(mask, jnp.exp(jax.nn.log_sigmoid(z) - later), 0.0)
        return jnp.einsum('bhqk,bkhd->bqhd', a, vf)

    o = sweep_query_blocks(block, q, q_pos0)
    y = o.reshape(B, T, H * d).astype(x.dtype) @ w_out
    return y, new_kv


def setup_inputs(seed: int = 0) -> dict:
    key = jax.random.key(seed)
    ks = jax.random.split(key, 24)
    f32 = jnp.float32
    n_pages = PAST_LEN // PAGE_SIZE
    n_used = DEC_BATCH * n_pages
    n_pool = n_used + n_used // 4
    win_rows = min(WINDOW, PAST_LEN)
    G, dk = NSA_KV_HEADS, HEAD_DIM

    def nrm(k, shape, scale=1.0):
        return jax.random.normal(k, shape, f32) * scale

    nsa_col = np.ones(NSA_IN, np.float32)
    for m in (1, 3, 5):
        s0 = NSA_HEADS * HEAD_DIM + m * G * dk
        nsa_col[s0:s0 + G * dk] = DN_BETA
    diff_col = np.ones(DIFF_IN, np.float32)
    diff_col[2 * DIFF_HEADS * 2 * DIFF_DIM:] = DN_BETA
    sb_col = np.ones(SB_IN, np.float32)
    sb_col[2 * SB_HEADS * HEAD_DIM:] = DN_BETA

    page_table = jax.random.permutation(ks[0], n_pool)[:n_used].reshape(DEC_BATCH, n_pages).astype(jnp.int32)
    return {
        'x_prompt': nrm(ks[1], (BATCH, SEQ, D_MODEL)),
        'x_sample': nrm(ks[2], (DEC_BATCH, DEC_SEQ, D_MODEL)),
        'cache_nsa_kv': nrm(ks[3], (N_NSA, n_pool, PAGE_SIZE, 4, G, dk)),
        'state_nsa_win': nrm(ks[4], (N_NSA, DEC_BATCH, win_rows, 2, G, dk)),
        'cache_diff_k': nrm(ks[5], (N_DIFF, n_pool, PAGE_SIZE, DIFF_HEADS, 2, DIFF_DIM)),
        'cache_diff_v': nrm(ks[6], (N_DIFF, n_pool, PAGE_SIZE, DIFF_HEADS, 2 * DIFF_DIM)),
        'cache_sb_kv': nrm(ks[7], (N_SB, n_pool, PAGE_SIZE, 2, SB_HEADS, HEAD_DIM)),
        'page_table': page_table,
        'rel_bias': nrm(ks[8], (REL_BUCKETS, BIAS_HEADS), 0.2),
        'nsa_w_in': nrm(ks[9], (N_NSA, D_MODEL, NSA_IN), D_MODEL ** -0.5) * jnp.asarray(nsa_col),
        'nsa_cmp_pos': nrm(ks[10], (N_NSA, 2, CMP_LEN, HEAD_DIM), 0.1),
        'nsa_cmp_w1': nrm(ks[11], (N_NSA, 2, CMP_LEN * HEAD_DIM, CMP_HIDDEN), (CMP_LEN * HEAD_DIM) ** -0.5),
        'nsa_cmp_w2': nrm(ks[12], (N_NSA, 2, CMP_HIDDEN, HEAD_DIM), CMP_HIDDEN ** -0.5),
        'nsa_w_out': nrm(ks[13], (N_NSA, NSA_HEADS * HEAD_DIM, D_MODEL), (NSA_HEADS * HEAD_DIM) ** -0.5 * DN_BETA),
        'diff_w_in': nrm(ks[14], (N_DIFF, D_MODEL, DIFF_IN), D_MODEL ** -0.5) * jnp.asarray(diff_col),
        'diff_lambda': nrm(ks[15], (N_DIFF, 4, DIFF_DIM), 0.1),
        'diff_norm_g': 1.0 + nrm(ks[16], (N_DIFF, 2 * DIFF_DIM), 0.02),
        'diff_w_out': nrm(ks[17], (N_DIFF, DIFF_HEADS * 2 * DIFF_DIM, D_MODEL), (DIFF_HEADS * 2 * DIFF_DIM) ** -0.5 * DN_BETA),
        'sb_w_in': nrm(ks[18], (N_SB, D_MODEL, SB_IN), D_MODEL ** -0.5) * jnp.asarray(sb_col),
        'sb_w_out': nrm(ks[19], (N_SB, SB_HEADS * HEAD_DIM, D_MODEL), (SB_HEADS * HEAD_DIM) ** -0.5 * DN_BETA),
        'mlp_w_up': nrm(ks[20], (DEPTH, D_MODEL, D_FF), D_MODEL ** -0.5 * DN_BETA),
        'mlp_w_down': nrm(ks[21], (DEPTH, D_FF, D_MODEL), D_FF ** -0.5 * DN_BETA),
        'ln_g': 1.0 + nrm(ks[22], (DEPTH, 2, D_MODEL), 0.02),
        'ln_b': nrm(ks[23], (DEPTH, 2, D_MODEL), 0.02),
    }


def reference(x_prompt, x_sample, cache_nsa_kv, state_nsa_win, cache_diff_k, cache_diff_v, cache_sb_kv,
              page_table, rel_bias, nsa_w_in, nsa_cmp_pos, nsa_cmp_w1, nsa_cmp_w2, nsa_w_out,
              diff_w_in, diff_lambda, diff_norm_g, diff_w_out, sb_w_in, sb_w_out,
              mlp_w_up, mlp_w_down, ln_g, ln_b):
    xp, xs = x_prompt, x_sample
    nsa_kv_p, nsa_kv_s, nsa_win_p, nsa_win_s = [], [], [], []
    diff_k_p, diff_k_s, diff_v_p, diff_v_s = [], [], [], []
    sb_kv_p, sb_kv_s = [], []
    for i in range(DEPTH):
        kind, j = i % N_MIXERS, i // N_MIXERS
        if kind == 0:
            past = gather_pages(cache_nsa_kv[j], page_table)
            args = (nsa_w_in[j], nsa_cmp_pos[j], nsa_cmp_w1[j], nsa_cmp_w2[j], nsa_w_out[j], rel_bias)
            mp, kv_p, win_p = nsa_mixer(xp, None, None, *args)
            ms, kv_s, win_s = nsa_mixer(xs, past, state_nsa_win[j], *args)
            nsa_kv_p.append(kv_p)
            nsa_kv_s.append(kv_s)
            nsa_win_p.append(win_p)
            nsa_win_s.append(win_s)
        elif kind == 1:
            lam_init = 0.8 - 0.6 * math.exp(-0.3 * i)
            kp = gather_pages(cache_diff_k[j], page_table)
            vp = gather_pages(cache_diff_v[j], page_table)
            args = (diff_w_in[j], diff_lambda[j], diff_norm_g[j], diff_w_out[j], rel_bias, lam_init)
            mp, k_p, v_p = diff_mixer(xp, None, None, *args)
            ms, k_s, v_s = diff_mixer(xs, kp, vp, *args)
            diff_k_p.append(k_p)
            diff_k_s.append(k_s)
            diff_v_p.append(v_p)
            diff_v_s.append(v_s)
        else:
            past = gather_pages(cache_sb_kv[j], page_table)
            mp, kv_p = sb_mixer(xp, None, sb_w_in[j], sb_w_out[j])
            ms, kv_s = sb_mixer(xs, past, sb_w_in[j], sb_w_out[j])
            sb_kv_p.append(kv_p)
            sb_kv_s.append(kv_s)
        xp = layer_norm(DN_ALPHA * xp + mp, ln_g[i, 0], ln_b[i, 0])
        xs = layer_norm(DN_ALPHA * xs + ms, ln_g[i, 0], ln_b[i, 0])
        xp = layer_norm(DN_ALPHA * xp + sq_relu_mlp(xp, mlp_w_up[i], mlp_w_down[i]), ln_g[i, 1], ln_b[i, 1])
        xs = layer_norm(DN_ALPHA * xs + sq_relu_mlp(xs, mlp_w_up[i], mlp_w_down[i]), ln_g[i, 1], ln_b[i, 1])
    return (xp, xs,
            jnp.stack(nsa_kv_p), jnp.stack(nsa_kv_s), jnp.stack(nsa_win_p), jnp.stack(nsa_win_s),
            jnp.stack(diff_k_p), jnp.stack(diff_k_s), jnp.stack(diff_v_p), jnp.stack(diff_v_s),
            jnp.stack(sb_kv_p), jnp.stack(sb_kv_s))
```

```python
import functools
import math

import numpy as np
import jax
import jax.numpy as jnp
from jax import lax
from jax.experimental import pallas as pl
from jax.experimental.pallas import tpu as pltpu

F32 = jnp.float32
BF16 = jnp.bfloat16

HEAD_DIM = 64
NSA_KV_HEADS = 4
NSA_HPG = 4
CMP_LEN = 32
CMP_STRIDE = 16
SEL_BLOCK = 64
N_SEL = 16
SEL_FORCE = 1000.0
WINDOW = 512
REL_BUCKETS = 32
REL_MAX_DIST = 128
LN_EPS = 1e-5
NEG = -1e30
N_MIXERS = 3

LANES = 128
VMEM_LIMIT = 48 * 1024 * 1024


def _cparams(*sem):
    return pltpu.CompilerParams(dimension_semantics=sem, vmem_limit_bytes=VMEM_LIMIT)


def _t5_bucket_np(dist):
    n = np.maximum(dist, 0)
    max_exact = REL_BUCKETS // 2
    nf = np.maximum(n, max_exact).astype(np.float32)
    large = max_exact + (np.log(nf / np.float32(max_exact)) / np.float32(math.log(REL_MAX_DIST / max_exact))
                         * np.float32(REL_BUCKETS - max_exact)).astype(np.int32)
    return np.where(n < max_exact, n, np.minimum(large, REL_BUCKETS - 1)).astype(np.int32)


FAR_DIST = int(np.min(np.nonzero(_t5_bucket_np(np.arange(4 * REL_MAX_DIST)) == REL_BUCKETS - 1)[0]))


def _toeplitz_bias(rel_bias, tq, tk, n_off):
    i = np.arange(tq)[None, :, None]
    j = np.arange(tk)[None, None, :]
    off = np.arange(n_off)[:, None, None]
    idx = _t5_bucket_np(off * tk + i - j)
    return jnp.transpose(rel_bias.astype(F32)[idx], (3, 0, 1, 2))


def _dense_bias(rel_bias, qpos, kpos, cols):
    idx = _t5_bucket_np(qpos[:, None] - kpos[None, :])
    tab = rel_bias.astype(F32)[idx]
    tab = jnp.transpose(tab[:, :, np.asarray(cols)], (2, 0, 1))
    return tab.reshape(len(cols) * len(qpos), len(kpos))


def _layer_norm(z, g, b):
    mu = jnp.mean(z, axis=-1, keepdims=True)
    zc = z - mu
    var = jnp.mean(zc * zc, axis=-1, keepdims=True)
    return zc * lax.rsqrt(var + LN_EPS) * g + b


def _multi_linear_kernel(*refs, n):
    x_ref, w_refs, o_refs = refs[0], refs[1:1 + n], refs[1 + n:]
    xb = x_ref[...].astype(BF16)
    for w_ref, o_ref in zip(w_refs, o_refs):
        o_ref[...] = jnp.dot(xb, w_ref[...], preferred_element_type=F32)


def multi_linear(x, ws, tm=512):
    M, K = x.shape
    n = len(ws)
    return pl.pallas_call(
        functools.partial(_multi_linear_kernel, n=n),
        grid=(M // tm,),
        in_specs=[pl.BlockSpec((tm, K), lambda i: (i, 0))]
        + [pl.BlockSpec(w.shape, lambda i: (0, 0)) for w in ws],
        out_specs=[pl.BlockSpec((tm, w.shape[1]), lambda i: (i, 0)) for w in ws],
        out_shape=[jax.ShapeDtypeStruct((M, w.shape[1]), F32) for w in ws],
        compiler_params=_cparams("parallel"),
        name="multi_linear",
    )(x, *ws)


def _post_mixer_kernel(o_ref, wo_ref, x_ref, ln_ref, wu_ref, wd_ref, y_ref, x1_scr, xb_scr, acc_scr, *, alpha):
    f = pl.program_id(1)

    @pl.when(f == 0)
    def _():
        y = jnp.dot(o_ref[...].astype(BF16), wo_ref[...], preferred_element_type=F32)
        x1 = _layer_norm(alpha * x_ref[...] + y, ln_ref[0:1, :], ln_ref[1:2, :])
        x1_scr[...] = x1
        xb_scr[...] = x1.astype(BF16)
        acc_scr[...] = jnp.zeros_like(acc_scr)

    h = jnp.dot(xb_scr[...], wu_ref[...], preferred_element_type=F32)
    h = jnp.square(jnp.maximum(h, 0.0))
    acc_scr[...] += jnp.dot(h.astype(BF16), wd_ref[...], preferred_element_type=F32)

    @pl.when(f == pl.num_programs(1) - 1)
    def _():
        y_ref[...] = _layer_norm(alpha * x1_scr[...] + acc_scr[...], ln_ref[2:3, :], ln_ref[3:4, :])


def post_mixer(o, w_out, x, ln, w_up, w_down, alpha, tm=512, tf=1024):
    M, D = x.shape
    Fd = w_up.shape[1]
    return pl.pallas_call(
        functools.partial(_post_mixer_kernel, alpha=alpha),
        grid=(M // tm, Fd // tf),
        in_specs=[
            pl.BlockSpec((tm, D), lambda i, f: (i, 0)),
            pl.BlockSpec((D, D), lambda i, f: (0, 0)),
            pl.BlockSpec((tm, D), lambda i, f: (i, 0)),
            pl.BlockSpec((4, D), lambda i, f: (0, 0)),
            pl.BlockSpec((D, tf), lambda i, f: (0, f)),
            pl.BlockSpec((tf, D), lambda i, f: (f, 0)),
        ],
        out_specs=pl.BlockSpec((tm, D), lambda i, f: (i, 0)),
        out_shape=jax.ShapeDtypeStruct((M, D), F32),
        scratch_shapes=[pltpu.VMEM((tm, D), F32), pltpu.VMEM((tm, D), BF16), pltpu.VMEM((tm, D), F32)],
        compiler_params=_cparams("parallel", "arbitrary"),
        name="post_mixer",
    )(o, w_out, x, ln, w_up, w_down)


def _nt_dot(a, b):
    return lax.dot_general(a, b, (((1,), (1,)), ((), ())), preferred_element_type=F32)


def _online_softmax_step(carry, s, v, mask=None):
    m, l, acc = carry
    m_new = jnp.maximum(m, jnp.max(s, axis=-1, keepdims=True))
    alpha = jnp.exp(m - m_new)
    p = jnp.exp(s - m_new)
    if mask is not None:
        p = jnp.where(mask, p, 0.0)
    l = alpha * l + jnp.sum(p, axis=-1, keepdims=True)
    acc = alpha * acc + jnp.dot(p.astype(BF16), v, preferred_element_type=F32)
    return m_new, l, acc


def _split_heads(q, tq):
    lane = lax.broadcasted_iota(jnp.int32, (tq, LANES), 1)
    lo = jnp.where(lane < HEAD_DIM, q, 0.0)
    hi = jnp.where(lane >= HEAD_DIM, q, 0.0)
    return jnp.concatenate([lo, hi], axis=0).astype(BF16)


def _diff_lambda(lam_ref, lam_init):
    lf = lam_ref[...]
    a = jnp.sum(lf[0:1, :] * lf[1:2, :], axis=-1, keepdims=True)
    b = jnp.sum(lf[2:3, :] * lf[3:4, :], axis=-1, keepdims=True)
    return jnp.exp(a) - jnp.exp(b) + lam_init


def _diff_finish(o1, o2, lam_full, g, lam_init):
    o = o1 - lam_full * o2
    o = o * lax.rsqrt(jnp.mean(o * o, axis=-1, keepdims=True) + LN_EPS) * g
    return o * (1.0 - lam_init)


def _diff_prompt_kernel(c31_ref, q_ref, k_ref, v_ref, toe_ref, lam_ref, g_ref, o_ref, *, tq, n_heads, lam_init):
    h = pl.program_id(1)
    qt = pl.program_id(2)
    tk = tq
    scale = HEAD_DIM ** -0.5
    q2 = _split_heads(q_ref[0] * scale, tq)
    row = lax.broadcasted_iota(jnp.int32, (2 * tq, 1), 0)
    c_far = jnp.where(row < tq, c31_ref[h], c31_ref[n_heads + h])

    def kv_tile(kt):
        k = k_ref[0, pl.ds(pl.multiple_of(kt * tk, tk), tk), :].astype(BF16)
        v = v_ref[0, pl.ds(pl.multiple_of(kt * tk, tk), tk), :].astype(BF16)
        return k, v

    def far_body(kt, carry):
        k, v = kv_tile(kt)
        bias = jnp.where(kt == qt - 1, toe_ref[0, 1], c_far)
        return _online_softmax_step(carry, _nt_dot(q2, k) + bias, v)

    init = (jnp.full((2 * tq, 1), NEG, F32), jnp.zeros((2 * tq, 1), F32), jnp.zeros((2 * tq, LANES), F32))
    carry = lax.fori_loop(0, qt, far_body, init)
    k, v = kv_tile(qt)
    ri = lax.broadcasted_iota(jnp.int32, (2 * tq, tk), 0)
    ci = lax.broadcasted_iota(jnp.int32, (2 * tq, tk), 1)
    mask = ci <= jnp.where(ri >= tq, ri - tq, ri)
    s = jnp.where(mask, _nt_dot(q2, k) + toe_ref[0, 0], NEG)
    _, l, acc = _online_softmax_step(carry, s, v, mask)
    o = acc / l
    o_ref[0] = _diff_finish(o[:tq], o[tq:], _diff_lambda(lam_ref, lam_init), g_ref[...], lam_init)


def diff_prompt_attention(q, k, v, rel_bias, lam, norm_g, lam_init, tq=128):
    B, T, W = q.shape
    n_heads = W // LANES
    toe = _toeplitz_bias(rel_bias, tq, tq, 2)
    toe = jnp.concatenate([toe[:n_heads], toe[n_heads:]], axis=2)
    c31 = rel_bias[REL_BUCKETS - 1].astype(F32)
    return pl.pallas_call(
        functools.partial(_diff_prompt_kernel, tq=tq, n_heads=n_heads, lam_init=lam_init),
        grid=(B, n_heads, T // tq),
        in_specs=[
            pl.BlockSpec(memory_space=pltpu.SMEM),
            pl.BlockSpec((1, tq, LANES), lambda b, h, i: (b, i, h)),
            pl.BlockSpec((1, T, LANES), lambda b, h, i: (b, 0, h)),
            pl.BlockSpec((1, T, LANES), lambda b, h, i: (b, 0, h)),
            pl.BlockSpec((1, 2, 2 * tq, tq), lambda b, h, i: (h, 0, 0, 0)),
            pl.BlockSpec((4, HEAD_DIM), lambda b, h, i: (0, 0)),
            pl.BlockSpec((1, LANES), lambda b, h, i: (0, 0)),
        ],
        out_specs=pl.BlockSpec((1, tq, LANES), lambda b, h, i: (b, i, h)),
        out_shape=jax.ShapeDtypeStruct((B, T, W), F32),
        compiler_params=_cparams("parallel", "parallel", "arbitrary"),
        name="diff_prompt",
    )(c31, q, k, v, toe, lam, norm_g.reshape(1, LANES))


def _rows_blockdiag(q, n_blk, blk_w):
    rep = jnp.concatenate([q] * n_blk, axis=0)
    r = lax.broadcasted_iota(jnp.int32, rep.shape, 0) // q.shape[0]
    c = lax.broadcasted_iota(jnp.int32, rep.shape, 1) // blk_w
    return jnp.where(r == c, rep, 0.0).astype(BF16)


def _pad_rows(a, rows):
    return jnp.concatenate([a, jnp.zeros((rows - a.shape[0], a.shape[1]), a.dtype)], axis=0)


def _diff_decode_kernel(pt_ref, q_ref, kn_ref, vn_ref, bias_ref, lam_ref, g_ref, *rest, n_pages, page, lam_init):
    kp, vp = rest[:n_pages], rest[n_pages:2 * n_pages]
    o_ref, s_scr = rest[2 * n_pages], rest[2 * n_pages + 1]
    nq = q_ref.shape[1]
    n_heads = q_ref.shape[2] // LANES
    rows = 2 * n_heads * nq
    qbd = _rows_blockdiag(q_ref[0] * HEAD_DIM ** -0.5, 2 * n_heads, HEAD_DIM)
    kn = _pad_rows(kn_ref[0], page).astype(BF16)
    vn = _pad_rows(vn_ref[0], page).astype(BF16)
    m = jnp.full((rows, 1), NEG, F32)
    for p in range(n_pages):
        s = _nt_dot(qbd, kp[p][0].astype(BF16)) + bias_ref[:, p * page:(p + 1) * page]
        s_scr[:, p * page:(p + 1) * page] = s
        m = jnp.maximum(m, jnp.max(s, axis=-1, keepdims=True))
    ri = lax.broadcasted_iota(jnp.int32, (rows, page), 0) % nq
    ci = lax.broadcasted_iota(jnp.int32, (rows, page), 1)
    mask = ci <= ri
    sn = jnp.where(mask, _nt_dot(qbd, kn) + bias_ref[:, n_pages * page:], NEG)
    m = jnp.maximum(m, jnp.max(sn, axis=-1, keepdims=True))
    pn = jnp.where(mask, jnp.exp(sn - m), 0.0)
    l = jnp.sum(pn, axis=-1, keepdims=True)
    acc = jnp.dot(pn.astype(BF16), vn, preferred_element_type=F32)
    for p in range(n_pages):
        pr = jnp.exp(s_scr[:, p * page:(p + 1) * page] - m)
        l = l + jnp.sum(pr, axis=-1, keepdims=True)
        acc = acc + jnp.dot(pr.astype(BF16), vp[p][0].astype(BF16), preferred_element_type=F32)
    o = acc / l
    lam_full = _diff_lambda(lam_ref, lam_init)
    outs = []
    for h in range(n_heads):
        r0 = 2 * h * nq
        o1 = o[r0:r0 + nq, h * LANES:(h + 1) * LANES]
        o2 = o[r0 + nq:r0 + 2 * nq, h * LANES:(h + 1) * LANES]
        outs.append(_diff_finish(o1, o2, lam_full, g_ref[...], lam_init))
    o_ref[0] = jnp.concatenate(outs, axis=1)


def _page_specs(n_pages, block, col):
    return [pl.BlockSpec(block, functools.partial(lambda b, pt, p: (pt[b, p], 0, col), p=p)) for p in range(n_pages)]


def diff_decode_attention(q, kn, vn, k_pool, v_pool, page_table, rel_bias, lam, norm_g, lam_init):
    S, nq, W = q.shape
    n_pages = page_table.shape[1]
    page = k_pool.shape[1]
    n_heads = W // LANES
    P = n_pages * page
    cols = [mp * n_heads + h for h in range(n_heads) for mp in range(2)]
    bias = _dense_bias(rel_bias, P + np.arange(nq), np.arange(P + page), cols)
    rows = bias.shape[0]
    grid_spec = pltpu.PrefetchScalarGridSpec(
        num_scalar_prefetch=1,
        grid=(S,),
        in_specs=[pl.BlockSpec((1, nq, W), lambda b, pt: (b, 0, 0))] * 3
        + [pl.BlockSpec(bias.shape, lambda b, pt: (0, 0)),
           pl.BlockSpec((4, HEAD_DIM), lambda b, pt: (0, 0)),
           pl.BlockSpec((1, LANES), lambda b, pt: (0, 0))]
        + _page_specs(n_pages, (1, page, W), 0) * 2,
        out_specs=pl.BlockSpec((1, nq, W), lambda b, pt: (b, 0, 0)),
        scratch_shapes=[pltpu.VMEM((rows, P), F32)],
    )
    return pl.pallas_call(
        functools.partial(_diff_decode_kernel, n_pages=n_pages, page=page, lam_init=lam_init),
        grid_spec=grid_spec,
        out_shape=jax.ShapeDtypeStruct((S, nq, W), F32),
        compiler_params=_cparams("arbitrary"),
        name="diff_decode",
    )(page_table, q, kn, vn, bias, lam, norm_g.reshape(1, LANES), *([k_pool] * n_pages), *([v_pool] * n_pages))


def _softplus(z):
    return jnp.maximum(z, 0.0) + jnp.log(1.0 + jnp.exp(-jnp.abs(z)))


def _suffix_sum(c, u):
    hi = c.astype(BF16)
    lo = (c - hi.astype(F32)).astype(BF16)
    return jnp.dot(hi, u, preferred_element_type=F32) + jnp.dot(lo, u, preferred_element_type=F32)


def _sb_step(carry, z, v, u, mask=None):
    r, acc = carry
    c = _softplus(z)
    if mask is not None:
        c = jnp.where(mask, c, 0.0)
    a = jnp.exp(z - c - (_suffix_sum(c, u) + r))
    if mask is not None:
        a = jnp.where(mask, a, 0.0)
    acc = acc + jnp.dot(a.astype(BF16), v, preferred_element_type=F32)
    return r + jnp.sum(c, axis=-1, keepdims=True), acc


def _strict_upper(n):
    return jnp.asarray(np.arange(n)[:, None] > np.arange(n)[None, :], BF16)


def _sb_prompt_kernel(q_ref, k_ref, v_ref, u_ref, o_ref, *, tq):
    qt = pl.program_id(2)
    tk = tq
    q2 = _split_heads(q_ref[0] * HEAD_DIM ** -0.5, tq)
    u = u_ref[...]

    def kv_tile(kt):
        k = k_ref[0, pl.ds(pl.multiple_of(kt * tk, tk), tk), :].astype(BF16)
        v = v_ref[0, pl.ds(pl.multiple_of(kt * tk, tk), tk), :].astype(BF16)
        return k, v

    k, v = kv_tile(qt)
    ri = lax.broadcasted_iota(jnp.int32, (2 * tq, tk), 0)
    ci = lax.broadcasted_iota(jnp.int32, (2 * tq, tk), 1)
    mask = ci < jnp.where(ri >= tq, ri - tq, ri)
    init = (jnp.zeros((2 * tq, 1), F32), jnp.zeros((2 * tq, LANES), F32))
    carry = _sb_step(init, _nt_dot(q2, k), v, u, mask)

    def body(i, carry):
        k, v = kv_tile(qt - 1 - i)
        return _sb_step(carry, _nt_dot(q2, k), v, u)

    _, acc = lax.fori_loop(0, qt, body, carry)
    lane = lax.broadcasted_iota(jnp.int32, (tq, LANES), 1)
    o_ref[0] = jnp.where(lane < HEAD_DIM, acc[:tq], acc[tq:])


def sb_prompt_attention(q, kv, tq=128):
    B, T, W = q.shape
    n_pairs = W // LANES
    return pl.pallas_call(
        functools.partial(_sb_prompt_kernel, tq=tq),
        grid=(B, n_pairs, T // tq),
        in_specs=[
            pl.BlockSpec((1, tq, LANES), lambda b, h, i: (b, i, h)),
            pl.BlockSpec((1, T, LANES), lambda b, h, i: (b, 0, h)),
            pl.BlockSpec((1, T, LANES), lambda b, h, i: (b, 0, n_pairs + h)),
            pl.BlockSpec((tq, tq), lambda b, h, i: (0, 0)),
        ],
        out_specs=pl.BlockSpec((1, tq, LANES), lambda b, h, i: (b, i, h)),
        out_shape=jax.ShapeDtypeStruct((B, T, W), F32),
        compiler_params=_cparams("parallel", "parallel", "arbitrary"),
        name="sb_prompt",
    )(q, kv, kv, _strict_upper(tq))


def _sb_decode_kernel(pt_ref, q_ref, kvn_ref, u_ref, *rest, n_pages, page):
    pages, o_ref = rest[:n_pages], rest[n_pages]
    nq, W = q_ref.shape[1], q_ref.shape[2]
    n_heads = W // HEAD_DIM
    rows = n_heads * nq
    u = u_ref[...]
    qbd = _rows_blockdiag(q_ref[0] * HEAD_DIM ** -0.5, n_heads, HEAD_DIM)
    kvn = _pad_rows(kvn_ref[0], page).astype(BF16)
    ri = lax.broadcasted_iota(jnp.int32, (rows, page), 0) % nq
    ci = lax.broadcasted_iota(jnp.int32, (rows, page), 1)
    init = (jnp.zeros((rows, 1), F32), jnp.zeros((rows, W), F32))
    carry = _sb_step(init, _nt_dot(qbd, kvn[:, :W]), kvn[:, W:], u, ci < ri)
    for p in reversed(range(n_pages)):
        kv = pages[p][0].astype(BF16)
        carry = _sb_step(carry, _nt_dot(qbd, kv[:, :W]), kv[:, W:], u)
    acc = carry[1]
    col = lax.broadcasted_iota(jnp.int32, (nq, W), 1) // HEAD_DIM
    o = jnp.zeros((nq, W), F32)
    for h in range(n_heads):
        o = o + jnp.where(col == h, acc[h * nq:(h + 1) * nq, :], 0.0)
    o_ref[0] = o


def sb_decode_attention(q, kvn, kv_pool, page_table):
    S, nq, W = q.shape
    n_pages = page_table.shape[1]
    page = kv_pool.shape[1]
    grid_spec = pltpu.PrefetchScalarGridSpec(
        num_scalar_prefetch=1,
        grid=(S,),
        in_specs=[pl.BlockSpec((1, nq, W), lambda b, pt: (b, 0, 0)),
                  pl.BlockSpec((1, nq, 2 * W), lambda b, pt: (b, 0, 0)),
                  pl.BlockSpec((page, page), lambda b, pt: (0, 0))]
        + _page_specs(n_pages, (1, page, 2 * W), 0),
        out_specs=pl.BlockSpec((1, nq, W), lambda b, pt: (b, 0, 0)),
    )
    return pl.pallas_call(
        functools.partial(_sb_decode_kernel, n_pages=n_pages, page=page),
        grid_spec=grid_spec,
        out_shape=jax.ShapeDtypeStruct((S, nq, W), F32),
        compiler_params=_cparams("arbitrary"),
        name="sb_decode",
    )(page_table, q, kvn, _strict_upper(page), *([kv_pool] * n_pages))


def _gelu_tanh(x):
    return 0.5 * x * (1.0 + jnp.tanh(math.sqrt(2.0 / math.pi) * (x + 0.044715 * x * x * x)))


def _compress_body(chunk_rows, w1_ref, pos_ref, w2_ref, o_ref, n_chunks):
    n_kinds = w1_ref.shape[0]
    tiles_per_kind = NSA_KV_HEADS * HEAD_DIM // LANES
    for kind in range(n_kinds):
        xcat = jnp.concatenate(
            [jnp.concatenate([chunk_rows(l, kind * tiles_per_kind + t) for l in range(CMP_STRIDE)], axis=1)
             for t in range(tiles_per_kind)], axis=0)
        u = []
        for a in range(2):
            xa = (xcat + pos_ref[kind, a]).astype(BF16)
            u.append(jnp.dot(xa, w1_ref[kind, a], preferred_element_type=F32))
        for t in range(tiles_per_kind):
            u0 = u[0][t * n_chunks:(t + 1) * n_chunks]
            u1 = pltpu.roll(u[1][t * n_chunks:(t + 1) * n_chunks], n_chunks - 1, 0)
            hid = _gelu_tanh(u0 + u1)
            out = jnp.dot(hid.astype(BF16), w2_ref[kind], preferred_element_type=F32)
            col = (kind * tiles_per_kind + t) * LANES
            o_ref[0, :, col:col + LANES] = out


N_CMP_TILES = 2 * NSA_KV_HEADS * HEAD_DIM // LANES


def _compress_seq_kernel(*refs, n_chunks):
    x_refs = refs[:N_CMP_TILES]
    w1_ref, pos_ref, w2_ref, o_ref = refs[N_CMP_TILES:]

    def chunk_rows(l, tile):
        return x_refs[tile][0, pl.ds(l, n_chunks, stride=CMP_STRIDE), :]
    _compress_body(chunk_rows, w1_ref, pos_ref, w2_ref, o_ref, n_chunks)


def _compress_paged_kernel(pt_ref, w1_ref, pos_ref, w2_ref, *rest, n_pages, page):
    pages, o_ref, x_scr = rest[:n_pages], rest[n_pages], rest[n_pages + 1]
    for p in range(n_pages):
        for t in range(N_CMP_TILES):
            x_scr[t, p * page:(p + 1) * page, :] = pages[p][0, :, t * LANES:(t + 1) * LANES]
    n_chunks = n_pages * page // CMP_STRIDE

    def chunk_rows(l, tile):
        return x_scr[tile, pl.ds(l, n_chunks, stride=CMP_STRIDE), :]
    _compress_body(chunk_rows, w1_ref, pos_ref, w2_ref, o_ref, n_chunks)


def _compress_weights(cmp_pos, cmp_w1, cmp_w2):
    eye2 = jnp.eye(2, dtype=F32)
    w1r = cmp_w1.reshape(2, 2, CMP_STRIDE, HEAD_DIM, -1)
    w1e = jnp.einsum('kaldj,hg->kalhdgj', w1r, eye2)
    w1e = w1e.reshape(2, 2, CMP_STRIDE * LANES, 2 * cmp_w1.shape[-1]).astype(BF16)
    pos = jnp.broadcast_to(cmp_pos.reshape(2, 2, CMP_STRIDE, 1, HEAD_DIM), (2, 2, CMP_STRIDE, 2, HEAD_DIM))
    pos = pos.reshape(2, 2, 1, CMP_STRIDE * LANES).astype(F32)
    w2e = jnp.einsum('kjd,hg->khjgd', cmp_w2, eye2).reshape(2, 2 * cmp_w2.shape[1], LANES).astype(BF16)
    return w1e, pos, w2e


def compress_seq(kv, cmp_w):
    B, L, _ = kv.shape
    w1e, pos, w2e = cmp_w
    n_chunks = L // CMP_STRIDE
    W = 4 * LANES
    return pl.pallas_call(
        functools.partial(_compress_seq_kernel, n_chunks=n_chunks),
        grid=(B,),
        in_specs=[pl.BlockSpec((1, L, LANES), functools.partial(lambda b, t: (b, 0, t), t=t)) for t in range(N_CMP_TILES)]
        + [pl.BlockSpec(w1e.shape, lambda b: (0, 0, 0, 0)),
           pl.BlockSpec(pos.shape, lambda b: (0, 0, 0, 0)),
           pl.BlockSpec(w2e.shape, lambda b: (0, 0, 0))],
        out_specs=pl.BlockSpec((1, n_chunks, W), lambda b: (b, 0, 0)),
        out_shape=jax.ShapeDtypeStruct((B, n_chunks, W), F32),
        compiler_params=_cparams("parallel"),
        name="nsa_compress_seq",
    )(*([kv] * N_CMP_TILES), w1e, pos, w2e)


def compress_paged(pool, page_table, cmp_w):
    S, n_pages = page_table.shape
    page = pool.shape[1]
    w1e, pos, w2e = cmp_w
    n_chunks = n_pages * page // CMP_STRIDE
    W = 4 * LANES
    grid_spec = pltpu.PrefetchScalarGridSpec(
        num_scalar_prefetch=1,
        grid=(S,),
        in_specs=[pl.BlockSpec(w1e.shape, lambda b, pt: (0, 0, 0, 0)),
                  pl.BlockSpec(pos.shape, lambda b, pt: (0, 0, 0, 0)),
                  pl.BlockSpec(w2e.shape, lambda b, pt: (0, 0, 0))]
        + _page_specs(n_pages, (1, page, W), 0),
        out_specs=pl.BlockSpec((1, n_chunks, W), lambda b, pt: (b, 0, 0)),
        scratch_shapes=[pltpu.VMEM((N_CMP_TILES, n_pages * page, LANES), F32)],
    )
    return pl.pallas_call(
        functools.partial(_compress_paged_kernel, n_pages=n_pages, page=page),
        grid_spec=grid_spec,
        out_shape=jax.ShapeDtypeStruct((S, n_chunks, W), F32),
        compiler_params=_cparams("arbitrary"),
        name="nsa_compress_paged",
    )(page_table, w1e, pos, w2e, *([pool] * n_pages))


def _masked_softmax(s, mask):
    s = jnp.where(mask, s, NEG)
    m = jnp.max(s, axis=-1, keepdims=True)
    e = jnp.where(mask, jnp.exp(s - m), 0.0)
    return e / jnp.maximum(jnp.sum(e, axis=-1, keepdims=True), 1e-30)


def _importance_t(ov_t, psum):
    hi = psum.astype(BF16)
    lo = (psum - hi.astype(F32)).astype(BF16)
    return _nt_dot(ov_t, hi) + _nt_dot(ov_t, lo)


def _select_topk_t(imp, n_rows):
    jj = lax.broadcasted_iota(jnp.int32, imp.shape, 0)
    cnt = jnp.zeros(imp.shape, F32)
    for i in range(n_rows):
        row = imp[i:i + 1, :]
        ahead = (row > imp) | ((row == imp) & (jj > i))
        cnt = cnt + jnp.where(ahead, 1.0, 0.0)
    return jnp.where(cnt < N_SEL, 1.0, 0.0)


def _block_expand(first_block, n_blocks, tk):
    j = lax.broadcasted_iota(jnp.int32, (n_blocks, tk), 0)
    col = lax.broadcasted_iota(jnp.int32, (n_blocks, tk), 1)
    return jnp.where(j == first_block + col // SEL_BLOCK, 1.0, 0.0).astype(BF16)


def _nsa_prompt_kernel(c31_ref, q_ref, ks_ref, vs_ref, kw_ref, vw_ref, kc_ref, vc_ref, gate_ref, bc_ref, toe_ref,
                       ovt_ref, o_ref, *, tq, n_cmp):
    g = pl.program_id(1)
    qt = pl.program_id(2)
    tk = tq
    R = NSA_HPG
    par = g % 2
    lane = lax.broadcasted_iota(jnp.int32, (tq, LANES), 1)
    own = (lane // HEAD_DIM) == par
    q = q_ref[0] * HEAD_DIM ** -0.5
    parts = []
    for r in range(R):
        t = q[:, (r // 2) * LANES:(r // 2 + 1) * LANES]
        src = jnp.where(par == r % 2, t, pltpu.roll(t, HEAD_DIM, 1))
        parts.append(jnp.where(own, src, 0.0))
    q4 = jnp.concatenate(parts, axis=0).astype(BF16)
    rows = R * tq
    ri = lax.broadcasted_iota(jnp.int32, (rows, tk), 0) % tq
    ci = lax.broadcasted_iota(jnp.int32, (rows, tk), 1)
    rblk = lax.broadcasted_iota(jnp.int32, (rows, 1), 0) // tq
    c_far = jnp.zeros((rows, 1), F32)
    for r in range(R):
        c_far = jnp.where(rblk == r, c31_ref[g * R + r], c_far)
    toe0 = toe_ref[:, 0].reshape(rows, tk)
    toe1 = toe_ref[:, 1].reshape(rows, tk)

    n_c = kc_ref.shape[1]
    cc = lax.broadcasted_iota(jnp.int32, (rows, n_c), 1)
    tt = qt * tq + lax.broadcasted_iota(jnp.int32, (rows, n_c), 0) % tq
    mask_c = (cc * CMP_STRIDE + CMP_LEN - 1 <= tt) & (cc < n_cmp)
    s_c = _nt_dot(q4, kc_ref[0].astype(BF16)) + bc_ref[...].reshape(rows, n_c)
    p_c = _masked_softmax(s_c, mask_c)
    o_c = jnp.dot(p_c.astype(BF16), vc_ref[0].astype(BF16), preferred_element_type=F32)
    psum = p_c[0:tq]
    for r in range(1, R):
        psum = psum + p_c[r * tq:(r + 1) * tq]
    n_blk = ovt_ref.shape[0]
    imp = _importance_t(ovt_ref[...], psum)
    jj = lax.broadcasted_iota(jnp.int32, (n_blk, tq), 0)
    tq_pos = qt * tq + lax.broadcasted_iota(jnp.int32, (n_blk, tq), 1)
    t_blk = tq_pos // SEL_BLOCK
    forced = (jj == 0) | (jj == t_blk) | (jj == t_blk - 1)
    imp = jnp.where(jj * SEL_BLOCK <= tq_pos, imp + jnp.where(forced, SEL_FORCE, 0.0), -1.0)
    sel_t = _select_topk_t(imp, n_blk)
    sel = jnp.concatenate([sel_t, jnp.zeros((LANES - n_blk, tq), F32)], axis=0).T if n_blk < LANES else sel_t.T
    sel4 = jnp.concatenate([sel] * R, axis=0).astype(BF16)

    def tile(ref, kt):
        return ref[0, pl.ds(pl.multiple_of(kt * tk, tk), tk), :].astype(BF16)

    init = (jnp.full((rows, 1), NEG, F32), jnp.zeros((rows, 1), F32), jnp.zeros((rows, LANES), F32))

    def sel_mask(kt):
        e = _block_expand(kt * (tk // SEL_BLOCK), LANES, tk)
        return jnp.dot(sel4, e, preferred_element_type=F32) > 0.5

    def sel_body(kt, carry):
        msk = sel_mask(kt)
        bias = jnp.where(kt == qt - 1, toe1, c_far)
        s = jnp.where(msk, _nt_dot(q4, tile(ks_ref, kt)) + bias, NEG)
        return _online_softmax_step(carry, s, tile(vs_ref, kt), msk)

    carry = lax.fori_loop(0, qt, sel_body, init)
    msk = sel_mask(qt) & (ci <= ri)
    s = jnp.where(msk, _nt_dot(q4, tile(ks_ref, qt)) + toe0, NEG)
    _, l, acc = _online_softmax_step(carry, s, tile(vs_ref, qt), msk)
    o_s = acc / l

    n_back = WINDOW // tk

    def win_body(kt, carry):
        msk = (ci > ri) | (kt != qt - n_back)
        bias = jnp.where(kt == qt - 1, toe1, c_far)
        s = jnp.where(msk, _nt_dot(q4, tile(kw_ref, kt)) + bias, NEG)
        return _online_softmax_step(carry, s, tile(vw_ref, kt), msk)

    carry = lax.fori_loop(jnp.maximum(qt - n_back, 0), qt, win_body, init)
    msk = ci <= ri
    s = jnp.where(msk, _nt_dot(q4, tile(kw_ref, qt)) + toe0, NEG)
    _, l, acc = _online_softmax_step(carry, s, tile(vw_ref, qt), msk)
    o_w = acc / l

    gs = jax.nn.sigmoid(gate_ref[0])
    n_h = NSA_KV_HEADS * R
    outs = []
    for r in range(R):
        o_r = jnp.zeros((tq, LANES), F32)
        for br, o_b in enumerate((o_c, o_s, o_w)):
            gcol = jnp.sum(jnp.where(lane == br * n_h + g * R + r, gs, 0.0), axis=-1, keepdims=True)
            o_r = o_r + gcol * o_b[r * tq:(r + 1) * tq]
        outs.append(jnp.where(par == r % 2, o_r, pltpu.roll(o_r, HEAD_DIM, 1)))
    for u in range(R // 2):
        o_ref[0, :, u * LANES:(u + 1) * LANES] = jnp.where(lane < HEAD_DIM, outs[2 * u], outs[2 * u + 1])


def _overlap_t(n_blk_rows, n_cmp_cols, n_cmp, n_slc):
    c0 = np.arange(n_cmp_cols)[None, :] * CMP_STRIDE
    j0 = np.arange(n_blk_rows)[:, None] * SEL_BLOCK
    ov = (c0 < j0 + SEL_BLOCK) & (c0 + CMP_LEN > j0)
    ov &= (np.arange(n_cmp_cols)[None, :] < n_cmp) & (np.arange(n_blk_rows)[:, None] < n_slc)
    return jnp.asarray(ov, BF16)


def nsa_prompt_attention(q, kv, win, cmp, gates, rel_bias, tq=128):
    B, T, W = q.shape
    G, R = NSA_KV_HEADS, NSA_HPG
    n_cmp = (T - CMP_LEN) // CMP_STRIDE + 1
    n_c = cmp.shape[1]
    n_slc = -(-T // SEL_BLOCK)
    n_blk = -(-n_slc // 8) * 8
    toe = _toeplitz_bias(rel_bias, tq, tq, 2)
    bias_c = _dense_bias(rel_bias, np.arange(T), np.arange(n_c) * CMP_STRIDE + CMP_LEN - 1, range(G * R))
    bias_c = bias_c.reshape(G * R, T, n_c)
    c31 = rel_bias[REL_BUCKETS - 1].astype(F32)
    ovt = _overlap_t(n_blk, n_c, n_cmp, n_slc)
    pair = lambda base: (lambda b, g, i: (b, 0, base + g // 2))
    return pl.pallas_call(
        functools.partial(_nsa_prompt_kernel, tq=tq, n_cmp=n_cmp),
        grid=(B, G, T // tq),
        in_specs=[
            pl.BlockSpec(memory_space=pltpu.SMEM),
            pl.BlockSpec((1, tq, R * HEAD_DIM), lambda b, g, i: (b, i, g)),
            pl.BlockSpec((1, T, LANES), pair(4)),
            pl.BlockSpec((1, T, LANES), pair(6)),
            pl.BlockSpec((1, T, LANES), pair(0)),
            pl.BlockSpec((1, T, LANES), pair(2)),
            pl.BlockSpec((1, n_c, LANES), pair(0)),
            pl.BlockSpec((1, n_c, LANES), pair(2)),
            pl.BlockSpec((1, tq, LANES), lambda b, g, i: (b, i, 0)),
            pl.BlockSpec((R, tq, n_c), lambda b, g, i: (g, i, 0)),
            pl.BlockSpec((R, 2, tq, tq), lambda b, g, i: (g, 0, 0, 0)),
            pl.BlockSpec(ovt.shape, lambda b, g, i: (0, 0)),
        ],
        out_specs=pl.BlockSpec((1, tq, R * HEAD_DIM), lambda b, g, i: (b, i, g)),
        out_shape=jax.ShapeDtypeStruct((B, T, W), F32),
        compiler_params=_cparams("parallel", "parallel", "arbitrary"),
        name="nsa_prompt",
    )(c31, q, kv, kv, win, win, cmp, cmp, gates, bias_c, toe, ovt)


def _nsa_decode_kernel(pt_ref, q_ref, kvn_ref, wn_ref, gate_ref, cmp_ref, win_ref, bc_ref, bs_ref, bw_ref, ovt_ref,
                       *rest, n_pages, page, n_cmp):
    pages = rest[:n_pages]
    o_ref, wout_ref, s_scr = rest[n_pages], rest[n_pages + 1], rest[n_pages + 2]
    G, R = NSA_KV_HEADS, NSA_HPG
    nq = q_ref.shape[1]
    n_h = G * R
    rows = n_h * nq
    GW = G * HEAD_DIM
    P = n_pages * page
    n_win = win_ref.shape[1]
    lane8 = lax.broadcasted_iota(jnp.int32, (nq, LANES), 1)

    q = q_ref[0] * HEAD_DIM ** -0.5
    blocks = []
    for h in range(n_h):
        g = h // R
        t = q[:, (h // 2) * LANES:(h // 2 + 1) * LANES]
        if h % 2 != g % 2:
            t = pltpu.roll(t, HEAD_DIM, 1)
        t = jnp.where((lane8 // HEAD_DIM) == g % 2, t, 0.0)
        z = jnp.zeros((nq, LANES), F32)
        blocks.append(jnp.concatenate([t, z] if g // 2 == 0 else [z, t], axis=1))
    qg = jnp.concatenate(blocks, axis=0).astype(BF16)

    ri = lax.broadcasted_iota(jnp.int32, (rows, page), 0) % nq
    ci = lax.broadcasted_iota(jnp.int32, (rows, page), 1)
    new_mask = ci <= ri

    n_c = cmp_ref.shape[1]
    cmpv = cmp_ref[0]
    cc = lax.broadcasted_iota(jnp.int32, (rows, n_c), 1)
    s_c = _nt_dot(qg, cmpv[:, :GW].astype(BF16)) + bc_ref[...]
    p_c = _masked_softmax(s_c, cc < n_cmp)
    o_c = jnp.dot(p_c.astype(BF16), cmpv[:, GW:].astype(BF16), preferred_element_type=F32)
    ps = []
    for g in range(G):
        acc = p_c[g * R * nq:(g * R + 1) * nq]
        for r in range(1, R):
            acc = acc + p_c[(g * R + r) * nq:(g * R + r + 1) * nq]
        ps.append(acc)
    psum = _pad_rows(jnp.concatenate(ps, axis=0), LANES)
    n_blk = ovt_ref.shape[0]
    imp = _importance_t(ovt_ref[...], psum)
    jj = lax.broadcasted_iota(jnp.int32, (n_blk, LANES), 0)
    t_pos = P + lax.broadcasted_iota(jnp.int32, (n_blk, LANES), 1) % nq
    t_blk = t_pos // SEL_BLOCK
    forced = (jj == 0) | (jj == t_blk) | (jj == t_blk - 1)
    imp = jnp.where(jj * SEL_BLOCK <= t_pos, imp + jnp.where(forced, SEL_FORCE, 0.0), -1.0)
    sel_t = _select_topk_t(imp, n_blk)
    sel = _pad_rows(sel_t, LANES).T
    sel_rows = jnp.concatenate([sel[g * nq:(g + 1) * nq] for g in range(G) for _ in range(R)], axis=0).astype(BF16)

    per_tile = page // SEL_BLOCK
    m = jnp.full((rows, 1), NEG, F32)
    for p in range(n_pages):
        pg = pages[p][0]
        msk = jnp.dot(sel_rows, _block_expand(p * per_tile, LANES, page), preferred_element_type=F32) > 0.5
        s = jnp.where(msk, _nt_dot(qg, pg[:, :GW].astype(BF16)) + bs_ref[:, p * page:(p + 1) * page], NEG)
        s_scr[:, p * page:(p + 1) * page] = s
        m = jnp.maximum(m, jnp.max(s, axis=-1, keepdims=True))
    kvn = _pad_rows(kvn_ref[0], page)
    msk = (jnp.dot(sel_rows, _block_expand(n_pages * per_tile, LANES, page), preferred_element_type=F32) > 0.5) & new_mask
    sn = jnp.where(msk, _nt_dot(qg, kvn[:, 2 * GW:3 * GW].astype(BF16)) + bs_ref[:, P:], NEG)
    m = jnp.maximum(m, jnp.max(sn, axis=-1, keepdims=True))
    pn = jnp.where(msk, jnp.exp(sn - m), 0.0)
    l = jnp.sum(pn, axis=-1, keepdims=True)
    acc = jnp.dot(pn.astype(BF16), kvn[:, 3 * GW:].astype(BF16), preferred_element_type=F32)
    for p in range(n_pages):
        pr = jnp.exp(s_scr[:, p * page:(p + 1) * page] - m)
        l = l + jnp.sum(pr, axis=-1, keepdims=True)
        acc = acc + jnp.dot(pr.astype(BF16), pages[p][0][:, GW:].astype(BF16), preferred_element_type=F32)
    o_s = acc / l

    wb = win_ref[0]
    rw = lax.broadcasted_iota(jnp.int32, (rows, n_win), 0) % nq
    cw = lax.broadcasted_iota(jnp.int32, (rows, n_win), 1)
    mask_w = (n_win + rw - cw) < WINDOW
    s_w = jnp.where(mask_w, _nt_dot(qg, wb[:, :GW].astype(BF16)) + bw_ref[:, :n_win], NEG)
    wn = _pad_rows(wn_ref[0], page)
    s_n = jnp.where(new_mask, _nt_dot(qg, wn[:, :GW].astype(BF16)) + bw_ref[:, n_win:], NEG)
    m = jnp.maximum(jnp.max(s_w, axis=-1, keepdims=True), jnp.max(s_n, axis=-1, keepdims=True))
    p_w = jnp.where(mask_w, jnp.exp(s_w - m), 0.0)
    p_n = jnp.where(new_mask, jnp.exp(s_n - m), 0.0)
    l = jnp.sum(p_w, axis=-1, keepdims=True) + jnp.sum(p_n, axis=-1, keepdims=True)
    o_w = (jnp.dot(p_w.astype(BF16), wb[:, GW:].astype(BF16), preferred_element_type=F32)
           + jnp.dot(p_n.astype(BF16), wn[:, GW:].astype(BF16), preferred_element_type=F32)) / l

    gs = jax.nn.sigmoid(gate_ref[0])
    grep = jnp.concatenate([gs] * n_h, axis=0)
    glane = lax.broadcasted_iota(jnp.int32, (rows, LANES), 1)
    ghead = lax.broadcasted_iota(jnp.int32, (rows, LANES), 0) // nq
    o = jnp.zeros((rows, GW), F32)
    for br, o_b in enumerate((o_c, o_s, o_w)):
        o = o + jnp.sum(jnp.where(glane == br * n_h + ghead, grep, 0.0), axis=-1, keepdims=True) * o_b
    pieces = []
    for h in range(n_h):
        g = h // R
        t = o[h * nq:(h + 1) * nq, (g // 2) * LANES:(g // 2 + 1) * LANES]
        pieces.append(t if h % 2 == g % 2 else pltpu.roll(t, HEAD_DIM, 1))
    o_ref[0] = jnp.concatenate(
        [jnp.where(lane8 < HEAD_DIM, pieces[2 * u], pieces[2 * u + 1]) for u in range(n_h // 2)], axis=1)

    wout_ref[0, 0:n_win - nq, :] = wb[nq:, :]
    wout_ref[0, n_win - nq:, :] = wn_ref[0]


def nsa_decode_attention(q, kvn, wn, gates, cmp, win_buf, kv_pool, page_table, rel_bias):
    S, nq, W = q.shape
    G, R = NSA_KV_HEADS, NSA_HPG
    n_pages = page_table.shape[1]
    page = kv_pool.shape[1]
    P = n_pages * page
    n_win = win_buf.shape[1]
    assert nq < CMP_STRIDE and n_win == WINDOW and nq <= 8
    L = P + nq
    n_cmp = (L - CMP_LEN) // CMP_STRIDE + 1
    n_c = cmp.shape[1]
    n_slc = -(-L // SEL_BLOCK)
    n_blk = -(-n_slc // 8) * 8
    heads = range(G * R)
    qpos = P + np.arange(nq)
    bias_c = _dense_bias(rel_bias, qpos, np.arange(n_c) * CMP_STRIDE + CMP_LEN - 1, heads)
    bias_s = _dense_bias(rel_bias, qpos, np.arange(P + page), heads)
    bias_w = _dense_bias(rel_bias, qpos, P - n_win + np.arange(n_win + page), heads)
    ovt = _overlap_t(n_blk, n_c, n_cmp, n_slc)
    rows = G * R * nq
    const = lambda a: pl.BlockSpec(a.shape, lambda b, pt: (0,) * a.ndim)
    seq = lambda a: pl.BlockSpec((1,) + a.shape[1:], lambda b, pt: (b,) + (0,) * (a.ndim - 1))
    grid_spec = pltpu.PrefetchScalarGridSpec(
        num_scalar_prefetch=1,
        grid=(S,),
        in_specs=[seq(q), seq(kvn), seq(wn), seq(gates), seq(cmp), seq(win_buf),
                  const(bias_c), const(bias_s), const(bias_w), const(ovt)]
        + _page_specs(n_pages, (1, page, 2 * G * HEAD_DIM), 1),
        out_specs=[seq(q), seq(win_buf)],
        scratch_shapes=[pltpu.VMEM((rows, P), F32)],
    )
    return pl.pallas_call(
        functools.partial(_nsa_decode_kernel, n_pages=n_pages, page=page, n_cmp=n_cmp),
        grid_spec=grid_spec,
        out_shape=[jax.ShapeDtypeStruct(q.shape, F32), jax.ShapeDtypeStruct(win_buf.shape, F32)],
        compiler_params=_cparams("arbitrary"),
        name="nsa_decode",
    )(page_table, q, kvn, wn, gates, cmp, win_buf, bias_c, bias_s, bias_w, ovt, *([kv_pool] * n_pages))


def _split_cols(w, widths):
    out, c0 = [], 0
    for wd in widths:
        piece = w[:, c0:c0 + wd]
        if wd % LANES:
            piece = jnp.pad(piece, ((0, 0), (0, LANES - wd % LANES)))
        out.append(piece.astype(BF16))
        c0 += wd
    return out


def kernel(x_prompt, x_sample, cache_nsa_kv, state_nsa_win, cache_diff_k, cache_diff_v, cache_sb_kv, page_table, rel_bias, nsa_w_in, nsa_cmp_pos, nsa_cmp_w1, nsa_cmp_w2, nsa_w_out, diff_w_in, diff_lambda, diff_norm_g, diff_w_out, sb_w_in, sb_w_out, mlp_w_up, mlp_w_down, ln_g, ln_b):
    B, T, D = x_prompt.shape
    S, nq, _ = x_sample.shape
    depth = mlp_w_up.shape[0]
    alpha = (2 * depth) ** 0.25
    G, R, dk = NSA_KV_HEADS, NSA_HPG, HEAD_DIM
    n_pool, page = cache_nsa_kv.shape[1], cache_nsa_kv.shape[2]
    xp = x_prompt.reshape(B * T, D)
    xs = x_sample.reshape(S * nq, D)
    res = {k: [] for k in ("nsa_kv_p", "nsa_kv_s", "nsa_win_p", "nsa_win_s", "diff_k_p", "diff_k_s",
                           "diff_v_p", "diff_v_s", "sb_kv_p", "sb_kv_s")}
    for i in range(depth):
        kind, j = i % N_MIXERS, i // N_MIXERS
        if kind == 0:
            ws = _split_cols(nsa_w_in[j], (G * R * dk, 4 * G * dk, 2 * G * dk, 3 * G * R))
            cmp_w = _compress_weights(nsa_cmp_pos[j], nsa_cmp_w1[j], nsa_cmp_w2[j])
            q, kv, win, gates = [a.reshape(B, T, -1) for a in multi_linear(xp, ws)]
            op = nsa_prompt_attention(q, kv, win, compress_seq(kv, cmp_w), gates, rel_bias)
            qs, kvs, wns, gts = [a.reshape(S, nq, -1) for a in multi_linear(xs, ws)]
            pool = cache_nsa_kv[j].reshape(n_pool, page, 4 * G * dk)
            win_buf = state_nsa_win[j].reshape(S, -1, 2 * G * dk)
            os_, win_s = nsa_decode_attention(qs, kvs, wns, gts, compress_paged(pool, page_table, cmp_w), win_buf,
                                              pool, page_table, rel_bias)
            n_keep = min(WINDOW, T)
            res["nsa_kv_p"].append(kv.reshape(B, T, 4, G, dk))
            res["nsa_kv_s"].append(kvs.reshape(S, nq, 4, G, dk))
            res["nsa_win_p"].append(win[:, T - n_keep:].reshape(B, n_keep, 2, G, dk))
            res["nsa_win_s"].append(win_s.reshape(S, -1, 2, G, dk))
            w_out = nsa_w_out[j]
        elif kind == 1:
            lam_init = 0.8 - 0.6 * math.exp(-0.3 * i)
            Hd = D // (2 * dk)
            ws = _split_cols(diff_w_in[j], (D, D, D))
            q, k, v = [a.reshape(B, T, D) for a in multi_linear(xp, ws)]
            op = diff_prompt_attention(q, k, v, rel_bias, diff_lambda[j], diff_norm_g[j], lam_init)
            qs, ksn, vsn = [a.reshape(S, nq, D) for a in multi_linear(xs, ws)]
            os_ = diff_decode_attention(qs, ksn, vsn, cache_diff_k[j].reshape(n_pool, page, D),
                                        cache_diff_v[j].reshape(n_pool, page, D), page_table, rel_bias,
                                        diff_lambda[j], diff_norm_g[j], lam_init)
            res["diff_k_p"].append(k.reshape(B, T, Hd, 2, dk))
            res["diff_k_s"].append(ksn.reshape(S, nq, Hd, 2, dk))
            res["diff_v_p"].append(v.reshape(B, T, Hd, 2 * dk))
            res["diff_v_s"].append(vsn.reshape(S, nq, Hd, 2 * dk))
            w_out = diff_w_out[j]
        else:
            H = D // dk
            ws = _split_cols(sb_w_in[j], (D, 2 * D))
            q, kv = [a.reshape(B, T, -1) for a in multi_linear(xp, ws)]
            op = sb_prompt_attention(q, kv)
            qs, kvs = [a.reshape(S, nq, -1) for a in multi_linear(xs, ws)]
            os_ = sb_decode_attention(qs, kvs, cache_sb_kv[j].reshape(n_pool, page, 2 * D), page_table)
            res["sb_kv_p"].append(kv.reshape(B, T, 2, H, dk))
            res["sb_kv_s"].append(kvs.reshape(S, nq, 2, H, dk))
            w_out = sb_w_out[j]
        ln = jnp.stack([ln_g[i, 0], ln_b[i, 0], ln_g[i, 1], ln_b[i, 1]])
        w_out, w_up, w_down = w_out.astype(BF16), mlp_w_up[i].astype(BF16), mlp_w_down[i].astype(BF16)
        xp = post_mixer(op.reshape(B * T, D), w_out, xp, ln, w_up, w_down, alpha)
        xs = post_mixer(os_.reshape(S * nq, D), w_out, xs, ln, w_up, w_down, alpha)
    return (xp.reshape(B, T, D), xs.reshape(S, nq, D),
            jnp.stack(res["nsa_kv_p"]), jnp.stack(res["nsa_kv_s"]), jnp.stack(res["nsa_win_p"]),
            jnp.stack(res["nsa_win_s"]), jnp.stack(res["diff_k_p"]), jnp.stack(res["diff_k_s"]),
            jnp.stack(res["diff_v_p"]), jnp.stack(res["diff_v_s"]), jnp.stack(res["sb_kv_p"]),
            jnp.stack(res["sb_kv_s"]))
```

```python
import functools
import math

import numpy as np
import jax
import jax.numpy as jnp
from jax import lax
from jax.experimental import pallas as pl
from jax.experimental.pallas import tpu as pltpu

F32 = jnp.float32
BF16 = jnp.bfloat16

HEAD_DIM = 64
NSA_KV_HEADS = 4
NSA_HPG = 4
CMP_LEN = 32
CMP_STRIDE = 16
SEL_BLOCK = 64
N_SEL = 16
SEL_FORCE = 1000.0
WINDOW = 512
REL_BUCKETS = 32
REL_MAX_DIST = 128
LN_EPS = 1e-5
NEG = -1e30
N_MIXERS = 3
SB_DEAD = 104.0

LANES = 128
VMEM_LIMIT = 48 * 1024 * 1024


def _cparams(*sem):
    return pltpu.CompilerParams(dimension_semantics=sem, vmem_limit_bytes=VMEM_LIMIT)


def _t5_bucket_np(dist):
    n = np.maximum(dist, 0)
    max_exact = REL_BUCKETS // 2
    nf = np.maximum(n, max_exact).astype(np.float32)
    large = max_exact + (np.log(nf / np.float32(max_exact)) / np.float32(math.log(REL_MAX_DIST / max_exact))
                         * np.float32(REL_BUCKETS - max_exact)).astype(np.int32)
    return np.where(n < max_exact, n, np.minimum(large, REL_BUCKETS - 1)).astype(np.int32)


FAR_DIST = int(np.min(np.nonzero(_t5_bucket_np(np.arange(4 * REL_MAX_DIST)) == REL_BUCKETS - 1)[0]))


def _bias_rows(rel_bias, dist):
    idx = _t5_bucket_np(dist)
    flat = jnp.take(rel_bias.astype(F32).T, jnp.asarray(idx.reshape(-1)), axis=1)
    return flat.reshape((rel_bias.shape[1],) + idx.shape)


def _toeplitz_bias(rel_bias, tq, tk, n_off):
    i = np.arange(tq)[None, :, None]
    j = np.arange(tk)[None, None, :]
    off = np.arange(n_off)[:, None, None]
    return _bias_rows(rel_bias, off * tk + i - j)


def _dense_bias(rel_bias, qpos, kpos, cols):
    tab = _bias_rows(rel_bias, qpos[:, None] - kpos[None, :])[np.asarray(cols)]
    return tab.reshape(len(cols) * len(qpos), len(kpos))


def _layer_norm(z, g, b):
    mu = jnp.mean(z, axis=-1, keepdims=True)
    zc = z - mu
    var = jnp.mean(zc * zc, axis=-1, keepdims=True)
    return zc * lax.rsqrt(var + LN_EPS) * g + b


def _nt_dot(a, b):
    return lax.dot_general(a, b, (((1,), (1,)), ((), ())), preferred_element_type=F32)


def _project_kernel(*refs, n_row, n_t):
    x_ref = refs[0]
    row_w, t_w = refs[1:1 + n_row], refs[1 + n_row:1 + n_row + n_t]
    row_o, t_o = refs[1 + n_row + n_t:1 + 2 * n_row + n_t], refs[1 + 2 * n_row + n_t:]
    xb = x_ref[...].astype(BF16)
    for w_ref, o_ref in zip(row_w, row_o):
        o_ref[...] = jnp.dot(xb, w_ref[...], preferred_element_type=F32)
    for w_ref, o_ref in zip(t_w, t_o):
        o_ref[...] = _nt_dot(w_ref[...], xb)


def project(x, row_ws, t_ws, n_seq, tm):
    M, K = x.shape
    T = M // n_seq
    nt = T // tm
    n_row, n_t = len(row_ws), len(t_ws)
    const = lambda w: pl.BlockSpec(w.shape, lambda b, i: (0, 0))
    return pl.pallas_call(
        functools.partial(_project_kernel, n_row=n_row, n_t=n_t),
        grid=(n_seq, nt),
        in_specs=[pl.BlockSpec((tm, K), lambda b, i: (b * nt + i, 0))] + [const(w) for w in row_ws + t_ws],
        out_specs=[pl.BlockSpec((tm, w.shape[1]), lambda b, i: (b * nt + i, 0)) for w in row_ws]
        + [pl.BlockSpec((None, w.shape[0], tm), lambda b, i: (b, 0, i)) for w in t_ws],
        out_shape=[jax.ShapeDtypeStruct((M, w.shape[1]), F32) for w in row_ws]
        + [jax.ShapeDtypeStruct((n_seq, w.shape[0], T), F32) for w in t_ws],
        compiler_params=_cparams("parallel", "parallel"),
        name="project",
    )(x, *row_ws, *t_ws)


def _post_mixer_kernel(o_ref, wo_ref, x_ref, ln_ref, wu_ref, wd_ref, y_ref, x1_scr, xb_scr, acc_scr, *, alpha):
    f = pl.program_id(1)

    @pl.when(f == 0)
    def _():
        y = jnp.dot(o_ref[...].astype(BF16), wo_ref[...], preferred_element_type=F32)
        x1 = _layer_norm(alpha * x_ref[...] + y, ln_ref[0:1, :], ln_ref[1:2, :])
        x1_scr[...] = x1
        xb_scr[...] = x1.astype(BF16)
        acc_scr[...] = jnp.zeros_like(acc_scr)

    h = jnp.dot(xb_scr[...], wu_ref[...], preferred_element_type=F32)
    h = jnp.square(jnp.maximum(h, 0.0))
    acc_scr[...] += jnp.dot(h.astype(BF16), wd_ref[...], preferred_element_type=F32)

    @pl.when(f == pl.num_programs(1) - 1)
    def _():
        y_ref[...] = _layer_norm(alpha * x1_scr[...] + acc_scr[...], ln_ref[2:3, :], ln_ref[3:4, :])


def post_mixer(o, w_out, x, ln, w_up, w_down, alpha, tm=512, tf=1024):
    M, D = x.shape
    Fd = w_up.shape[1]
    return pl.pallas_call(
        functools.partial(_post_mixer_kernel, alpha=alpha),
        grid=(M // tm, Fd // tf),
        in_specs=[
            pl.BlockSpec((tm, D), lambda i, f: (i, 0)),
            pl.BlockSpec((D, D), lambda i, f: (0, 0)),
            pl.BlockSpec((tm, D), lambda i, f: (i, 0)),
            pl.BlockSpec((4, D), lambda i, f: (0, 0)),
            pl.BlockSpec((D, tf), lambda i, f: (0, f)),
            pl.BlockSpec((tf, D), lambda i, f: (f, 0)),
        ],
        out_specs=pl.BlockSpec((tm, D), lambda i, f: (i, 0)),
        out_shape=jax.ShapeDtypeStruct((M, D), F32),
        scratch_shapes=[pltpu.VMEM((tm, D), F32), pltpu.VMEM((tm, D), BF16), pltpu.VMEM((tm, D), F32)],
        compiler_params=_cparams("parallel", "arbitrary"),
        name="post_mixer",
    )(o, w_out, x, ln, w_up, w_down)


def _online_softmax_step(carry, s, pv, mask=None):
    m, l, acc = carry
    m_new = jnp.maximum(m, jnp.max(s, axis=-1, keepdims=True))
    alpha = jnp.exp(m - m_new)
    p = jnp.exp(s - m_new)
    if mask is not None:
        p = jnp.where(mask, p, 0.0)
    l = alpha * l + jnp.sum(p, axis=-1, keepdims=True)
    acc = alpha * acc + pv(p.astype(BF16))
    return m_new, l, acc


def _softmax_init(rows, width):
    return jnp.full((rows, 1), NEG, F32), jnp.zeros((rows, 1), F32), jnp.zeros((rows, width), F32)


def _split_heads(q, tq):
    lane = lax.broadcasted_iota(jnp.int32, (tq, LANES), 1)
    lo = jnp.where(lane < HEAD_DIM, q, 0.0)
    hi = jnp.where(lane >= HEAD_DIM, q, 0.0)
    return jnp.concatenate([lo, hi], axis=0).astype(BF16)


def _cols(ref, start, width):
    return ref[:, pl.ds(pl.multiple_of(start, width), width)].astype(BF16)


def _rows(ref, start, height):
    return ref[pl.ds(pl.multiple_of(start, height), height), :].astype(BF16)


def _rows_blockdiag(q, n_blk, blk_w):
    rep = jnp.concatenate([q] * n_blk, axis=0)
    r = lax.broadcasted_iota(jnp.int32, rep.shape, 0) // q.shape[0]
    c = lax.broadcasted_iota(jnp.int32, rep.shape, 1) // blk_w
    return jnp.where(r == c, rep, 0.0).astype(BF16)


def _pad_rows(a, rows):
    return jnp.concatenate([a, jnp.zeros((rows - a.shape[0], a.shape[1]), a.dtype)], axis=0)


def _page_specs(n_pages, block, index):
    return [pl.BlockSpec(block, functools.partial(lambda b, pt, p: index(pt[b, p]), p=p)) for p in range(n_pages)]


def _diff_lambda(lam_ref, lam_init):
    lf = lam_ref[...]
    a = jnp.sum(lf[0:1, :] * lf[1:2, :], axis=-1, keepdims=True)
    b = jnp.sum(lf[2:3, :] * lf[3:4, :], axis=-1, keepdims=True)
    return jnp.exp(a) - jnp.exp(b) + lam_init


def _diff_finish(o1, o2, lam_full, g, lam_init):
    o = o1 - lam_full * o2
    o = o * lax.rsqrt(jnp.mean(o * o, axis=-1, keepdims=True) + LN_EPS) * g
    return o * (1.0 - lam_init)


def _diff_prompt_kernel(c31_ref, q_ref, kt_ref, v_ref, toe_ref, lam_ref, g_ref, o_ref, *, tq, n_heads, lam_init):
    h = pl.program_id(1)
    qt = pl.program_id(2)
    tk = tq
    rows = 2 * tq
    q2 = _split_heads(q_ref[...] * HEAD_DIM ** -0.5, tq)
    row = lax.broadcasted_iota(jnp.int32, (rows, 1), 0)
    c_far = jnp.where(row < tq, c31_ref[h], c31_ref[n_heads + h])

    def step(kt, carry, bias, mask=None):
        s = jnp.dot(q2, _cols(kt_ref, kt * tk, tk), preferred_element_type=F32) + bias
        if mask is not None:
            s = jnp.where(mask, s, NEG)
        v = _rows(v_ref, kt * tk, tk)
        return _online_softmax_step(carry, s, lambda p: jnp.dot(p, v, preferred_element_type=F32), mask)

    carry = lax.fori_loop(0, jnp.maximum(qt - 1, 0), lambda kt, c: step(kt, c, c_far), _softmax_init(rows, LANES))
    carry = lax.fori_loop(jnp.maximum(qt - 1, 0), qt, lambda kt, c: step(kt, c, toe_ref[1]), carry)
    ri = lax.broadcasted_iota(jnp.int32, (rows, tk), 0) % tq
    ci = lax.broadcasted_iota(jnp.int32, (rows, tk), 1)
    _, l, acc = step(qt, carry, toe_ref[0], ci <= ri)
    o = acc / l
    o_ref[...] = _diff_finish(o[:tq], o[tq:], _diff_lambda(lam_ref, lam_init), g_ref[...], lam_init)


def diff_prompt_attention(q, kt, v, rel_bias, lam, norm_g, lam_init, tq=256):
    B, T, W = q.shape
    n_heads = W // LANES
    toe = _toeplitz_bias(rel_bias, tq, tq, 2)
    toe = jnp.concatenate([toe[:n_heads], toe[n_heads:]], axis=2)
    c31 = rel_bias[REL_BUCKETS - 1].astype(F32)
    return pl.pallas_call(
        functools.partial(_diff_prompt_kernel, tq=tq, n_heads=n_heads, lam_init=lam_init),
        grid=(B, n_heads, T // tq),
        in_specs=[
            pl.BlockSpec(memory_space=pltpu.SMEM),
            pl.BlockSpec((None, tq, LANES), lambda b, h, i: (b, i, h)),
            pl.BlockSpec((None, LANES, T), lambda b, h, i: (b, h, 0)),
            pl.BlockSpec((None, T, LANES), lambda b, h, i: (b, 0, h)),
            pl.BlockSpec((None, 2, 2 * tq, tq), lambda b, h, i: (h, 0, 0, 0)),
            pl.BlockSpec((4, HEAD_DIM), lambda b, h, i: (0, 0)),
            pl.BlockSpec((1, LANES), lambda b, h, i: (0, 0)),
        ],
        out_specs=pl.BlockSpec((None, tq, LANES), lambda b, h, i: (b, i, h)),
        out_shape=jax.ShapeDtypeStruct((B, T, W), F32),
        compiler_params=_cparams("parallel", "parallel", "arbitrary"),
        name="diff_prompt",
    )(c31, q, kt, v, toe, lam, norm_g.reshape(1, LANES))


def _diff_decode_kernel(pt_ref, q_ref, kn_ref, vn_ref, bias_ref, lam_ref, g_ref, *rest, n_pages, page, lam_init):
    kp, vp = rest[:n_pages], rest[n_pages:2 * n_pages]
    o_ref, s_scr = rest[2 * n_pages], rest[2 * n_pages + 1]
    nq, W = q_ref.shape
    n_heads = W // LANES
    hr = 2 * nq
    rows = n_heads * hr
    qbd = _rows_blockdiag(q_ref[...] * HEAD_DIM ** -0.5, 2 * n_heads, HEAD_DIM)

    def pv(p, v_of_head):
        return jnp.concatenate(
            [jnp.dot(p[h * hr:(h + 1) * hr], v_of_head(h), preferred_element_type=F32) for h in range(n_heads)], axis=0)

    m = jnp.full((rows, 1), NEG, F32)
    for p in range(n_pages):
        kt = kp[p][...].reshape(W, page).astype(BF16)
        s = jnp.dot(qbd, kt, preferred_element_type=F32) + bias_ref[:, p * page:(p + 1) * page]
        s_scr[:, p * page:(p + 1) * page] = s
        m = jnp.maximum(m, jnp.max(s, axis=-1, keepdims=True))
    ri = lax.broadcasted_iota(jnp.int32, (rows, page), 0) % nq
    ci = lax.broadcasted_iota(jnp.int32, (rows, page), 1)
    mask = ci <= ri
    kn = _pad_rows(kn_ref[...], page).astype(BF16)
    vn = _pad_rows(vn_ref[...], page).astype(BF16)
    sn = jnp.where(mask, _nt_dot(qbd, kn) + bias_ref[:, n_pages * page:], NEG)
    m = jnp.maximum(m, jnp.max(sn, axis=-1, keepdims=True))
    pn = jnp.where(mask, jnp.exp(sn - m), 0.0)
    l = jnp.sum(pn, axis=-1, keepdims=True)
    acc = pv(pn.astype(BF16), lambda h: vn[:, h * LANES:(h + 1) * LANES])
    for p in range(n_pages):
        pr = jnp.exp(s_scr[:, p * page:(p + 1) * page] - m)
        l = l + jnp.sum(pr, axis=-1, keepdims=True)
        acc = acc + pv(pr.astype(BF16), lambda h: vp[p][pl.ds(h, page, stride=n_heads), :].astype(BF16))
    o = acc / l
    lam_full = _diff_lambda(lam_ref, lam_init)
    outs = [_diff_finish(o[h * hr:h * hr + nq], o[h * hr + nq:(h + 1) * hr], lam_full, g_ref[...], lam_init)
            for h in range(n_heads)]
    o_ref[...] = jnp.concatenate(outs, axis=1)


def diff_decode_attention(q, kn, vn, kt_pool, v_pool, page_table, rel_bias, lam, norm_g, lam_init):
    S, nq, W = q.shape
    n_pages = page_table.shape[1]
    page = kt_pool.shape[-1]
    n_heads = W // LANES
    P = n_pages * page
    cols = [mp * n_heads + h for h in range(n_heads) for mp in range(2)]
    bias = _dense_bias(rel_bias, P + np.arange(nq), np.arange(P + page), cols)
    rows = bias.shape[0]
    seq = pl.BlockSpec((None, nq, W), lambda b, pt: (b, 0, 0))
    grid_spec = pltpu.PrefetchScalarGridSpec(
        num_scalar_prefetch=1,
        grid=(S,),
        in_specs=[seq, seq, seq,
                  pl.BlockSpec(bias.shape, lambda b, pt: (0, 0)),
                  pl.BlockSpec((4, HEAD_DIM), lambda b, pt: (0, 0)),
                  pl.BlockSpec((1, LANES), lambda b, pt: (0, 0))]
        + _page_specs(n_pages, (None,) + kt_pool.shape[1:], lambda pg: (pg, 0, 0, 0, 0))
        + _page_specs(n_pages, (None,) + v_pool.shape[1:], lambda pg: (pg, 0, 0)),
        out_specs=seq,
        scratch_shapes=[pltpu.VMEM((rows, P), F32)],
    )
    return pl.pallas_call(
        functools.partial(_diff_decode_kernel, n_pages=n_pages, page=page, lam_init=lam_init),
        grid_spec=grid_spec,
        out_shape=jax.ShapeDtypeStruct((S, nq, W), F32),
        compiler_params=_cparams("arbitrary"),
        name="diff_decode",
    )(page_table, q, kn, vn, bias, lam, norm_g.reshape(1, LANES), *([kt_pool] * n_pages), *([v_pool] * n_pages))


def _softplus(z):
    return jnp.maximum(z, 0.0) + jnp.log(1.0 + jnp.exp(-jnp.abs(z)))


def _suffix_sum(c, u):
    hi = c.astype(BF16)
    lo = (c - hi.astype(F32)).astype(BF16)
    return jnp.dot(hi, u, preferred_element_type=F32) + jnp.dot(lo, u, preferred_element_type=F32)


def _sb_step(carry, z, av, u, mask=None):
    r, acc = carry
    c = _softplus(z)
    if mask is not None:
        c = jnp.where(mask, c, 0.0)
    a = jnp.exp(z - c - (_suffix_sum(c, u) + r))
    if mask is not None:
        a = jnp.where(mask, a, 0.0)
    return r + jnp.sum(c, axis=-1, keepdims=True), acc + av(a.astype(BF16))


def _strict_upper(n):
    return jnp.asarray(np.arange(n)[:, None] > np.arange(n)[None, :], BF16)


def _sb_prompt_kernel(q_ref, kt_ref, vt_ref, u_ref, o_ref, *, tq, tk):
    qt = pl.program_id(2)
    n_sub = tq // tk
    rows = 2 * tq
    q2 = _split_heads(q_ref[...] * HEAD_DIM ** -0.5, tq)
    u = u_ref[...]

    def block(kt, carry, diag):
        for sub in reversed(range(n_sub)):
            c0 = kt * tq + sub * tk
            mask = None
            if diag:
                ri = lax.broadcasted_iota(jnp.int32, (rows, tk), 0) % tq
                ci = lax.broadcasted_iota(jnp.int32, (rows, tk), 1) + sub * tk
                mask = ci < ri
            vt = _cols(vt_ref, c0, tk)
            carry = _sb_step(carry, jnp.dot(q2, _cols(kt_ref, c0, tk), preferred_element_type=F32),
                             lambda a: _nt_dot(a, vt), u, mask)
        return carry

    r, acc = block(qt, (jnp.zeros((rows, 1), F32), jnp.zeros((rows, LANES), F32)), True)

    def cond(c):
        return jnp.logical_and(c[0] >= 0, jnp.min(c[1]) < SB_DEAD)

    def body(c):
        r, acc = block(c[0], (c[1], c[2]), False)
        return c[0] - 1, r, acc

    _, _, acc = lax.while_loop(cond, body, (qt - 1, r, acc))
    lane = lax.broadcasted_iota(jnp.int32, (tq, LANES), 1)
    o_ref[...] = jnp.where(lane < HEAD_DIM, acc[:tq], acc[tq:])


def sb_prompt_attention(q, kvt, tq=512, tk=256):
    B, T, W = q.shape
    n_pairs = W // LANES
    return pl.pallas_call(
        functools.partial(_sb_prompt_kernel, tq=tq, tk=tk),
        grid=(B, n_pairs, T // tq),
        in_specs=[
            pl.BlockSpec((None, tq, LANES), lambda b, h, i: (b, i, h)),
            pl.BlockSpec((None, LANES, T), lambda b, h, i: (b, h, 0)),
            pl.BlockSpec((None, LANES, T), lambda b, h, i: (b, n_pairs + h, 0)),
            pl.BlockSpec((tk, tk), lambda b, h, i: (0, 0)),
        ],
        out_specs=pl.BlockSpec((None, tq, LANES), lambda b, h, i: (b, i, h)),
        out_shape=jax.ShapeDtypeStruct((B, T, W), F32),
        compiler_params=_cparams("parallel", "parallel", "arbitrary"),
        name="sb_prompt",
    )(q, kvt, kvt, _strict_upper(tk))


def _sb_decode_kernel(pt_ref, q_ref, kvn_ref, u_ref, *rest, n_pages, page):
    pages, o_ref = rest[:n_pages], rest[n_pages]
    nq, W = q_ref.shape
    n_heads = W // HEAD_DIM
    rows = n_heads * nq
    u = u_ref[...]
    qbd = _rows_blockdiag(q_ref[...] * HEAD_DIM ** -0.5, n_heads, HEAD_DIM)
    kvn = _pad_rows(kvn_ref[...], page).astype(BF16)
    ri = lax.broadcasted_iota(jnp.int32, (rows, page), 0) % nq
    ci = lax.broadcasted_iota(jnp.int32, (rows, page), 1)
    init = (jnp.zeros((rows, 1), F32), jnp.zeros((rows, W), F32))
    carry = _sb_step(init, _nt_dot(qbd, kvn[:, :W]),
                     lambda a: jnp.dot(a, kvn[:, W:], preferred_element_type=F32), u, ci < ri)
    for p in reversed(range(n_pages)):
        kt = pages[p][0].reshape(W, page).astype(BF16)
        vt = pages[p][1].reshape(W, page).astype(BF16)
        carry = _sb_step(carry, jnp.dot(qbd, kt, preferred_element_type=F32), lambda a: _nt_dot(a, vt), u)
    acc = carry[1]
    col = lax.broadcasted_iota(jnp.int32, (nq, W), 1) // HEAD_DIM
    o = jnp.zeros((nq, W), F32)
    for h in range(n_heads):
        o = o + jnp.where(col == h, acc[h * nq:(h + 1) * nq, :], 0.0)
    o_ref[...] = o


def sb_decode_attention(q, kvn, kvt_pool, page_table):
    S, nq, W = q.shape
    n_pages = page_table.shape[1]
    page = kvt_pool.shape[-1]
    grid_spec = pltpu.PrefetchScalarGridSpec(
        num_scalar_prefetch=1,
        grid=(S,),
        in_specs=[pl.BlockSpec((None, nq, W), lambda b, pt: (b, 0, 0)),
                  pl.BlockSpec((None, nq, 2 * W), lambda b, pt: (b, 0, 0)),
                  pl.BlockSpec((page, page), lambda b, pt: (0, 0))]
        + _page_specs(n_pages, (None,) + kvt_pool.shape[1:], lambda pg: (pg, 0, 0, 0, 0)),
        out_specs=pl.BlockSpec((None, nq, W), lambda b, pt: (b, 0, 0)),
    )
    return pl.pallas_call(
        functools.partial(_sb_decode_kernel, n_pages=n_pages, page=page),
        grid_spec=grid_spec,
        out_shape=jax.ShapeDtypeStruct((S, nq, W), F32),
        compiler_params=_cparams("arbitrary"),
        name="sb_decode",
    )(page_table, q, kvn, _strict_upper(page), *([kvt_pool] * n_pages))


N_CMP_TILES = 2 * NSA_KV_HEADS * HEAD_DIM // LANES


def _gelu_tanh(x):
    return 0.5 * x * (1.0 + jnp.tanh(math.sqrt(2.0 / math.pi) * (x + 0.044715 * x * x * x)))


def _compress_body(chunk_rows, w1_ref, pos_ref, w2_ref, o_ref, n_chunks):
    n_kinds = w1_ref.shape[0]
    tiles_per_kind = NSA_KV_HEADS * HEAD_DIM // LANES
    for kind in range(n_kinds):
        xcat = jnp.concatenate(
            [jnp.concatenate([chunk_rows(l, kind * tiles_per_kind + t) for l in range(CMP_STRIDE)], axis=1)
             for t in range(tiles_per_kind)], axis=0)
        u = []
        for a in range(2):
            xa = (xcat + pos_ref[kind, a]).astype(BF16)
            u.append(jnp.dot(xa, w1_ref[kind, a], preferred_element_type=F32))
        for t in range(tiles_per_kind):
            u0 = u[0][t * n_chunks:(t + 1) * n_chunks]
            u1 = pltpu.roll(u[1][t * n_chunks:(t + 1) * n_chunks], n_chunks - 1, 0)
            hid = _gelu_tanh(u0 + u1)
            out = jnp.dot(hid.astype(BF16), w2_ref[kind], preferred_element_type=F32)
            col = (kind * tiles_per_kind + t) * LANES
            o_ref[:, col:col + LANES] = out


def _compress_seq_kernel(*refs, n_chunks):
    x_refs = refs[:N_CMP_TILES]
    w1_ref, pos_ref, w2_ref, o_ref = refs[N_CMP_TILES:]

    def chunk_rows(l, tile):
        return x_refs[tile][pl.ds(l, n_chunks, stride=CMP_STRIDE), :]
    _compress_body(chunk_rows, w1_ref, pos_ref, w2_ref, o_ref, n_chunks)


def _compress_paged_kernel(pt_ref, w1_ref, pos_ref, w2_ref, *rest, n_pages, page):
    pages, o_ref, x_scr = rest[:n_pages], rest[n_pages], rest[n_pages + 1]
    tiles_per_kind = NSA_KV_HEADS * HEAD_DIM // LANES
    for p in range(n_pages):
        for t in range(N_CMP_TILES):
            kind, pair = t // tiles_per_kind, t % tiles_per_kind
            xt = pages[p][kind, 2 * pair:2 * pair + 2].reshape(LANES, page)
            x_scr[t, p * page:(p + 1) * page, :] = xt.T
    n_chunks = n_pages * page // CMP_STRIDE

    def chunk_rows(l, tile):
        return x_scr[tile, pl.ds(l, n_chunks, stride=CMP_STRIDE), :]
    _compress_body(chunk_rows, w1_ref, pos_ref, w2_ref, o_ref, n_chunks)


def _compress_weights(cmp_pos, cmp_w1, cmp_w2):
    eye2 = jnp.eye(2, dtype=F32)
    w1r = cmp_w1.reshape(2, 2, CMP_STRIDE, HEAD_DIM, -1)
    w1e = jnp.einsum('kaldj,hg->kalhdgj', w1r, eye2)
    w1e = w1e.reshape(2, 2, CMP_STRIDE * LANES, 2 * cmp_w1.shape[-1]).astype(BF16)
    pos = jnp.broadcast_to(cmp_pos.reshape(2, 2, CMP_STRIDE, 1, HEAD_DIM), (2, 2, CMP_STRIDE, 2, HEAD_DIM))
    pos = pos.reshape(2, 2, 1, CMP_STRIDE * LANES).astype(F32)
    w2e = jnp.einsum('kjd,hg->khjgd', cmp_w2, eye2).reshape(2, 2 * cmp_w2.shape[1], LANES).astype(BF16)
    return w1e, pos, w2e


def compress_seq(kv, cmp_w):
    B, L, _ = kv.shape
    w1e, pos, w2e = cmp_w
    n_chunks = L // CMP_STRIDE
    W = N_CMP_TILES * LANES
    return pl.pallas_call(
        functools.partial(_compress_seq_kernel, n_chunks=n_chunks),
        grid=(B,),
        in_specs=[pl.BlockSpec((None, L, LANES), functools.partial(lambda b, t: (b, 0, t), t=t)) for t in range(N_CMP_TILES)]
        + [pl.BlockSpec(w1e.shape, lambda b: (0, 0, 0, 0)),
           pl.BlockSpec(pos.shape, lambda b: (0, 0, 0, 0)),
           pl.BlockSpec(w2e.shape, lambda b: (0, 0, 0))],
        out_specs=pl.BlockSpec((None, n_chunks, W), lambda b: (b, 0, 0)),
        out_shape=jax.ShapeDtypeStruct((B, n_chunks, W), F32),
        compiler_params=_cparams("parallel"),
        name="nsa_compress_seq",
    )(*([kv] * N_CMP_TILES), w1e, pos, w2e)


def compress_paged(pool_t, layer, page_table, cmp_w):
    S, n_pages = page_table.shape
    page = pool_t.shape[-1]
    w1e, pos, w2e = cmp_w
    n_chunks = n_pages * page // CMP_STRIDE
    W = N_CMP_TILES * LANES
    blk = (None, None, 2) + pool_t.shape[3:]
    grid_spec = pltpu.PrefetchScalarGridSpec(
        num_scalar_prefetch=1,
        grid=(S,),
        in_specs=[pl.BlockSpec(w1e.shape, lambda b, pt: (0, 0, 0, 0)),
                  pl.BlockSpec(pos.shape, lambda b, pt: (0, 0, 0, 0)),
                  pl.BlockSpec(w2e.shape, lambda b, pt: (0, 0, 0))]
        + _page_specs(n_pages, blk, lambda pg: (layer, pg, 0, 0, 0, 0)),
        out_specs=pl.BlockSpec((None, n_chunks, W), lambda b, pt: (b, 0, 0)),
        scratch_shapes=[pltpu.VMEM((N_CMP_TILES, n_pages * page, LANES), F32)],
    )
    return pl.pallas_call(
        functools.partial(_compress_paged_kernel, n_pages=n_pages, page=page),
        grid_spec=grid_spec,
        out_shape=jax.ShapeDtypeStruct((S, n_chunks, W), F32),
        compiler_params=_cparams("arbitrary"),
        name="nsa_compress_paged",
    )(page_table, w1e, pos, w2e, *([pool_t] * n_pages))


def _masked_softmax(s, mask):
    s = jnp.where(mask, s, NEG)
    m = jnp.max(s, axis=-1, keepdims=True)
    e = jnp.where(mask, jnp.exp(s - m), 0.0)
    return e / jnp.maximum(jnp.sum(e, axis=-1, keepdims=True), 1e-30)


def _importance_t(ov_t, psum):
    hi = psum.astype(BF16)
    lo = (psum - hi.astype(F32)).astype(BF16)
    return _nt_dot(ov_t, hi) + _nt_dot(ov_t, lo)


def _select_topk_t(imp, n_rows):
    jj = lax.broadcasted_iota(jnp.int32, imp.shape, 0)
    cnt = jnp.zeros(imp.shape, F32)
    for i in range(n_rows):
        row = imp[i:i + 1, :]
        ahead = (row > imp) | ((row == imp) & (jj > i))
        cnt = cnt + jnp.where(ahead, 1.0, 0.0)
    return jnp.where(cnt < N_SEL, 1.0, 0.0)


def _block_expand(first_block, n_blocks, tk):
    j = lax.broadcasted_iota(jnp.int32, (n_blocks, tk), 0)
    col = lax.broadcasted_iota(jnp.int32, (n_blocks, tk), 1)
    return jnp.where(j == first_block + col // SEL_BLOCK, 1.0, 0.0).astype(BF16)


CMP_NEAR_BACK = -(-(CMP_LEN - 1 + FAR_DIST) // CMP_STRIDE)


def _nsa_prompt_kernel(c31_ref, q_ref, kst_ref, vst_ref, kwt_ref, vwt_ref, kc_ref, vc_ref, gate_ref, pat_ref, toe_ref,
                       ovt_ref, o_ref, *, tq, n_cmp):
    g = pl.program_id(1)
    qt = pl.program_id(2)
    tk = tq
    R = NSA_HPG
    par = g % 2
    lane = lax.broadcasted_iota(jnp.int32, (tq, LANES), 1)
    own = (lane // HEAD_DIM) == par
    q = q_ref[...] * HEAD_DIM ** -0.5
    parts = []
    for r in range(R):
        t = q[:, (r // 2) * LANES:(r // 2 + 1) * LANES]
        src = jnp.where(par == r % 2, t, pltpu.roll(t, HEAD_DIM, 1))
        parts.append(jnp.where(own, src, 0.0))
    q4 = jnp.concatenate(parts, axis=0).astype(BF16)
    rows = R * tq
    ri = lax.broadcasted_iota(jnp.int32, (rows, tk), 0) % tq
    ci = lax.broadcasted_iota(jnp.int32, (rows, tk), 1)
    rblk = lax.broadcasted_iota(jnp.int32, (rows, 1), 0) // tq
    c_far = jnp.zeros((rows, 1), F32)
    for r in range(R):
        c_far = jnp.where(rblk == r, c31_ref[g * R + r], c_far)

    def toe(off):
        return toe_ref[:, off].reshape(rows, tk)

    n_c = kc_ref.shape[0]
    cc = lax.broadcasted_iota(jnp.int32, (rows, n_c), 1)
    tt = qt * tq + lax.broadcasted_iota(jnp.int32, (rows, n_c), 0) % tq
    mask_c = (cc * CMP_STRIDE + CMP_LEN - 1 <= tt) & (cc < n_cmp)
    pat = pat_ref[...].reshape(rows, LANES)
    uu = lax.broadcasted_iota(jnp.int32, (LANES, n_c), 0)
    shift = jnp.where(lax.broadcasted_iota(jnp.int32, (LANES, n_c), 1) == qt * (tq // CMP_STRIDE) - CMP_NEAR_BACK + uu,
                      1.0, 0.0).astype(BF16)
    pat_hi = pat.astype(BF16)
    pat_lo = (pat - pat_hi.astype(F32)).astype(BF16)
    s_c = (_nt_dot(q4, kc_ref[...].astype(BF16)) + c_far
           + jnp.dot(pat_hi, shift, preferred_element_type=F32) + jnp.dot(pat_lo, shift, preferred_element_type=F32))
    p_c = _masked_softmax(s_c, mask_c)
    o_c = jnp.dot(p_c.astype(BF16), vc_ref[...].astype(BF16), preferred_element_type=F32)
    psum = p_c[0:tq]
    for r in range(1, R):
        psum = psum + p_c[r * tq:(r + 1) * tq]
    n_blk = ovt_ref.shape[0]
    imp = _importance_t(ovt_ref[...], psum)
    jj = lax.broadcasted_iota(jnp.int32, (n_blk, tq), 0)
    tq_pos = qt * tq + lax.broadcasted_iota(jnp.int32, (n_blk, tq), 1)
    t_blk = tq_pos // SEL_BLOCK
    forced = (jj == 0) | (jj == t_blk) | (jj == t_blk - 1)
    imp = jnp.where(jj * SEL_BLOCK <= tq_pos, imp + jnp.where(forced, SEL_FORCE, 0.0), -1.0)
    sel_t = _select_topk_t(imp, n_blk)
    sel = _pad_rows(sel_t, LANES).T
    sel4 = jnp.concatenate([sel] * R, axis=0).astype(BF16)

    def step(k_ref, v_ref, kt, carry, bias, msk):
        s = jnp.where(msk, jnp.dot(q4, _cols(k_ref, kt * tk, tk), preferred_element_type=F32) + bias, NEG)
        vt = _cols(v_ref, kt * tk, tk)
        return _online_softmax_step(carry, s, lambda p: _nt_dot(p, vt), msk)

    def sel_mask(kt):
        e = _block_expand(kt * (tk // SEL_BLOCK), LANES, tk)
        return jnp.dot(sel4, e, preferred_element_type=F32) > 0.5

    init = _softmax_init(rows, LANES)
    near0 = jnp.maximum(qt - 1, 0)

    carry = lax.fori_loop(0, near0, lambda kt, c: step(kst_ref, vst_ref, kt, c, c_far, sel_mask(kt)), init)
    carry = lax.fori_loop(near0, qt, lambda kt, c: step(kst_ref, vst_ref, kt, c, toe(1), sel_mask(kt)), carry)
    _, l, acc = step(kst_ref, vst_ref, qt, carry, toe(0), sel_mask(qt) & (ci <= ri))
    o_s = acc / l

    n_back = WINDOW // tk
    carry = lax.fori_loop(jnp.maximum(qt - n_back, 0), near0,
                          lambda kt, c: step(kwt_ref, vwt_ref, kt, c, c_far, (ci > ri) | (kt != qt - n_back)), init)
    carry = lax.fori_loop(near0, qt,
                          lambda kt, c: step(kwt_ref, vwt_ref, kt, c, toe(1), (ci > ri) | (kt != qt - n_back)), carry)
    _, l, acc = step(kwt_ref, vwt_ref, qt, carry, toe(0), ci <= ri)
    o_w = acc / l

    gs = jax.nn.sigmoid(gate_ref[...])
    n_h = NSA_KV_HEADS * R
    outs = []
    for r in range(R):
        o_r = jnp.zeros((tq, LANES), F32)
        for br, o_b in enumerate((o_c, o_s, o_w)):
            gcol = jnp.sum(jnp.where(lane == br * n_h + g * R + r, gs, 0.0), axis=-1, keepdims=True)
            o_r = o_r + gcol * o_b[r * tq:(r + 1) * tq]
        outs.append(jnp.where(par == r % 2, o_r, pltpu.roll(o_r, HEAD_DIM, 1)))
    for u in range(R // 2):
        o_ref[:, u * LANES:(u + 1) * LANES] = jnp.where(lane < HEAD_DIM, outs[2 * u], outs[2 * u + 1])


def _overlap_t(n_blk_rows, n_cmp_cols, n_cmp, n_slc):
    c0 = np.arange(n_cmp_cols)[None, :] * CMP_STRIDE
    j0 = np.arange(n_blk_rows)[:, None] * SEL_BLOCK
    ov = (c0 < j0 + SEL_BLOCK) & (c0 + CMP_LEN > j0)
    ov &= (np.arange(n_cmp_cols)[None, :] < n_cmp) & (np.arange(n_blk_rows)[:, None] < n_slc)
    return jnp.asarray(ov, BF16)


def nsa_prompt_attention(q, kvt, wint, cmp, gates, rel_bias, tq=256):
    B, T, W = q.shape
    G, R = NSA_KV_HEADS, NSA_HPG
    n_cmp = (T - CMP_LEN) // CMP_STRIDE + 1
    n_c = cmp.shape[1]
    n_slc = -(-T // SEL_BLOCK)
    n_blk = -(-n_slc // 8) * 8
    assert WINDOW % tq == 0 and n_blk <= LANES
    toe = _toeplitz_bias(rel_bias, tq, tq, 2)
    c31 = rel_bias[REL_BUCKETS - 1].astype(F32)
    n_near = CMP_NEAR_BACK + (tq - CMP_LEN) // CMP_STRIDE + 1
    assert n_near <= LANES and tq % CMP_STRIDE == 0
    dist = np.arange(tq)[:, None] - (CMP_LEN - 1) - CMP_STRIDE * (np.arange(LANES)[None, :] - CMP_NEAR_BACK)
    pat = (_bias_rows(rel_bias, dist) - c31[:, None, None]) * jnp.asarray(np.arange(LANES) < n_near, F32)
    ovt = _overlap_t(n_blk, n_c, n_cmp, n_slc)
    pair_rows = lambda base: pl.BlockSpec((None, LANES, T), lambda b, g, i: (b, base + g // 2, 0))
    pair_cols = lambda base: pl.BlockSpec((None, n_c, LANES), lambda b, g, i: (b, 0, base + g // 2))
    return pl.pallas_call(
        functools.partial(_nsa_prompt_kernel, tq=tq, n_cmp=n_cmp),
        grid=(B, G, T // tq),
        in_specs=[
            pl.BlockSpec(memory_space=pltpu.SMEM),
            pl.BlockSpec((None, tq, R * HEAD_DIM), lambda b, g, i: (b, i, g)),
            pair_rows(4), pair_rows(6), pair_rows(0), pair_rows(2),
            pair_cols(0), pair_cols(2),
            pl.BlockSpec((None, tq, LANES), lambda b, g, i: (b, i, 0)),
            pl.BlockSpec((R, tq, LANES), lambda b, g, i: (g, 0, 0)),
            pl.BlockSpec((R, 2, tq, tq), lambda b, g, i: (g, 0, 0, 0)),
            pl.BlockSpec(ovt.shape, lambda b, g, i: (0, 0)),
        ],
        out_specs=pl.BlockSpec((None, tq, R * HEAD_DIM), lambda b, g, i: (b, i, g)),
        out_shape=jax.ShapeDtypeStruct((B, T, W), F32),
        compiler_params=_cparams("parallel", "parallel", "arbitrary"),
        name="nsa_prompt",
    )(c31, q, kvt, kvt, wint, wint, cmp, cmp, gates, pat, toe, ovt)


def _nsa_decode_kernel(pt_ref, q_ref, kvn_ref, wn_ref, gate_ref, cmp_ref, win_ref, bc_ref, bs_ref, bw_ref, ovt_ref,
                       *rest, n_pages, page, n_cmp):
    pages = rest[:n_pages]
    o_ref, wout_ref, s_scr = rest[n_pages], rest[n_pages + 1], rest[n_pages + 2]
    G, R = NSA_KV_HEADS, NSA_HPG
    nq = q_ref.shape[0]
    n_h = G * R
    rows = n_h * nq
    GW = G * HEAD_DIM
    P = n_pages * page
    n_win = win_ref.shape[-1]
    lane8 = lax.broadcasted_iota(jnp.int32, (nq, LANES), 1)

    q = q_ref[...] * HEAD_DIM ** -0.5
    blocks = []
    for h in range(n_h):
        g = h // R
        t = q[:, (h // 2) * LANES:(h // 2 + 1) * LANES]
        if h % 2 != g % 2:
            t = pltpu.roll(t, HEAD_DIM, 1)
        t = jnp.where((lane8 // HEAD_DIM) == g % 2, t, 0.0)
        z = jnp.zeros((nq, LANES), F32)
        blocks.append(jnp.concatenate([t, z] if g // 2 == 0 else [z, t], axis=1))
    qg = jnp.concatenate(blocks, axis=0).astype(BF16)

    ri = lax.broadcasted_iota(jnp.int32, (rows, page), 0) % nq
    ci = lax.broadcasted_iota(jnp.int32, (rows, page), 1)
    new_mask = ci <= ri

    n_c = cmp_ref.shape[0]
    cmpv = cmp_ref[...]
    cc = lax.broadcasted_iota(jnp.int32, (rows, n_c), 1)
    s_c = _nt_dot(qg, cmpv[:, :GW].astype(BF16)) + bc_ref[...]
    p_c = _masked_softmax(s_c, cc < n_cmp)
    o_c = jnp.dot(p_c.astype(BF16), cmpv[:, GW:].astype(BF16), preferred_element_type=F32)
    ps = []
    for g in range(G):
        acc = p_c[g * R * nq:(g * R + 1) * nq]
        for r in range(1, R):
            acc = acc + p_c[(g * R + r) * nq:(g * R + r + 1) * nq]
        ps.append(acc)
    psum = _pad_rows(jnp.concatenate(ps, axis=0), LANES)
    n_blk = ovt_ref.shape[0]
    imp = _importance_t(ovt_ref[...], psum)
    jj = lax.broadcasted_iota(jnp.int32, (n_blk, LANES), 0)
    t_pos = P + lax.broadcasted_iota(jnp.int32, (n_blk, LANES), 1) % nq
    t_blk = t_pos // SEL_BLOCK
    forced = (jj == 0) | (jj == t_blk) | (jj == t_blk - 1)
    imp = jnp.where(jj * SEL_BLOCK <= t_pos, imp + jnp.where(forced, SEL_FORCE, 0.0), -1.0)
    sel_t = _select_topk_t(imp, n_blk)
    sel = _pad_rows(sel_t, LANES).T
    sel_rows = jnp.concatenate([sel[g * nq:(g + 1) * nq] for g in range(G) for _ in range(R)], axis=0).astype(BF16)

    per_tile = page // SEL_BLOCK
    m = jnp.full((rows, 1), NEG, F32)
    for p in range(n_pages):
        kt = pages[p][0].reshape(GW, page).astype(BF16)
        msk = jnp.dot(sel_rows, _block_expand(p * per_tile, LANES, page), preferred_element_type=F32) > 0.5
        s = jnp.where(msk, jnp.dot(qg, kt, preferred_element_type=F32) + bs_ref[:, p * page:(p + 1) * page], NEG)
        s_scr[:, p * page:(p + 1) * page] = s
        m = jnp.maximum(m, jnp.max(s, axis=-1, keepdims=True))
    kvn = _pad_rows(kvn_ref[...], page)
    msk = (jnp.dot(sel_rows, _block_expand(n_pages * per_tile, LANES, page), preferred_element_type=F32) > 0.5) & new_mask
    sn = jnp.where(msk, _nt_dot(qg, kvn[:, 2 * GW:3 * GW].astype(BF16)) + bs_ref[:, P:], NEG)
    m = jnp.maximum(m, jnp.max(sn, axis=-1, keepdims=True))
    pn = jnp.where(msk, jnp.exp(sn - m), 0.0)
    l = jnp.sum(pn, axis=-1, keepdims=True)
    acc = jnp.dot(pn.astype(BF16), kvn[:, 3 * GW:].astype(BF16), preferred_element_type=F32)
    for p in range(n_pages):
        pr = jnp.exp(s_scr[:, p * page:(p + 1) * page] - m)
        l = l + jnp.sum(pr, axis=-1, keepdims=True)
        acc = acc + _nt_dot(pr.astype(BF16), pages[p][1].reshape(GW, page).astype(BF16))
    o_s = acc / l

    kwt = win_ref[0].reshape(GW, n_win)
    vwt = win_ref[1].reshape(GW, n_win)
    rw = lax.broadcasted_iota(jnp.int32, (rows, n_win), 0) % nq
    cw = lax.broadcasted_iota(jnp.int32, (rows, n_win), 1)
    mask_w = (n_win + rw - cw) < WINDOW
    s_w = jnp.where(mask_w, jnp.dot(qg, kwt.astype(BF16), preferred_element_type=F32) + bw_ref[:, :n_win], NEG)
    wn = _pad_rows(wn_ref[...], page)
    s_n = jnp.where(new_mask, _nt_dot(qg, wn[:, :GW].astype(BF16)) + bw_ref[:, n_win:], NEG)
    m = jnp.maximum(jnp.max(s_w, axis=-1, keepdims=True), jnp.max(s_n, axis=-1, keepdims=True))
    p_w = jnp.where(mask_w, jnp.exp(s_w - m), 0.0)
    p_n = jnp.where(new_mask, jnp.exp(s_n - m), 0.0)
    l = jnp.sum(p_w, axis=-1, keepdims=True) + jnp.sum(p_n, axis=-1, keepdims=True)
    o_w = (_nt_dot(p_w.astype(BF16), vwt.astype(BF16))
           + jnp.dot(p_n.astype(BF16), wn[:, GW:].astype(BF16), preferred_element_type=F32)) / l

    gs = jax.nn.sigmoid(gate_ref[...])
    grep = jnp.concatenate([gs] * n_h, axis=0)
    glane = lax.broadcasted_iota(jnp.int32, (rows, LANES), 1)
    ghead = lax.broadcasted_iota(jnp.int32, (rows, LANES), 0) // nq
    o = jnp.zeros((rows, GW), F32)
    for br, o_b in enumerate((o_c, o_s, o_w)):
        o = o + jnp.sum(jnp.where(glane == br * n_h + ghead, grep, 0.0), axis=-1, keepdims=True) * o_b
    pieces = []
    for h in range(n_h):
        g = h // R
        t = o[h * nq:(h + 1) * nq, (g // 2) * LANES:(g // 2 + 1) * LANES]
        pieces.append(t if h % 2 == g % 2 else pltpu.roll(t, HEAD_DIM, 1))
    o_ref[...] = jnp.concatenate(
        [jnp.where(lane8 < HEAD_DIM, pieces[2 * u], pieces[2 * u + 1]) for u in range(n_h // 2)], axis=1)

    wt = win_ref[...].reshape(2 * GW, n_win)
    wnt = wn.T
    wout_ref[...] = jnp.concatenate([wt[:, nq:], wnt[:, :nq]], axis=1).reshape(wout_ref.shape)


def nsa_decode_attention(q, kvn, wn, gates, cmp, win_t, layer, pool_t, page_table, rel_bias):
    S, nq, W = q.shape
    G, R = NSA_KV_HEADS, NSA_HPG
    n_pages = page_table.shape[1]
    page = pool_t.shape[-1]
    P = n_pages * page
    n_win = win_t.shape[-1]
    assert nq < CMP_STRIDE and n_win == WINDOW and nq <= 8
    L = P + nq
    n_cmp = (L - CMP_LEN) // CMP_STRIDE + 1
    n_c = cmp.shape[1]
    n_slc = -(-L // SEL_BLOCK)
    n_blk = -(-n_slc // 8) * 8
    heads = range(G * R)
    qpos = P + np.arange(nq)
    bias_c = _dense_bias(rel_bias, qpos, np.arange(n_c) * CMP_STRIDE + CMP_LEN - 1, heads)
    bias_s = _dense_bias(rel_bias, qpos, np.arange(P + page), heads)
    bias_w = _dense_bias(rel_bias, qpos, P - n_win + np.arange(n_win + page), heads)
    ovt = _overlap_t(n_blk, n_c, n_cmp, n_slc)
    rows = G * R * nq
    const = lambda a: pl.BlockSpec(a.shape, lambda b, pt: (0,) * a.ndim)
    seq = lambda a: pl.BlockSpec((None,) + a.shape[1:], lambda b, pt: (b,) + (0,) * (a.ndim - 1))
    win_blk = (None, None) + win_t.shape[2:]
    grid_spec = pltpu.PrefetchScalarGridSpec(
        num_scalar_prefetch=1,
        grid=(S,),
        in_specs=[seq(q), seq(kvn), seq(wn), seq(gates), seq(cmp),
                  pl.BlockSpec(win_blk, lambda b, pt: (layer, b, 0, 0, 0, 0)),
                  const(bias_c), const(bias_s), const(bias_w), const(ovt)]
        + _page_specs(n_pages, (None, None, 2) + pool_t.shape[3:], lambda pg: (layer, pg, 1, 0, 0, 0)),
        out_specs=[seq(q), pl.BlockSpec((None,) + win_t.shape[2:], lambda b, pt: (b, 0, 0, 0, 0))],
        scratch_shapes=[pltpu.VMEM((rows, P), F32)],
    )
    return pl.pallas_call(
        functools.partial(_nsa_decode_kernel, n_pages=n_pages, page=page, n_cmp=n_cmp),
        grid_spec=grid_spec,
        out_shape=[jax.ShapeDtypeStruct(q.shape, F32), jax.ShapeDtypeStruct(win_t.shape[1:], F32)],
        compiler_params=_cparams("arbitrary"),
        name="nsa_decode",
    )(page_table, q, kvn, wn, gates, cmp, win_t, bias_c, bias_s, bias_w, ovt, *([pool_t] * n_pages))


def _w_cols(w, c0, width):
    piece = w[:, c0:c0 + width]
    if width % LANES:
        piece = jnp.pad(piece, ((0, 0), (0, LANES - width % LANES)))
    return piece.astype(BF16)


def _w_rows(w, c0, width):
    return w[:, c0:c0 + width].T.astype(BF16)


def _seq_major(a, nq, S):
    return jnp.transpose(a.reshape(nq, S, -1), (1, 0, 2))


def kernel(x_prompt, x_sample, cache_nsa_kv, state_nsa_win, cache_diff_k, cache_diff_v, cache_sb_kv, page_table, rel_bias, nsa_w_in, nsa_cmp_pos, nsa_cmp_w1, nsa_cmp_w2, nsa_w_out, diff_w_in, diff_lambda, diff_norm_g, diff_w_out, sb_w_in, sb_w_out, mlp_w_up, mlp_w_down, ln_g, ln_b):
    B, T, D = x_prompt.shape
    S, nq, _ = x_sample.shape
    depth = mlp_w_up.shape[0]
    alpha = (2 * depth) ** 0.25
    G, R, dk = NSA_KV_HEADS, NSA_HPG, HEAD_DIM
    n_pool, page = cache_nsa_kv.shape[1], cache_nsa_kv.shape[2]
    Hd, H = D // (2 * dk), D // dk
    xp = x_prompt.reshape(B * T, D)
    xs = jnp.transpose(x_sample, (1, 0, 2)).reshape(nq * S, D)
    nsa_pool_t = jnp.transpose(cache_nsa_kv, (0, 1, 3, 4, 5, 2))
    nsa_win_t = jnp.transpose(state_nsa_win, (0, 1, 3, 4, 5, 2))
    diff_k_t = jnp.transpose(cache_diff_k, (0, 1, 3, 4, 5, 2))
    sb_pool_t = jnp.transpose(cache_sb_kv, (0, 1, 3, 4, 5, 2))
    res = {k: [] for k in ("nsa_kv_p", "nsa_kv_s", "nsa_win_p", "nsa_win_s", "diff_k_p", "diff_k_s",
                           "diff_v_p", "diff_v_s", "sb_kv_p", "sb_kv_s")}
    tm_p = 512

    def to_tokens(a_t, lead):
        n, _, t = a_t.shape
        nd = len(lead)
        return jnp.transpose(a_t.reshape((n,) + lead + (t,)), (0, nd + 1) + tuple(range(1, nd + 1)))

    for i in range(depth):
        kind, j = i % N_MIXERS, i // N_MIXERS
        if kind == 0:
            w = nsa_w_in[j]
            nq_c, kv_c, win_c = G * R * dk, 4 * G * dk, 2 * G * dk
            row_p = [_w_cols(w, 0, nq_c), _w_cols(w, nq_c, 2 * G * dk), _w_cols(w, nq_c + kv_c + win_c, 3 * G * R)]
            t_ws = [_w_rows(w, nq_c, kv_c), _w_rows(w, nq_c + kv_c, win_c)]
            cmp_w = _compress_weights(nsa_cmp_pos[j], nsa_cmp_w1[j], nsa_cmp_w2[j])
            q, kc_rows, gates, kvt, wint = project(xp, row_p, t_ws, B, tm_p)
            cmp_p = compress_seq(kc_rows.reshape(B, T, -1), cmp_w)
            op = nsa_prompt_attention(q.reshape(B, T, -1), kvt, wint, cmp_p, gates.reshape(B, T, -1), rel_bias)
            row_s = [row_p[0], _w_cols(w, nq_c, kv_c), _w_cols(w, nq_c + kv_c, win_c), row_p[2]]
            qs, kvs, wns, gts, kvst = project(xs, row_s, t_ws[:1], nq, S)
            cmp_s = compress_paged(nsa_pool_t, j, page_table, cmp_w)
            os_, win_s = nsa_decode_attention(_seq_major(qs, nq, S), _seq_major(kvs, nq, S), _seq_major(wns, nq, S),
                                              _seq_major(gts, nq, S), cmp_s, nsa_win_t, j, nsa_pool_t, page_table,
                                              rel_bias)
            n_keep = min(WINDOW, T)
            res["nsa_kv_p"].append(to_tokens(kvt, (4, G, dk)))
            res["nsa_kv_s"].append(jnp.transpose(kvst.reshape(nq, 4, G, dk, S), (4, 0, 1, 2, 3)))
            res["nsa_win_p"].append(to_tokens(wint[:, :, T - n_keep:], (2, G, dk)))
            res["nsa_win_s"].append(jnp.transpose(win_s, (0, 4, 1, 2, 3)))
            w_out = nsa_w_out[j]
        elif kind == 1:
            lam_init = 0.8 - 0.6 * math.exp(-0.3 * i)
            w = diff_w_in[j]
            row_ws, t_ws = [_w_cols(w, 0, D), _w_cols(w, 2 * D, D)], [_w_rows(w, D, D)]
            q, v, kt = project(xp, row_ws, t_ws, B, tm_p)
            op = diff_prompt_attention(q.reshape(B, T, D), kt, v.reshape(B, T, D), rel_bias,
                                       diff_lambda[j], diff_norm_g[j], lam_init)
            qs, vsn, ksn, kst = project(xs, row_ws + [_w_cols(w, D, D)], t_ws, nq, S)
            vsn = _seq_major(vsn, nq, S)
            os_ = diff_decode_attention(_seq_major(qs, nq, S), _seq_major(ksn, nq, S), vsn, diff_k_t[j],
                                        cache_diff_v[j].reshape(n_pool, page * Hd, 2 * dk), page_table, rel_bias,
                                        diff_lambda[j], diff_norm_g[j], lam_init)
            res["diff_k_p"].append(to_tokens(kt, (Hd, 2, dk)))
            res["diff_k_s"].append(jnp.transpose(kst.reshape(nq, Hd, 2, dk, S), (4, 0, 1, 2, 3)))
            res["diff_v_p"].append(v.reshape(B, T, Hd, 2 * dk))
            res["diff_v_s"].append(vsn.reshape(S, nq, Hd, 2 * dk))
            w_out = diff_w_out[j]
        else:
            w = sb_w_in[j]
            row_ws, t_ws = [_w_cols(w, 0, D)], [_w_rows(w, D, 2 * D)]
            q, kvt = project(xp, row_ws, t_ws, B, tm_p)
            op = sb_prompt_attention(q.reshape(B, T, D), kvt)
            qs, kvs, kvst = project(xs, row_ws + [_w_cols(w, D, 2 * D)], t_ws, nq, S)
            os_ = sb_decode_attention(_seq_major(qs, nq, S), _seq_major(kvs, nq, S), sb_pool_t[j], page_table)
            res["sb_kv_p"].append(to_tokens(kvt, (2, H, dk)))
            res["sb_kv_s"].append(jnp.transpose(kvst.reshape(nq, 2, H, dk, S), (4, 0, 1, 2, 3)))
            w_out = sb_w_out[j]
        ln = jnp.stack([ln_g[i, 0], ln_b[i, 0], ln_g[i, 1], ln_b[i, 1]])
        w_out, w_up, w_down = w_out.astype(BF16), mlp_w_up[i].astype(BF16), mlp_w_down[i].astype(BF16)
        xp = post_mixer(op.reshape(B * T, D), w_out, xp, ln, w_up, w_down, alpha)
        os_ = jnp.transpose(os_, (1, 0, 2)).reshape(nq * S, D)
        xs = post_mixer(os_, w_out, xs, ln, w_up, w_down, alpha)
    return (xp.reshape(B, T, D), _seq_major(xs, nq, S),
            jnp.stack(res["nsa_kv_p"]), jnp.stack(res["nsa_kv_s"]), jnp.stack(res["nsa_win_p"]),
            jnp.stack(res["nsa_win_s"]), jnp.stack(res["diff_k_p"]), jnp.stack(res["diff_k_s"]),
            jnp.stack(res["diff_v_p"]), jnp.stack(res["diff_v_s"]), jnp.stack(res["sb_kv_p"]),
            jnp.stack(res["sb_kv_s"]))
```

```python
import functools
import math

import numpy as np
import jax
import jax.numpy as jnp
from jax import lax
from jax.experimental import pallas as pl
from jax.experimental.pallas import tpu as pltpu

F32 = jnp.float32
BF16 = jnp.bfloat16

HEAD_DIM = 64
NSA_KV_HEADS = 4
NSA_HPG = 4
CMP_LEN = 32
CMP_STRIDE = 16
SEL_BLOCK = 64
N_SEL = 16
SEL_FORCE = 1000.0
WINDOW = 512
REL_BUCKETS = 32
REL_MAX_DIST = 128
LN_EPS = 1e-5
NEG = -1e30
N_MIXERS = 3
SB_DEAD = 104.0

LANES = 128
VMEM_LIMIT = 48 * 1024 * 1024


def _cparams(*sem):
    return pltpu.CompilerParams(dimension_semantics=sem, vmem_limit_bytes=VMEM_LIMIT)


def _t5_bucket_np(dist):
    n = np.maximum(dist, 0)
    max_exact = REL_BUCKETS // 2
    nf = np.maximum(n, max_exact).astype(np.float32)
    large = max_exact + (np.log(nf / np.float32(max_exact)) / np.float32(math.log(REL_MAX_DIST / max_exact))
                         * np.float32(REL_BUCKETS - max_exact)).astype(np.int32)
    return np.where(n < max_exact, n, np.minimum(large, REL_BUCKETS - 1)).astype(np.int32)


FAR_DIST = int(np.min(np.nonzero(_t5_bucket_np(np.arange(4 * REL_MAX_DIST)) == REL_BUCKETS - 1)[0]))


def _bias_rows(rel_bias, dist):
    idx = jnp.asarray(_t5_bucket_np(dist))[None]
    rel = rel_bias.astype(F32)
    out = jnp.zeros((rel.shape[1],) + dist.shape, F32)
    for b in range(REL_BUCKETS):
        out = jnp.where(idx == b, rel[b].reshape((-1,) + (1,) * dist.ndim), out)
    return out


def _toeplitz_bias(rel_bias, tq, tk, n_off):
    i = np.arange(tq)[None, :, None]
    j = np.arange(tk)[None, None, :]
    off = np.arange(n_off)[:, None, None]
    return _bias_rows(rel_bias, off * tk + i - j)


def _dense_bias(rel_bias, qpos, kpos, cols):
    tab = _bias_rows(rel_bias, qpos[:, None] - kpos[None, :])[np.asarray(cols)]
    return tab.reshape(len(cols) * len(qpos), len(kpos))


def _layer_norm(z, g, b):
    mu = jnp.mean(z, axis=-1, keepdims=True)
    zc = z - mu
    var = jnp.mean(zc * zc, axis=-1, keepdims=True)
    return zc * lax.rsqrt(var + LN_EPS) * g + b


def _nt_dot(a, b):
    return lax.dot_general(a, b, (((1,), (1,)), ((), ())), preferred_element_type=F32)


def _project_kernel(*refs, n_row, n_t):
    x_ref = refs[0]
    row_w, t_w = refs[1:1 + n_row], refs[1 + n_row:1 + n_row + n_t]
    row_o, t_o = refs[1 + n_row + n_t:1 + 2 * n_row + n_t], refs[1 + 2 * n_row + n_t:]
    xb = x_ref[...].astype(BF16)
    for w_ref, o_ref in zip(row_w, row_o):
        o_ref[...] = jnp.dot(xb, w_ref[...], preferred_element_type=F32)
    for w_ref, o_ref in zip(t_w, t_o):
        o_ref[...] = _nt_dot(w_ref[...], xb)


def project(x, row_ws, t_ws, n_seq, tm):
    M, K = x.shape
    T = M // n_seq
    nt = T // tm
    n_row, n_t = len(row_ws), len(t_ws)
    const = lambda w: pl.BlockSpec(w.shape, lambda b, i: (0, 0))
    return pl.pallas_call(
        functools.partial(_project_kernel, n_row=n_row, n_t=n_t),
        grid=(n_seq, nt),
        in_specs=[pl.BlockSpec((tm, K), lambda b, i: (b * nt + i, 0))] + [const(w) for w in row_ws + t_ws],
        out_specs=[pl.BlockSpec((tm, w.shape[1]), lambda b, i: (b * nt + i, 0)) for w in row_ws]
        + [pl.BlockSpec((None, w.shape[0], tm), lambda b, i: (b, 0, i)) for w in t_ws],
        out_shape=[jax.ShapeDtypeStruct((M, w.shape[1]), F32) for w in row_ws]
        + [jax.ShapeDtypeStruct((n_seq, w.shape[0], T), F32) for w in t_ws],
        compiler_params=_cparams("parallel", "parallel"),
        name="project",
    )(x, *row_ws, *t_ws)


def _post_mixer_kernel(o_ref, wo_ref, x_ref, ln_ref, wu_ref, wd_ref, y_ref, x1_scr, xb_scr, acc_scr, *, alpha):
    f = pl.program_id(1)

    @pl.when(f == 0)
    def _():
        y = jnp.dot(o_ref[...].astype(BF16), wo_ref[...], preferred_element_type=F32)
        x1 = _layer_norm(alpha * x_ref[...] + y, ln_ref[0:1, :], ln_ref[1:2, :])
        x1_scr[...] = x1
        xb_scr[...] = x1.astype(BF16)
        acc_scr[...] = jnp.zeros_like(acc_scr)

    h = jnp.dot(xb_scr[...], wu_ref[...], preferred_element_type=F32)
    h = jnp.square(jnp.maximum(h, 0.0))
    acc_scr[...] += jnp.dot(h.astype(BF16), wd_ref[...], preferred_element_type=F32)

    @pl.when(f == pl.num_programs(1) - 1)
    def _():
        y_ref[...] = _layer_norm(alpha * x1_scr[...] + acc_scr[...], ln_ref[2:3, :], ln_ref[3:4, :])


def post_mixer(o, w_out, x, ln, w_up, w_down, alpha, tm=512, tf=1024):
    M, D = x.shape
    Fd = w_up.shape[1]
    return pl.pallas_call(
        functools.partial(_post_mixer_kernel, alpha=alpha),
        grid=(M // tm, Fd // tf),
        in_specs=[
            pl.BlockSpec((tm, D), lambda i, f: (i, 0)),
            pl.BlockSpec((D, D), lambda i, f: (0, 0)),
            pl.BlockSpec((tm, D), lambda i, f: (i, 0)),
            pl.BlockSpec((4, D), lambda i, f: (0, 0)),
            pl.BlockSpec((D, tf), lambda i, f: (0, f)),
            pl.BlockSpec((tf, D), lambda i, f: (f, 0)),
        ],
        out_specs=pl.BlockSpec((tm, D), lambda i, f: (i, 0)),
        out_shape=jax.ShapeDtypeStruct((M, D), F32),
        scratch_shapes=[pltpu.VMEM((tm, D), F32), pltpu.VMEM((tm, D), BF16), pltpu.VMEM((tm, D), F32)],
        compiler_params=_cparams("parallel", "arbitrary"),
        name="post_mixer",
    )(o, w_out, x, ln, w_up, w_down)


LOG2E = math.log2(math.e)


def _flash_step(carry, s, pv):
    m, l, acc = carry
    m_new = jnp.maximum(m, jnp.max(s, axis=-1, keepdims=True))
    alpha = jnp.exp2(m - m_new)
    p = jnp.exp2(s - m_new)
    if l is not None:
        l = alpha * l + jnp.sum(p, axis=-1, keepdims=True)
    acc = alpha * acc + pv(p.astype(BF16))
    return m_new, l, acc


def _flash_init(rows, width, with_l=True):
    return jnp.full((rows, 1), NEG, F32), jnp.zeros((rows, 1), F32) if with_l else None, jnp.zeros((rows, width), F32)


def _split_heads(q, tq):
    lane = lax.broadcasted_iota(jnp.int32, (tq, LANES), 1)
    lo = jnp.where(lane < HEAD_DIM, q, 0.0)
    hi = jnp.where(lane >= HEAD_DIM, q, 0.0)
    return jnp.concatenate([lo, hi], axis=0).astype(BF16)


def _cols(ref, start, width):
    return ref[:, pl.ds(pl.multiple_of(start, width), width)].astype(BF16)


def _rows(ref, start, height):
    return ref[pl.ds(pl.multiple_of(start, height), height), :].astype(BF16)


def _rows_blockdiag(q, n_blk, blk_w):
    rep = jnp.concatenate([q] * n_blk, axis=0)
    r = lax.broadcasted_iota(jnp.int32, rep.shape, 0) // q.shape[0]
    c = lax.broadcasted_iota(jnp.int32, rep.shape, 1) // blk_w
    return jnp.where(r == c, rep, 0.0).astype(BF16)


def _pad_rows(a, rows):
    return jnp.concatenate([a, jnp.zeros((rows - a.shape[0], a.shape[1]), a.dtype)], axis=0)


def _page_specs(n_pages, block, index):
    return [pl.BlockSpec(block, functools.partial(lambda b, pt, p: index(pt[b, p]), p=p)) for p in range(n_pages)]


def _diff_lambda(lam_ref, lam_init):
    lf = lam_ref[...]
    a = jnp.sum(lf[0:1, :] * lf[1:2, :], axis=-1, keepdims=True)
    b = jnp.sum(lf[2:3, :] * lf[3:4, :], axis=-1, keepdims=True)
    return jnp.exp(a) - jnp.exp(b) + lam_init


def _diff_finish(o1, o2, lam_full, g, lam_init):
    o = o1 - lam_full * o2
    o = o * lax.rsqrt(jnp.mean(o * o, axis=-1, keepdims=True) + LN_EPS) * g
    return o * (1.0 - lam_init)


def _causal_bias(tq, tk):
    ri = lax.broadcasted_iota(jnp.int32, (tq, tk), 0)
    ci = lax.broadcasted_iota(jnp.int32, (tq, tk), 1)
    return jnp.where(ci <= ri, 0.0, NEG)


def _add_per_head(s, b, n_blocks):
    tq, tk = b.shape
    return (s.reshape(n_blocks, tq, tk) + b[None]).reshape(n_blocks * tq, tk)


def _diff_prompt_kernel(q_ref, kt_ref, v_ref, toe_ref, lam_ref, g_ref, o_ref, *, tq, lam_init):
    qt = pl.program_id(2)
    tk = tq
    rows = 2 * tq
    q2 = _split_heads(q_ref[...] * (HEAD_DIM ** -0.5 * LOG2E), tq)

    def qk(kt):
        return jnp.dot(q2, _cols(kt_ref, kt * tk, tk), preferred_element_type=F32)

    def step(kt, c, bias):
        carry, s = c
        s_next = qk(jnp.minimum(kt + 1, qt))
        if bias is not None:
            s = s + bias
        v = _rows(v_ref, kt * tk, tk)
        return _flash_step(carry, s, lambda p: jnp.dot(p, v, preferred_element_type=F32)), s_next

    near0 = jnp.maximum(qt - 1, 0)
    c = lax.fori_loop(0, near0, lambda kt, c: step(kt, c, None), (_flash_init(rows, LANES), qk(0)))
    carry, s = lax.fori_loop(near0, qt, lambda kt, c: step(kt, c, toe_ref[1]), c)
    v = _rows(v_ref, qt * tk, tk)
    s = _add_per_head(s + toe_ref[0], _causal_bias(tq, tk), 2)
    _, l, acc = _flash_step(carry, s, lambda p: jnp.dot(p, v, preferred_element_type=F32))
    o = acc / l
    o_ref[...] = _diff_finish(o[:tq], o[tq:], _diff_lambda(lam_ref, lam_init), g_ref[...], lam_init)


def _toeplitz_rel(rel_bias, tq, tk, n_off):
    far = rel_bias[REL_BUCKETS - 1].astype(F32)[:, None, None, None]
    return (_toeplitz_bias(rel_bias, tq, tk, n_off) - far) * LOG2E


def diff_prompt_attention(q, kt, v, rel_bias, lam, norm_g, lam_init, tq=256):
    B, T, W = q.shape
    n_heads = W // LANES
    toe = _toeplitz_rel(rel_bias, tq, tq, 2)
    toe = jnp.concatenate([toe[:n_heads], toe[n_heads:]], axis=2)
    return pl.pallas_call(
        functools.partial(_diff_prompt_kernel, tq=tq, lam_init=lam_init),
        grid=(B, n_heads, T // tq),
        in_specs=[
            pl.BlockSpec((None, tq, LANES), lambda b, h, i: (b, i, h)),
            pl.BlockSpec((None, LANES, T), lambda b, h, i: (b, h, 0)),
            pl.BlockSpec((None, T, LANES), lambda b, h, i: (b, 0, h)),
            pl.BlockSpec((None, 2, 2 * tq, tq), lambda b, h, i: (h, 0, 0, 0)),
            pl.BlockSpec((4, HEAD_DIM), lambda b, h, i: (0, 0)),
            pl.BlockSpec((1, LANES), lambda b, h, i: (0, 0)),
        ],
        out_specs=pl.BlockSpec((None, tq, LANES), lambda b, h, i: (b, i, h)),
        out_shape=jax.ShapeDtypeStruct((B, T, W), F32),
        compiler_params=_cparams("parallel", "parallel", "arbitrary"),
        name="diff_prompt",
    )(q, kt, v, toe, lam, norm_g.reshape(1, LANES))


def _diff_decode_kernel(pt_ref, q_ref, kn_ref, vn_ref, bias_ref, lam_ref, g_ref, *rest, n_pages, page, lam_init):
    kp, vp = rest[:n_pages], rest[n_pages:2 * n_pages]
    o_ref, s_scr = rest[2 * n_pages], rest[2 * n_pages + 1]
    nq, W = q_ref.shape
    n_heads = W // LANES
    hr = 2 * nq
    rows = n_heads * hr
    qbd = _rows_blockdiag(q_ref[...] * HEAD_DIM ** -0.5, 2 * n_heads, HEAD_DIM)

    def pv(p, v_of_head):
        return jnp.concatenate(
            [jnp.dot(p[h * hr:(h + 1) * hr], v_of_head(h), preferred_element_type=F32) for h in range(n_heads)], axis=0)

    m = jnp.full((rows, 1), NEG, F32)
    for p in range(n_pages):
        kt = kp[p][...].reshape(W, page).astype(BF16)
        s = jnp.dot(qbd, kt, preferred_element_type=F32) + bias_ref[:, p * page:(p + 1) * page]
        s_scr[:, p * page:(p + 1) * page] = s
        m = jnp.maximum(m, jnp.max(s, axis=-1, keepdims=True))
    ri = lax.broadcasted_iota(jnp.int32, (rows, page), 0) % nq
    ci = lax.broadcasted_iota(jnp.int32, (rows, page), 1)
    mask = ci <= ri
    kn = _pad_rows(kn_ref[...], page).astype(BF16)
    vn = _pad_rows(vn_ref[...], page).astype(BF16)
    sn = jnp.where(mask, _nt_dot(qbd, kn) + bias_ref[:, n_pages * page:], NEG)
    m = jnp.maximum(m, jnp.max(sn, axis=-1, keepdims=True))
    pn = jnp.where(mask, jnp.exp(sn - m), 0.0)
    l = jnp.sum(pn, axis=-1, keepdims=True)
    acc = pv(pn.astype(BF16), lambda h: vn[:, h * LANES:(h + 1) * LANES])
    for p in range(n_pages):
        pr = jnp.exp(s_scr[:, p * page:(p + 1) * page] - m)
        l = l + jnp.sum(pr, axis=-1, keepdims=True)
        acc = acc + pv(pr.astype(BF16), lambda h: vp[p][pl.ds(h, page, stride=n_heads), :].astype(BF16))
    o = acc / l
    lam_full = _diff_lambda(lam_ref, lam_init)
    outs = [_diff_finish(o[h * hr:h * hr + nq], o[h * hr + nq:(h + 1) * hr], lam_full, g_ref[...], lam_init)
            for h in range(n_heads)]
    o_ref[...] = jnp.concatenate(outs, axis=1)


def diff_decode_attention(q, kn, vn, kt_pool, v_pool, page_table, rel_bias, lam, norm_g, lam_init):
    S, nq, W = q.shape
    n_pages = page_table.shape[1]
    page = kt_pool.shape[-1]
    n_heads = W // LANES
    P = n_pages * page
    cols = [mp * n_heads + h for h in range(n_heads) for mp in range(2)]
    bias = _dense_bias(rel_bias, P + np.arange(nq), np.arange(P + page), cols)
    rows = bias.shape[0]
    seq = pl.BlockSpec((None, nq, W), lambda b, pt: (b, 0, 0))
    grid_spec = pltpu.PrefetchScalarGridSpec(
        num_scalar_prefetch=1,
        grid=(S,),
        in_specs=[seq, seq, seq,
                  pl.BlockSpec(bias.shape, lambda b, pt: (0, 0)),
                  pl.BlockSpec((4, HEAD_DIM), lambda b, pt: (0, 0)),
                  pl.BlockSpec((1, LANES), lambda b, pt: (0, 0))]
        + _page_specs(n_pages, (None,) + kt_pool.shape[1:], lambda pg: (pg, 0, 0, 0, 0))
        + _page_specs(n_pages, (None,) + v_pool.shape[1:], lambda pg: (pg, 0, 0)),
        out_specs=seq,
        scratch_shapes=[pltpu.VMEM((rows, P), F32)],
    )
    return pl.pallas_call(
        functools.partial(_diff_decode_kernel, n_pages=n_pages, page=page, lam_init=lam_init),
        grid_spec=grid_spec,
        out_shape=jax.ShapeDtypeStruct((S, nq, W), F32),
        compiler_params=_cparams("arbitrary"),
        name="diff_decode",
    )(page_table, q, kn, vn, bias, lam, norm_g.reshape(1, LANES), *([kt_pool] * n_pages), *([v_pool] * n_pages))


def _softplus(z):
    return jnp.maximum(z, 0.0) + jnp.log(1.0 + jnp.exp(-jnp.abs(z)))


def _suffix_sum(c, u):
    hi = c.astype(BF16)
    lo = (c - hi.astype(F32)).astype(BF16)
    return jnp.dot(hi, u, preferred_element_type=F32) + jnp.dot(lo, u, preferred_element_type=F32)


def _sb_local(z, u, mask=None):
    c = _softplus(z)
    if mask is not None:
        c = jnp.where(mask, c, 0.0)
    return z - c - _suffix_sum(c, u), jnp.sum(c, axis=-1, keepdims=True)


def _sb_weights(e, r, mask=None):
    a = jnp.exp(e - r)
    if mask is not None:
        a = jnp.where(mask, a, 0.0)
    return a.astype(BF16)


def _strict_upper(n):
    return jnp.asarray(np.arange(n)[:, None] > np.arange(n)[None, :], BF16)


def _sb_prompt_kernel(q_ref, kt_ref, vt_ref, u_ref, o_ref, *, tq, tk):
    qt = pl.program_id(2)
    n_sub = tq // tk
    rows = 2 * tq
    q2 = _split_heads(q_ref[...] * HEAD_DIM ** -0.5, tq)
    u = u_ref[...]

    def block(kt, carry, diag):
        r, acc = carry
        local = []
        for sub in range(n_sub):
            mask = None
            if diag:
                ri = lax.broadcasted_iota(jnp.int32, (rows, tk), 0) % tq
                ci = lax.broadcasted_iota(jnp.int32, (rows, tk), 1) + sub * tk
                mask = ci < ri
            z = jnp.dot(q2, _cols(kt_ref, kt * tq + sub * tk, tk), preferred_element_type=F32)
            local.append(_sb_local(z, u, mask) + (mask,))
        for sub in reversed(range(n_sub)):
            e, tot, mask = local[sub]
            acc = acc + _nt_dot(_sb_weights(e, r, mask), _cols(vt_ref, kt * tq + sub * tk, tk))
            r = r + tot
        return r, acc

    r, acc = block(qt, (jnp.zeros((rows, 1), F32), jnp.zeros((rows, LANES), F32)), True)

    def cond(c):
        return jnp.logical_and(c[0] >= 0, jnp.min(c[1]) < SB_DEAD)

    def body(c):
        r, acc = block(c[0], (c[1], c[2]), False)
        return c[0] - 1, r, acc

    _, _, acc = lax.while_loop(cond, body, (qt - 1, r, acc))
    lane = lax.broadcasted_iota(jnp.int32, (tq, LANES), 1)
    o_ref[...] = jnp.where(lane < HEAD_DIM, acc[:tq], acc[tq:])


def sb_prompt_attention(q, kvt, tq=512, tk=256):
    B, T, W = q.shape
    n_pairs = W // LANES
    return pl.pallas_call(
        functools.partial(_sb_prompt_kernel, tq=tq, tk=tk),
        grid=(B, n_pairs, T // tq),
        in_specs=[
            pl.BlockSpec((None, tq, LANES), lambda b, h, i: (b, i, h)),
            pl.BlockSpec((None, LANES, T), lambda b, h, i: (b, h, 0)),
            pl.BlockSpec((None, LANES, T), lambda b, h, i: (b, n_pairs + h, 0)),
            pl.BlockSpec((tk, tk), lambda b, h, i: (0, 0)),
        ],
        out_specs=pl.BlockSpec((None, tq, LANES), lambda b, h, i: (b, i, h)),
        out_shape=jax.ShapeDtypeStruct((B, T, W), F32),
        compiler_params=_cparams("parallel", "parallel", "arbitrary"),
        name="sb_prompt",
    )(q, kvt, kvt, _strict_upper(tk))


def _sb_decode_kernel(pt_ref, q_ref, kvn_ref, u_ref, *rest, n_pages, page):
    pages, o_ref, e_scr = rest[:n_pages], rest[n_pages], rest[n_pages + 1]
    nq, W = q_ref.shape
    n_heads = W // HEAD_DIM
    rows = n_heads * nq
    u = u_ref[...]
    qbd = _rows_blockdiag(q_ref[...] * HEAD_DIM ** -0.5, n_heads, HEAD_DIM)
    kvn = _pad_rows(kvn_ref[...], page).astype(BF16)
    ri = lax.broadcasted_iota(jnp.int32, (rows, page), 0) % nq
    ci = lax.broadcasted_iota(jnp.int32, (rows, page), 1)
    new_mask = ci < ri
    e_new, r = _sb_local(_nt_dot(qbd, kvn[:, :W]), u, new_mask)
    tots = []
    for p in range(n_pages):
        kt = pages[p][0].reshape(W, page).astype(BF16)
        e, tot = _sb_local(jnp.dot(qbd, kt, preferred_element_type=F32), u)
        e_scr[:, p * page:(p + 1) * page] = e
        tots.append(tot)
    acc = jnp.dot(_sb_weights(e_new, 0.0, new_mask), kvn[:, W:], preferred_element_type=F32)
    for p in reversed(range(n_pages)):
        vt = pages[p][1].reshape(W, page).astype(BF16)
        acc = acc + _nt_dot(_sb_weights(e_scr[:, p * page:(p + 1) * page], r), vt)
        r = r + tots[p]
    col = lax.broadcasted_iota(jnp.int32, (nq, W), 1) // HEAD_DIM
    o = jnp.zeros((nq, W), F32)
    for h in range(n_heads):
        o = o + jnp.where(col == h, acc[h * nq:(h + 1) * nq, :], 0.0)
    o_ref[...] = o


def sb_decode_attention(q, kvn, kvt_pool, page_table):
    S, nq, W = q.shape
    n_pages = page_table.shape[1]
    page = kvt_pool.shape[-1]
    grid_spec = pltpu.PrefetchScalarGridSpec(
        num_scalar_prefetch=1,
        grid=(S,),
        in_specs=[pl.BlockSpec((None, nq, W), lambda b, pt: (b, 0, 0)),
                  pl.BlockSpec((None, nq, 2 * W), lambda b, pt: (b, 0, 0)),
                  pl.BlockSpec((page, page), lambda b, pt: (0, 0))]
        + _page_specs(n_pages, (None,) + kvt_pool.shape[1:], lambda pg: (pg, 0, 0, 0, 0)),
        out_specs=pl.BlockSpec((None, nq, W), lambda b, pt: (b, 0, 0)),
        scratch_shapes=[pltpu.VMEM((W // HEAD_DIM * nq, n_pages * page), F32)],
    )
    return pl.pallas_call(
        functools.partial(_sb_decode_kernel, n_pages=n_pages, page=page),
        grid_spec=grid_spec,
        out_shape=jax.ShapeDtypeStruct((S, nq, W), F32),
        compiler_params=_cparams("arbitrary"),
        name="sb_decode",
    )(page_table, q, kvn, _strict_upper(page), *([kvt_pool] * n_pages))


N_CMP_TILES = 2 * NSA_KV_HEADS * HEAD_DIM // LANES


def _gelu_tanh(x):
    return 0.5 * x * (1.0 + jnp.tanh(math.sqrt(2.0 / math.pi) * (x + 0.044715 * x * x * x)))


def _compress_body(chunk_rows, w1_ref, pos_ref, w2_ref, o_ref, n_chunks):
    n_kinds = w1_ref.shape[0]
    tiles_per_kind = NSA_KV_HEADS * HEAD_DIM // LANES
    for kind in range(n_kinds):
        xcat = jnp.concatenate(
            [jnp.concatenate([chunk_rows(l, kind * tiles_per_kind + t) for l in range(CMP_STRIDE)], axis=1)
             for t in range(tiles_per_kind)], axis=0)
        u = []
        for a in range(2):
            xa = (xcat + pos_ref[kind, a]).astype(BF16)
            u.append(jnp.dot(xa, w1_ref[kind, a], preferred_element_type=F32))
        for t in range(tiles_per_kind):
            u0 = u[0][t * n_chunks:(t + 1) * n_chunks]
            u1 = pltpu.roll(u[1][t * n_chunks:(t + 1) * n_chunks], n_chunks - 1, 0)
            hid = _gelu_tanh(u0 + u1)
            out = jnp.dot(hid.astype(BF16), w2_ref[kind], preferred_element_type=F32)
            col = (kind * tiles_per_kind + t) * LANES
            o_ref[:, col:col + LANES] = out


def _compress_seq_kernel(*refs, n_chunks):
    x_refs = refs[:N_CMP_TILES]
    w1_ref, pos_ref, w2_ref, o_ref = refs[N_CMP_TILES:]

    def chunk_rows(l, tile):
        return x_refs[tile][pl.ds(l, n_chunks, stride=CMP_STRIDE), :]
    _compress_body(chunk_rows, w1_ref, pos_ref, w2_ref, o_ref, n_chunks)


def _compress_paged_kernel(pt_ref, w1_ref, pos_ref, w2_ref, *rest, n_pages, page):
    pages, o_ref, x_scr = rest[:n_pages], rest[n_pages], rest[n_pages + 1]
    tiles_per_kind = NSA_KV_HEADS * HEAD_DIM // LANES
    for p in range(n_pages):
        for t in range(N_CMP_TILES):
            kind, pair = t // tiles_per_kind, t % tiles_per_kind
            xt = pages[p][kind, 2 * pair:2 * pair + 2].reshape(LANES, page)
            x_scr[t, p * page:(p + 1) * page, :] = xt.T
    n_chunks = n_pages * page // CMP_STRIDE

    def chunk_rows(l, tile):
        return x_scr[tile, pl.ds(l, n_chunks, stride=CMP_STRIDE), :]
    _compress_body(chunk_rows, w1_ref, pos_ref, w2_ref, o_ref, n_chunks)


def _compress_weights(cmp_pos, cmp_w1, cmp_w2):
    eye2 = jnp.eye(2, dtype=F32)
    w1r = cmp_w1.reshape(2, 2, CMP_STRIDE, HEAD_DIM, -1)
    w1e = jnp.einsum('kaldj,hg->kalhdgj', w1r, eye2)
    w1e = w1e.reshape(2, 2, CMP_STRIDE * LANES, 2 * cmp_w1.shape[-1]).astype(BF16)
    pos = jnp.broadcast_to(cmp_pos.reshape(2, 2, CMP_STRIDE, 1, HEAD_DIM), (2, 2, CMP_STRIDE, 2, HEAD_DIM))
    pos = pos.reshape(2, 2, 1, CMP_STRIDE * LANES).astype(F32)
    w2e = jnp.einsum('kjd,hg->khjgd', cmp_w2, eye2).reshape(2, 2 * cmp_w2.shape[1], LANES).astype(BF16)
    return w1e, pos, w2e


def compress_seq(kv, cmp_w):
    B, L, _ = kv.shape
    w1e, pos, w2e = cmp_w
    n_chunks = L // CMP_STRIDE
    W = N_CMP_TILES * LANES
    return pl.pallas_call(
        functools.partial(_compress_seq_kernel, n_chunks=n_chunks),
        grid=(B,),
        in_specs=[pl.BlockSpec((None, L, LANES), functools.partial(lambda b, t: (b, 0, t), t=t)) for t in range(N_CMP_TILES)]
        + [pl.BlockSpec(w1e.shape, lambda b: (0, 0, 0, 0)),
           pl.BlockSpec(pos.shape, lambda b: (0, 0, 0, 0)),
           pl.BlockSpec(w2e.shape, lambda b: (0, 0, 0))],
        out_specs=pl.BlockSpec((None, n_chunks, W), lambda b: (b, 0, 0)),
        out_shape=jax.ShapeDtypeStruct((B, n_chunks, W), F32),
        compiler_params=_cparams("parallel"),
        name="nsa_compress_seq",
    )(*([kv] * N_CMP_TILES), w1e, pos, w2e)


def compress_paged(pool_t, layer, page_table, cmp_w):
    S, n_pages = page_table.shape
    page = pool_t.shape[-1]
    w1e, pos, w2e = cmp_w
    n_chunks = n_pages * page // CMP_STRIDE
    W = N_CMP_TILES * LANES
    blk = (None, None, 2) + pool_t.shape[3:]
    grid_spec = pltpu.PrefetchScalarGridSpec(
        num_scalar_prefetch=1,
        grid=(S,),
        in_specs=[pl.BlockSpec(w1e.shape, lambda b, pt: (0, 0, 0, 0)),
                  pl.BlockSpec(pos.shape, lambda b, pt: (0, 0, 0, 0)),
                  pl.BlockSpec(w2e.shape, lambda b, pt: (0, 0, 0))]
        + _page_specs(n_pages, blk, lambda pg: (layer, pg, 0, 0, 0, 0)),
        out_specs=pl.BlockSpec((None, n_chunks, W), lambda b, pt: (b, 0, 0)),
        scratch_shapes=[pltpu.VMEM((N_CMP_TILES, n_pages * page, LANES), F32)],
    )
    return pl.pallas_call(
        functools.partial(_compress_paged_kernel, n_pages=n_pages, page=page),
        grid_spec=grid_spec,
        out_shape=jax.ShapeDtypeStruct((S, n_chunks, W), F32),
        compiler_params=_cparams("arbitrary"),
        name="nsa_compress_paged",
    )(page_table, w1e, pos, w2e, *([pool_t] * n_pages))


def _masked_softmax(s, mask, exp=jnp.exp):
    s = jnp.where(mask, s, NEG)
    m = jnp.max(s, axis=-1, keepdims=True)
    e = jnp.where(mask, exp(s - m), 0.0)
    return e / jnp.maximum(jnp.sum(e, axis=-1, keepdims=True), 1e-30)


def _importance_t(ov_t, psum):
    hi = psum.astype(BF16)
    lo = (psum - hi.astype(F32)).astype(BF16)
    return _nt_dot(ov_t, hi) + _nt_dot(ov_t, lo)


def _select_topk_t(imp, n_rows):
    jj = lax.broadcasted_iota(jnp.int32, imp.shape, 0)
    cnt = jnp.zeros(imp.shape, F32)
    for i in range(n_rows):
        row = imp[i:i + 1, :]
        ahead = (row > imp) | ((row == imp) & (jj > i))
        cnt = cnt + jnp.where(ahead, 1.0, 0.0)
    return jnp.where(cnt < N_SEL, 1.0, 0.0)


def _select_topk_ref(imp_ref, n_active):
    imp = imp_ref[...]
    jj = lax.broadcasted_iota(jnp.int32, imp.shape, 0)

    def body(i, cnt):
        row = imp_ref[pl.ds(i, 1), :]
        ahead = (row > imp) | ((row == imp) & (jj > i))
        return cnt + jnp.where(ahead, 1.0, 0.0)

    cnt = lax.fori_loop(0, n_active, body, jnp.zeros(imp.shape, F32))
    return jnp.where(cnt < N_SEL, 1.0, 0.0)


def _block_expand(first_block, n_blocks, tk):
    j = lax.broadcasted_iota(jnp.int32, (n_blocks, tk), 0)
    col = lax.broadcasted_iota(jnp.int32, (n_blocks, tk), 1)
    return jnp.where(j == first_block + col // SEL_BLOCK, 1.0, 0.0).astype(BF16)


CMP_NEAR_BACK = -(-(CMP_LEN - 1 + FAR_DIST) // CMP_STRIDE)


def _nsa_prompt_kernel(q_ref, kst_ref, vst_ref, kwt_ref, vwt_ref, kc_ref, vc_ref, gate_ref, pat_ref, toe_ref,
                       ovt_ref, o_ref, imp_scr, *, tq, n_cmp):
    g = pl.program_id(1)
    qt = pl.program_id(2)
    tk = tq
    R = NSA_HPG
    par = g % 2
    lane = lax.broadcasted_iota(jnp.int32, (tq, LANES), 1)
    own = (lane // HEAD_DIM) == par
    q = q_ref[...] * (HEAD_DIM ** -0.5 * LOG2E)
    parts = []
    for r in range(R):
        t = q[:, (r // 2) * LANES:(r // 2 + 1) * LANES]
        src = jnp.where(par == r % 2, t, pltpu.roll(t, HEAD_DIM, 1))
        parts.append(jnp.where(own, src, 0.0))
    q4 = jnp.concatenate(parts, axis=0).astype(BF16)
    rows = R * tq

    def toe(off):
        return toe_ref[:, off].reshape(rows, tk)

    n_c = kc_ref.shape[0]
    cc = lax.broadcasted_iota(jnp.int32, (rows, n_c), 1)
    tt = qt * tq + lax.broadcasted_iota(jnp.int32, (rows, n_c), 0) % tq
    mask_c = (cc * CMP_STRIDE + CMP_LEN - 1 <= tt) & (cc < n_cmp)
    pat = pat_ref[...].reshape(rows, LANES)
    uu = lax.broadcasted_iota(jnp.int32, (LANES, n_c), 0)
    shift = jnp.where(lax.broadcasted_iota(jnp.int32, (LANES, n_c), 1) == qt * (tq // CMP_STRIDE) - CMP_NEAR_BACK + uu,
                      1.0, 0.0).astype(BF16)
    s_c = _nt_dot(q4, kc_ref[...].astype(BF16)) + jnp.dot(pat.astype(BF16), shift, preferred_element_type=F32)
    p_c = _masked_softmax(s_c, mask_c, jnp.exp2)
    o_c = jnp.dot(p_c.astype(BF16), vc_ref[...].astype(BF16), preferred_element_type=F32)
    psum = p_c[0:tq]
    for r in range(1, R):
        psum = psum + p_c[r * tq:(r + 1) * tq]
    n_blk = ovt_ref.shape[0]
    imp = _importance_t(ovt_ref[...], psum)
    jj = lax.broadcasted_iota(jnp.int32, (n_blk, tq), 0)
    tq_pos = qt * tq + lax.broadcasted_iota(jnp.int32, (n_blk, tq), 1)
    t_blk = tq_pos // SEL_BLOCK
    forced = (jj == 0) | (jj == t_blk) | (jj == t_blk - 1)
    imp_scr[...] = jnp.where(jj * SEL_BLOCK <= tq_pos, imp + jnp.where(forced, SEL_FORCE, 0.0), -1.0)
    sel_t = _select_topk_ref(imp_scr, jnp.minimum((qt + 1) * (tq // SEL_BLOCK), n_blk))
    sel = _pad_rows(sel_t, LANES).T
    sel_neg = ((sel - 1.0) * -NEG).astype(BF16)

    own_rows = (lax.broadcasted_iota(jnp.int32, (LANES, 1), 0) // HEAD_DIM) == par

    def branch(k_ref, v_ref, first, tile_bias):
        def qk(kt):
            return jnp.dot(q4, _cols(k_ref, kt * tk, tk), preferred_element_type=F32)

        def finish_tile(kt, carry, s, toe_off, extra):
            b = tile_bias(kt)
            if extra is not None:
                b = extra if b is None else b + extra
            if toe_off is not None:
                s = s + toe(toe_off)
            if b is not None:
                s = _add_per_head(s, b, R)
            vt = jnp.where(own_rows, _cols(v_ref, kt * tk, tk), 1.0)
            return _flash_step(carry, s, lambda p: _nt_dot(p, vt))

        def step(kt, c, toe_off):
            carry, s = c
            s_next = qk(jnp.minimum(kt + 1, qt))
            return finish_tile(kt, carry, s, toe_off, None), s_next

        c = (_flash_init(rows, LANES, with_l=False), qk(first))
        c = lax.fori_loop(first, near0, lambda kt, c: step(kt, c, None), c)
        carry, s = lax.fori_loop(jnp.maximum(near0, first), qt, lambda kt, c: step(kt, c, 1), c)
        _, _, acc = finish_tile(qt, carry, s, 0, _causal_bias(tq, tk))
        return acc / pltpu.roll(acc, HEAD_DIM, 1)

    near0 = jnp.maximum(qt - 1, 0)

    def sel_bias(kt):
        return jnp.dot(sel_neg, _block_expand(kt * (tk // SEL_BLOCK), LANES, tk), preferred_element_type=F32)

    o_s = branch(kst_ref, vst_ref, 0, sel_bias)

    n_back = WINDOW // tk
    ri = lax.broadcasted_iota(jnp.int32, (tq, tk), 0)
    ci = lax.broadcasted_iota(jnp.int32, (tq, tk), 1)

    def win_bias(kt):
        return jnp.where((ci > ri) | (kt != qt - n_back), 0.0, NEG)

    o_w = branch(kwt_ref, vwt_ref, jnp.maximum(qt - n_back, 0), win_bias)

    gs = jax.nn.sigmoid(gate_ref[...])
    n_h = NSA_KV_HEADS * R
    outs = []
    for r in range(R):
        o_r = jnp.zeros((tq, LANES), F32)
        for br, o_b in enumerate((o_c, o_s, o_w)):
            gcol = jnp.sum(jnp.where(lane == br * n_h + g * R + r, gs, 0.0), axis=-1, keepdims=True)
            o_r = o_r + gcol * o_b[r * tq:(r + 1) * tq]
        outs.append(jnp.where(par == r % 2, o_r, pltpu.roll(o_r, HEAD_DIM, 1)))
    for u in range(R // 2):
        o_ref[:, u * LANES:(u + 1) * LANES] = jnp.where(lane < HEAD_DIM, outs[2 * u], outs[2 * u + 1])


def _overlap_t(n_blk_rows, n_cmp_cols, n_cmp, n_slc):
    c0 = np.arange(n_cmp_cols)[None, :] * CMP_STRIDE
    j0 = np.arange(n_blk_rows)[:, None] * SEL_BLOCK
    ov = (c0 < j0 + SEL_BLOCK) & (c0 + CMP_LEN > j0)
    ov &= (np.arange(n_cmp_cols)[None, :] < n_cmp) & (np.arange(n_blk_rows)[:, None] < n_slc)
    return jnp.asarray(ov, BF16)


def nsa_prompt_attention(q, kvt, wint, cmp, gates, rel_bias, tq=256):
    B, T, W = q.shape
    G, R = NSA_KV_HEADS, NSA_HPG
    n_cmp = (T - CMP_LEN) // CMP_STRIDE + 1
    n_c = cmp.shape[1]
    n_slc = -(-T // SEL_BLOCK)
    n_blk = -(-n_slc // 8) * 8
    assert WINDOW % tq == 0 and n_blk <= LANES
    toe = _toeplitz_rel(rel_bias, tq, tq, 2)
    c31 = rel_bias[REL_BUCKETS - 1].astype(F32)
    n_near = CMP_NEAR_BACK + (tq - CMP_LEN) // CMP_STRIDE + 1
    assert n_near <= LANES and tq % CMP_STRIDE == 0
    dist = np.arange(tq)[:, None] - (CMP_LEN - 1) - CMP_STRIDE * (np.arange(LANES)[None, :] - CMP_NEAR_BACK)
    pat = (_bias_rows(rel_bias, dist) - c31[:, None, None]) * jnp.asarray((np.arange(LANES) < n_near) * LOG2E, F32)
    ovt = _overlap_t(n_blk, n_c, n_cmp, n_slc)
    pair_rows = lambda base: pl.BlockSpec((None, LANES, T), lambda b, g, i: (b, base + g // 2, 0))
    pair_cols = lambda base: pl.BlockSpec((None, n_c, LANES), lambda b, g, i: (b, 0, base + g // 2))
    return pl.pallas_call(
        functools.partial(_nsa_prompt_kernel, tq=tq, n_cmp=n_cmp),
        grid=(B, G, T // tq),
        in_specs=[
            pl.BlockSpec((None, tq, R * HEAD_DIM), lambda b, g, i: (b, i, g)),
            pair_rows(4), pair_rows(6), pair_rows(0), pair_rows(2),
            pair_cols(0), pair_cols(2),
            pl.BlockSpec((None, tq, LANES), lambda b, g, i: (b, i, 0)),
            pl.BlockSpec((R, tq, LANES), lambda b, g, i: (g, 0, 0)),
            pl.BlockSpec((R, 2, tq, tq), lambda b, g, i: (g, 0, 0, 0)),
            pl.BlockSpec(ovt.shape, lambda b, g, i: (0, 0)),
        ],
        out_specs=pl.BlockSpec((None, tq, R * HEAD_DIM), lambda b, g, i: (b, i, g)),
        out_shape=jax.ShapeDtypeStruct((B, T, W), F32),
        scratch_shapes=[pltpu.VMEM((n_blk, tq), F32)],
        compiler_params=_cparams("parallel", "parallel", "arbitrary"),
        name="nsa_prompt",
    )(q, kvt, kvt, wint, wint, cmp, cmp, gates, pat, toe, ovt)


def _nsa_decode_kernel(pt_ref, q_ref, kvn_ref, wn_ref, gate_ref, cmp_ref, win_ref, bc_ref, bs_ref, bw_ref, ovt_ref,
                       *rest, n_pages, page, n_cmp):
    pages = rest[:n_pages]
    o_ref, wout_ref, s_scr = rest[n_pages], rest[n_pages + 1], rest[n_pages + 2]
    G, R = NSA_KV_HEADS, NSA_HPG
    nq = q_ref.shape[0]
    n_h = G * R
    rows = n_h * nq
    GW = G * HEAD_DIM
    P = n_pages * page
    n_win = win_ref.shape[-1]
    lane8 = lax.broadcasted_iota(jnp.int32, (nq, LANES), 1)

    q = q_ref[...] * HEAD_DIM ** -0.5
    blocks = []
    for h in range(n_h):
        g = h // R
        t = q[:, (h // 2) * LANES:(h // 2 + 1) * LANES]
        if h % 2 != g % 2:
            t = pltpu.roll(t, HEAD_DIM, 1)
        t = jnp.where((lane8 // HEAD_DIM) == g % 2, t, 0.0)
        z = jnp.zeros((nq, LANES), F32)
        blocks.append(jnp.concatenate([t, z] if g // 2 == 0 else [z, t], axis=1))
    qg = jnp.concatenate(blocks, axis=0).astype(BF16)

    ri = lax.broadcasted_iota(jnp.int32, (rows, page), 0) % nq
    ci = lax.broadcasted_iota(jnp.int32, (rows, page), 1)
    new_mask = ci <= ri

    n_c = cmp_ref.shape[0]
    cmpv = cmp_ref[...]
    cc = lax.broadcasted_iota(jnp.int32, (rows, n_c), 1)
    s_c = _nt_dot(qg, cmpv[:, :GW].astype(BF16)) + bc_ref[...]
    p_c = _masked_softmax(s_c, cc < n_cmp)
    o_c = jnp.dot(p_c.astype(BF16), cmpv[:, GW:].astype(BF16), preferred_element_type=F32)
    ps = []
    for g in range(G):
        acc = p_c[g * R * nq:(g * R + 1) * nq]
        for r in range(1, R):
            acc = acc + p_c[(g * R + r) * nq:(g * R + r + 1) * nq]
        ps.append(acc)
    psum = _pad_rows(jnp.concatenate(ps, axis=0), LANES)
    n_blk = ovt_ref.shape[0]
    imp = _importance_t(ovt_ref[...], psum)
    jj = lax.broadcasted_iota(jnp.int32, (n_blk, LANES), 0)
    t_pos = P + lax.broadcasted_iota(jnp.int32, (n_blk, LANES), 1) % nq
    t_blk = t_pos // SEL_BLOCK
    forced = (jj == 0) | (jj == t_blk) | (jj == t_blk - 1)
    imp = jnp.where(jj * SEL_BLOCK <= t_pos, imp + jnp.where(forced, SEL_FORCE, 0.0), -1.0)
    sel_t = _select_topk_t(imp, n_blk)
    sel = _pad_rows(sel_t, LANES).T
    sel_rows = jnp.concatenate([sel[g * nq:(g + 1) * nq] for g in range(G) for _ in range(R)], axis=0).astype(BF16)

    per_tile = page // SEL_BLOCK
    m = jnp.full((rows, 1), NEG, F32)
    for p in range(n_pages):
        kt = pages[p][0].reshape(GW, page).astype(BF16)
        msk = jnp.dot(sel_rows, _block_expand(p * per_tile, LANES, page), preferred_element_type=F32) > 0.5
        s = jnp.where(msk, jnp.dot(qg, kt, preferred_element_type=F32) + bs_ref[:, p * page:(p + 1) * page], NEG)
        s_scr[:, p * page:(p + 1) * page] = s
        m = jnp.maximum(m, jnp.max(s, axis=-1, keepdims=True))
    kvn = _pad_rows(kvn_ref[...], page)
    msk = (jnp.dot(sel_rows, _block_expand(n_pages * per_tile, LANES, page), preferred_element_type=F32) > 0.5) & new_mask
    sn = jnp.where(msk, _nt_dot(qg, kvn[:, 2 * GW:3 * GW].astype(BF16)) + bs_ref[:, P:], NEG)
    m = jnp.maximum(m, jnp.max(sn, axis=-1, keepdims=True))
    pn = jnp.where(msk, jnp.exp(sn - m), 0.0)
    l = jnp.sum(pn, axis=-1, keepdims=True)
    acc = jnp.dot(pn.astype(BF16), kvn[:, 3 * GW:].astype(BF16), preferred_element_type=F32)
    for p in range(n_pages):
        pr = jnp.exp(s_scr[:, p * page:(p + 1) * page] - m)
        l = l + jnp.sum(pr, axis=-1, keepdims=True)
        acc = acc + _nt_dot(pr.astype(BF16), pages[p][1].reshape(GW, page).astype(BF16))
    o_s = acc / l

    kwt = win_ref[0].reshape(GW, n_win)
    vwt = win_ref[1].reshape(GW, n_win)
    rw = lax.broadcasted_iota(jnp.int32, (rows, n_win), 0) % nq
    cw = lax.broadcasted_iota(jnp.int32, (rows, n_win), 1)
    mask_w = (n_win + rw - cw) < WINDOW
    s_w = jnp.where(mask_w, jnp.dot(qg, kwt.astype(BF16), preferred_element_type=F32) + bw_ref[:, :n_win], NEG)
    wn = _pad_rows(wn_ref[...], page)
    s_n = jnp.where(new_mask, _nt_dot(qg, wn[:, :GW].astype(BF16)) + bw_ref[:, n_win:], NEG)
    m = jnp.maximum(jnp.max(s_w, axis=-1, keepdims=True), jnp.max(s_n, axis=-1, keepdims=True))
    p_w = jnp.where(mask_w, jnp.exp(s_w - m), 0.0)
    p_n = jnp.where(new_mask, jnp.exp(s_n - m), 0.0)
    l = jnp.sum(p_w, axis=-1, keepdims=True) + jnp.sum(p_n, axis=-1, keepdims=True)
    o_w = (_nt_dot(p_w.astype(BF16), vwt.astype(BF16))
           + jnp.dot(p_n.astype(BF16), wn[:, GW:].astype(BF16), preferred_element_type=F32)) / l

    gs = jax.nn.sigmoid(gate_ref[...])
    grep = jnp.concatenate([gs] * n_h, axis=0)
    glane = lax.broadcasted_iota(jnp.int32, (rows, LANES), 1)
    ghead = lax.broadcasted_iota(jnp.int32, (rows, LANES), 0) // nq
    o = jnp.zeros((rows, GW), F32)
    for br, o_b in enumerate((o_c, o_s, o_w)):
        o = o + jnp.sum(jnp.where(glane == br * n_h + ghead, grep, 0.0), axis=-1, keepdims=True) * o_b
    pieces = []
    for h in range(n_h):
        g = h // R
        t = o[h * nq:(h + 1) * nq, (g // 2) * LANES:(g // 2 + 1) * LANES]
        pieces.append(t if h % 2 == g % 2 else pltpu.roll(t, HEAD_DIM, 1))
    o_ref[...] = jnp.concatenate(
        [jnp.where(lane8 < HEAD_DIM, pieces[2 * u], pieces[2 * u + 1]) for u in range(n_h // 2)], axis=1)

    wt = win_ref[...].reshape(2 * GW, n_win)
    wnt = wn.T
    wout_ref[...] = jnp.concatenate([wt[:, nq:], wnt[:, :nq]], axis=1).reshape(wout_ref.shape)


def nsa_decode_attention(q, kvn, wn, gates, cmp, win_t, layer, pool_t, page_table, rel_bias):
    S, nq, W = q.shape
    G, R = NSA_KV_HEADS, NSA_HPG
    n_pages = page_table.shape[1]
    page = pool_t.shape[-1]
    P = n_pages * page
    n_win = win_t.shape[-1]
    assert nq < CMP_STRIDE and n_win == WINDOW and nq <= 8
    L = P + nq
    n_cmp = (L - CMP_LEN) // CMP_STRIDE + 1
    n_c = cmp.shape[1]
    n_slc = -(-L // SEL_BLOCK)
    n_blk = -(-n_slc // 8) * 8
    heads = range(G * R)
    qpos = P + np.arange(nq)
    bias_c = _dense_bias(rel_bias, qpos, np.arange(n_c) * CMP_STRIDE + CMP_LEN - 1, heads)
    bias_s = _dense_bias(rel_bias, qpos, np.arange(P + page), heads)
    bias_w = _dense_bias(rel_bias, qpos, P - n_win + np.arange(n_win + page), heads)
    ovt = _overlap_t(n_blk, n_c, n_cmp, n_slc)
    rows = G * R * nq
    const = lambda a: pl.BlockSpec(a.shape, lambda b, pt: (0,) * a.ndim)
    seq = lambda a: pl.BlockSpec((None,) + a.shape[1:], lambda b, pt: (b,) + (0,) * (a.ndim - 1))
    win_blk = (None, None) + win_t.shape[2:]
    grid_spec = pltpu.PrefetchScalarGridSpec(
        num_scalar_prefetch=1,
        grid=(S,),
        in_specs=[seq(q), seq(kvn), seq(wn), seq(gates), seq(cmp),
                  pl.BlockSpec(win_blk, lambda b, pt: (layer, b, 0, 0, 0, 0)),
                  const(bias_c), const(bias_s), const(bias_w), const(ovt)]
        + _page_specs(n_pages, (None, None, 2) + pool_t.shape[3:], lambda pg: (layer, pg, 1, 0, 0, 0)),
        out_specs=[seq(q), pl.BlockSpec((None,) + win_t.shape[2:], lambda b, pt: (b, 0, 0, 0, 0))],
        scratch_shapes=[pltpu.VMEM((rows, P), F32)],
    )
    return pl.pallas_call(
        functools.partial(_nsa_decode_kernel, n_pages=n_pages, page=page, n_cmp=n_cmp),
        grid_spec=grid_spec,
        out_shape=[jax.ShapeDtypeStruct(q.shape, F32), jax.ShapeDtypeStruct(win_t.shape[1:], F32)],
        compiler_params=_cparams("arbitrary"),
        name="nsa_decode",
    )(page_table, q, kvn, wn, gates, cmp, win_t, bias_c, bias_s, bias_w, ovt, *([pool_t] * n_pages))


def _w_cols(w, c0, width):
    piece = w[:, c0:c0 + width]
    if width % LANES:
        piece = jnp.pad(piece, ((0, 0), (0, LANES - width % LANES)))
    return piece.astype(BF16)


def _w_rows(w, c0, width):
    return w[:, c0:c0 + width].T.astype(BF16)


def _seq_major(a, nq, S):
    return jnp.transpose(a.reshape(nq, S, -1), (1, 0, 2))


def kernel(x_prompt, x_sample, cache_nsa_kv, state_nsa_win, cache_diff_k, cache_diff_v, cache_sb_kv, page_table, rel_bias, nsa_w_in, nsa_cmp_pos, nsa_cmp_w1, nsa_cmp_w2, nsa_w_out, diff_w_in, diff_lambda, diff_norm_g, diff_w_out, sb_w_in, sb_w_out, mlp_w_up, mlp_w_down, ln_g, ln_b):
    B, T, D = x_prompt.shape
    S, nq, _ = x_sample.shape
    depth = mlp_w_up.shape[0]
    alpha = (2 * depth) ** 0.25
    G, R, dk = NSA_KV_HEADS, NSA_HPG, HEAD_DIM
    n_pool, page = cache_nsa_kv.shape[1], cache_nsa_kv.shape[2]
    Hd, H = D // (2 * dk), D // dk
    xp = x_prompt.reshape(B * T, D)
    xs = jnp.transpose(x_sample, (1, 0, 2)).reshape(nq * S, D)
    nsa_pool_t = jnp.transpose(cache_nsa_kv, (0, 1, 3, 4, 5, 2))
    nsa_win_t = jnp.transpose(state_nsa_win, (0, 1, 3, 4, 5, 2))
    diff_k_t = jnp.transpose(cache_diff_k, (0, 1, 3, 4, 5, 2))
    sb_pool_t = jnp.transpose(cache_sb_kv, (0, 1, 3, 4, 5, 2))
    res = {k: [] for k in ("nsa_kv_p", "nsa_kv_s", "nsa_win_p", "nsa_win_s", "diff_k_p", "diff_k_s",
                           "diff_v_p", "diff_v_s", "sb_kv_p", "sb_kv_s")}
    tm_p = 512

    def to_tokens(a_t, lead):
        n, _, t = a_t.shape
        nd = len(lead)
        return jnp.transpose(a_t.reshape((n,) + lead + (t,)), (0, nd + 1) + tuple(range(1, nd + 1)))

    for i in range(depth):
        kind, j = i % N_MIXERS, i // N_MIXERS
        if kind == 0:
            w = nsa_w_in[j]
            nq_c, kv_c, win_c = G * R * dk, 4 * G * dk, 2 * G * dk
            row_p = [_w_cols(w, 0, nq_c), _w_cols(w, nq_c, 2 * G * dk), _w_cols(w, nq_c + kv_c + win_c, 3 * G * R)]
            t_ws = [_w_rows(w, nq_c, kv_c), _w_rows(w, nq_c + kv_c, win_c)]
            cmp_w = _compress_weights(nsa_cmp_pos[j], nsa_cmp_w1[j], nsa_cmp_w2[j])
            q, kc_rows, gates, kvt, wint = project(xp, row_p, t_ws, B, tm_p)
            cmp_p = compress_seq(kc_rows.reshape(B, T, -1), cmp_w)
            op = nsa_prompt_attention(q.reshape(B, T, -1), kvt, wint, cmp_p, gates.reshape(B, T, -1), rel_bias)
            row_s = [row_p[0], _w_cols(w, nq_c, kv_c), _w_cols(w, nq_c + kv_c, win_c), row_p[2]]
            qs, kvs, wns, gts, kvst = project(xs, row_s, t_ws[:1], nq, S)
            cmp_s = compress_paged(nsa_pool_t, j, page_table, cmp_w)
            os_, win_s = nsa_decode_attention(_seq_major(qs, nq, S), _seq_major(kvs, nq, S), _seq_major(wns, nq, S),
                                              _seq_major(gts, nq, S), cmp_s, nsa_win_t, j, nsa_pool_t, page_table,
                                              rel_bias)
            n_keep = min(WINDOW, T)
            res["nsa_kv_p"].append(to_tokens(kvt, (4, G, dk)))
            res["nsa_kv_s"].append(jnp.transpose(kvst.reshape(nq, 4, G, dk, S), (4, 0, 1, 2, 3)))
            res["nsa_win_p"].append(to_tokens(wint[:, :, T - n_keep:], (2, G, dk)))
            res["nsa_win_s"].append(jnp.transpose(win_s, (0, 4, 1, 2, 3)))
            w_out = nsa_w_out[j]
        elif kind == 1:
            lam_init = 0.8 - 0.6 * math.exp(-0.3 * i)
            w = diff_w_in[j]
            row_ws, t_ws = [_w_cols(w, 0, D), _w_cols(w, 2 * D, D)], [_w_rows(w, D, D)]
            q, v, kt = project(xp, row_ws, t_ws, B, tm_p)
            op = diff_prompt_attention(q.reshape(B, T, D), kt, v.reshape(B, T, D), rel_bias,
                                       diff_lambda[j], diff_norm_g[j], lam_init)
            qs, vsn, ksn, kst = project(xs, row_ws + [_w_cols(w, D, D)], t_ws, nq, S)
            vsn = _seq_major(vsn, nq, S)
            os_ = diff_decode_attention(_seq_major(qs, nq, S), _seq_major(ksn, nq, S), vsn, diff_k_t[j],
                                        cache_diff_v[j].reshape(n_pool, page * Hd, 2 * dk), page_table, rel_bias,
                                        diff_lambda[j], diff_norm_g[j], lam_init)
            res["diff_k_p"].append(to_tokens(kt, (Hd, 2, dk)))
            res["diff_k_s"].append(jnp.transpose(kst.reshape(nq, Hd, 2, dk, S), (4, 0, 1, 2, 3)))
            res["diff_v_p"].append(v.reshape(B, T, Hd, 2 * dk))
            res["diff_v_s"].append(vsn.reshape(S, nq, Hd, 2 * dk))
            w_out = diff_w_out[j]
        else:
            w = sb_w_in[j]
            row_ws, t_ws = [_w_cols(w, 0, D)], [_w_rows(w, D, 2 * D)]
            q, kvt = project(xp, row_ws, t_ws, B, tm_p)
            op = sb_prompt_attention(q.reshape(B, T, D), kvt)
            qs, kvs, kvst = project(xs, row_ws + [_w_cols(w, D, 2 * D)], t_ws, nq, S)
            os_ = sb_decode_attention(_seq_major(qs, nq, S), _seq_major(kvs, nq, S), sb_pool_t[j], page_table)
            res["sb_kv_p"].append(to_tokens(kvt, (2, H, dk)))
            res["sb_kv_s"].append(jnp.transpose(kvst.reshape(nq, 2, H, dk, S), (4, 0, 1, 2, 3)))
            w_out = sb_w_out[j]
        ln = jnp.stack([ln_g[i, 0], ln_b[i, 0], ln_g[i, 1], ln_b[i, 1]])
        w_out, w_up, w_down = w_out.astype(BF16), mlp_w_up[i].astype(BF16), mlp_w_down[i].astype(BF16)
        xp = post_mixer(op.reshape(B * T, D), w_out, xp, ln, w_up, w_down, alpha)
        os_ = jnp.transpose(os_, (1, 0, 2)).reshape(nq * S, D)
        xs = post_mixer(os_, w_out, xs, ln, w_up, w_down, alpha)
    return (xp.reshape(B, T, D), _seq_major(xs, nq, S),
            jnp.stack(res["nsa_kv_p"]), jnp.stack(res["nsa_kv_s"]), jnp.stack(res["nsa_win_p"]),
            jnp.stack(res["nsa_win_s"]), jnp.stack(res["diff_k_p"]), jnp.stack(res["diff_k_s"]),
            jnp.stack(res["diff_v_p"]), jnp.stack(res["diff_v_s"]), jnp.stack(res["sb_kv_p"]),
            jnp.stack(res["sb_kv_s"]))
```

```python
import functools
import math

import numpy as np
import jax
import jax.numpy as jnp
from jax import lax
from jax.experimental import pallas as pl
from jax.experimental.pallas import tpu as pltpu

F32 = jnp.float32
BF16 = jnp.bfloat16

HEAD_DIM = 64
NSA_KV_HEADS = 4
NSA_HPG = 4
CMP_LEN = 32
CMP_STRIDE = 16
SEL_BLOCK = 64
N_SEL = 16
SEL_FORCE = 1000.0
WINDOW = 512
REL_BUCKETS = 32
REL_MAX_DIST = 128
LN_EPS = 1e-5
NEG = -1e30
N_MIXERS = 3
SB_DEAD = 104.0

LANES = 128
VMEM_LIMIT = 48 * 1024 * 1024


def _cparams(*sem):
    return pltpu.CompilerParams(dimension_semantics=sem, vmem_limit_bytes=VMEM_LIMIT)


def _t5_bucket_np(dist):
    n = np.maximum(dist, 0)
    max_exact = REL_BUCKETS // 2
    nf = np.maximum(n, max_exact).astype(np.float32)
    large = max_exact + (np.log(nf / np.float32(max_exact)) / np.float32(math.log(REL_MAX_DIST / max_exact))
                         * np.float32(REL_BUCKETS - max_exact)).astype(np.int32)
    return np.where(n < max_exact, n, np.minimum(large, REL_BUCKETS - 1)).astype(np.int32)


FAR_DIST = int(np.min(np.nonzero(_t5_bucket_np(np.arange(4 * REL_MAX_DIST)) == REL_BUCKETS - 1)[0]))


def _bias_rows(rel_bias, dist):
    idx = jnp.asarray(_t5_bucket_np(dist))[None]
    rel = rel_bias.astype(F32)
    out = jnp.zeros((rel.shape[1],) + dist.shape, F32)
    for b in range(REL_BUCKETS):
        out = jnp.where(idx == b, rel[b].reshape((-1,) + (1,) * dist.ndim), out)
    return out


def _toeplitz_bias(rel_bias, tq, tk, n_off):
    i = np.arange(tq)[None, :, None]
    j = np.arange(tk)[None, None, :]
    off = np.arange(n_off)[:, None, None]
    return _bias_rows(rel_bias, off * tk + i - j)


def _dense_bias(rel_bias, qpos, kpos, cols):
    tab = _bias_rows(rel_bias, qpos[:, None] - kpos[None, :])[np.asarray(cols)]
    return tab.reshape(len(cols) * len(qpos), len(kpos))


def _layer_norm(z, g, b):
    mu = jnp.mean(z, axis=-1, keepdims=True)
    zc = z - mu
    var = jnp.mean(zc * zc, axis=-1, keepdims=True)
    return zc * lax.rsqrt(var + LN_EPS) * g + b


def _nt_dot(a, b):
    return lax.dot_general(a, b, (((1,), (1,)), ((), ())), preferred_element_type=F32)


def _project_kernel(*refs, n_row, n_t):
    x_ref = refs[0]
    row_w, t_w = refs[1:1 + n_row], refs[1 + n_row:1 + n_row + n_t]
    row_o, t_o = refs[1 + n_row + n_t:1 + 2 * n_row + n_t], refs[1 + 2 * n_row + n_t:]
    xb = x_ref[...].astype(BF16)
    for w_ref, o_ref in zip(row_w, row_o):
        o_ref[...] = jnp.dot(xb, w_ref[...], preferred_element_type=F32)
    for w_ref, o_ref in zip(t_w, t_o):
        o_ref[...] = _nt_dot(w_ref[...], xb)


def project(x, row_ws, t_ws, n_seq, tm):
    M, K = x.shape
    T = M // n_seq
    nt = T // tm
    n_row, n_t = len(row_ws), len(t_ws)
    const = lambda w: pl.BlockSpec(w.shape, lambda b, i: (0, 0))
    return pl.pallas_call(
        functools.partial(_project_kernel, n_row=n_row, n_t=n_t),
        grid=(n_seq, nt),
        in_specs=[pl.BlockSpec((tm, K), lambda b, i: (b * nt + i, 0))] + [const(w) for w in row_ws + t_ws],
        out_specs=[pl.BlockSpec((tm, w.shape[1]), lambda b, i: (b * nt + i, 0)) for w in row_ws]
        + [pl.BlockSpec((None, w.shape[0], tm), lambda b, i: (b, 0, i)) for w in t_ws],
        out_shape=[jax.ShapeDtypeStruct((M, w.shape[1]), F32) for w in row_ws]
        + [jax.ShapeDtypeStruct((n_seq, w.shape[0], T), F32) for w in t_ws],
        compiler_params=_cparams("parallel", "parallel"),
        name="project",
    )(x, *row_ws, *t_ws)


def _post_mixer_kernel(o_ref, wo_ref, x_ref, ln_ref, wu_ref, wd_ref, y_ref, x1_scr, xb_scr, acc_scr, *, alpha):
    f = pl.program_id(1)

    @pl.when(f == 0)
    def _():
        y = jnp.dot(o_ref[...].astype(BF16), wo_ref[...], preferred_element_type=F32)
        x1 = _layer_norm(alpha * x_ref[...] + y, ln_ref[0:1, :], ln_ref[1:2, :])
        x1_scr[...] = x1
        xb_scr[...] = x1.astype(BF16)
        acc_scr[...] = jnp.zeros_like(acc_scr)

    h = jnp.dot(xb_scr[...], wu_ref[...], preferred_element_type=F32)
    h = jnp.square(jnp.maximum(h, 0.0))
    acc_scr[...] += jnp.dot(h.astype(BF16), wd_ref[...], preferred_element_type=F32)

    @pl.when(f == pl.num_programs(1) - 1)
    def _():
        y_ref[...] = _layer_norm(alpha * x1_scr[...] + acc_scr[...], ln_ref[2:3, :], ln_ref[3:4, :])


def post_mixer(o, w_out, x, ln, w_up, w_down, alpha, tm=512, tf=1024):
    M, D = x.shape
    Fd = w_up.shape[1]
    return pl.pallas_call(
        functools.partial(_post_mixer_kernel, alpha=alpha),
        grid=(M // tm, Fd // tf),
        in_specs=[
            pl.BlockSpec((tm, D), lambda i, f: (i, 0)),
            pl.BlockSpec((D, D), lambda i, f: (0, 0)),
            pl.BlockSpec((tm, D), lambda i, f: (i, 0)),
            pl.BlockSpec((4, D), lambda i, f: (0, 0)),
            pl.BlockSpec((D, tf), lambda i, f: (0, f)),
            pl.BlockSpec((tf, D), lambda i, f: (f, 0)),
        ],
        out_specs=pl.BlockSpec((tm, D), lambda i, f: (i, 0)),
        out_shape=jax.ShapeDtypeStruct((M, D), F32),
        scratch_shapes=[pltpu.VMEM((tm, D), F32), pltpu.VMEM((tm, D), BF16), pltpu.VMEM((tm, D), F32)],
        compiler_params=_cparams("parallel", "arbitrary"),
        name="post_mixer",
    )(o, w_out, x, ln, w_up, w_down)


LOG2E = math.log2(math.e)


def _flash_step(carry, s, pv):
    m, l, acc = carry
    m_new = jnp.maximum(m, jnp.max(s, axis=-1, keepdims=True))
    alpha = jnp.exp2(m - m_new)
    p = jnp.exp2(s - m_new)
    if l is not None:
        l = alpha * l + jnp.sum(p, axis=-1, keepdims=True)
    acc = alpha * acc + pv(p.astype(BF16))
    return m_new, l, acc


def _flash_init(rows, width, with_l=True):
    return jnp.full((rows, 1), NEG, F32), jnp.zeros((rows, 1), F32) if with_l else None, jnp.zeros((rows, width), F32)


def _split_heads(q, tq):
    lane = lax.broadcasted_iota(jnp.int32, (tq, LANES), 1)
    lo = jnp.where(lane < HEAD_DIM, q, 0.0)
    hi = jnp.where(lane >= HEAD_DIM, q, 0.0)
    return jnp.concatenate([lo, hi], axis=0).astype(BF16)


def _cols(ref, start, width):
    return ref[:, pl.ds(pl.multiple_of(start, width), width)].astype(BF16)


def _rows(ref, start, height):
    return ref[pl.ds(pl.multiple_of(start, height), height), :].astype(BF16)


def _rows_blockdiag(q, n_blk, blk_w):
    rep = jnp.concatenate([q] * n_blk, axis=0)
    r = lax.broadcasted_iota(jnp.int32, rep.shape, 0) // q.shape[0]
    c = lax.broadcasted_iota(jnp.int32, rep.shape, 1) // blk_w
    return jnp.where(r == c, rep, 0.0).astype(BF16)


def _pad_rows(a, rows):
    return jnp.concatenate([a, jnp.zeros((rows - a.shape[0], a.shape[1]), a.dtype)], axis=0)


def _page_specs(n_pages, block, index):
    return [pl.BlockSpec(block, functools.partial(lambda b, pt, p: index(pt[b, p]), p=p)) for p in range(n_pages)]


def _diff_lambda(lam_ref, lam_init):
    lf = lam_ref[...]
    a = jnp.sum(lf[0:1, :] * lf[1:2, :], axis=-1, keepdims=True)
    b = jnp.sum(lf[2:3, :] * lf[3:4, :], axis=-1, keepdims=True)
    return jnp.exp(a) - jnp.exp(b) + lam_init


def _diff_finish(o1, o2, lam_full, g, lam_init):
    o = o1 - lam_full * o2
    o = o * lax.rsqrt(jnp.mean(o * o, axis=-1, keepdims=True) + LN_EPS) * g
    return o * (1.0 - lam_init)


def _causal_bias(tq, tk):
    ri = lax.broadcasted_iota(jnp.int32, (tq, tk), 0)
    ci = lax.broadcasted_iota(jnp.int32, (tq, tk), 1)
    return jnp.where(ci <= ri, 0.0, NEG)


def _add_per_head(s, b, n_blocks):
    tq, tk = b.shape
    return (s.reshape(n_blocks, tq, tk) + b[None]).reshape(n_blocks * tq, tk)


def _diff_prompt_kernel(q_ref, kt_ref, v_ref, toe_ref, lam_ref, g_ref, o_ref, *, tq, lam_init):
    qt = pl.program_id(2)
    tk = tq
    rows = 2 * tq
    q2 = _split_heads(q_ref[...] * (HEAD_DIM ** -0.5 * LOG2E), tq)

    def qk(kt):
        return jnp.dot(q2, _cols(kt_ref, kt * tk, tk), preferred_element_type=F32)

    def step(kt, c, bias):
        carry, s = c
        s_next = qk(jnp.minimum(kt + 1, qt))
        if bias is not None:
            s = s + bias
        v = _rows(v_ref, kt * tk, tk)
        return _flash_step(carry, s, lambda p: jnp.dot(p, v, preferred_element_type=F32)), s_next

    near0 = jnp.maximum(qt - 1, 0)
    c = lax.fori_loop(0, near0, lambda kt, c: step(kt, c, None), (_flash_init(rows, LANES), qk(0)))
    carry, s = lax.fori_loop(near0, qt, lambda kt, c: step(kt, c, toe_ref[1]), c)
    v = _rows(v_ref, qt * tk, tk)
    s = _add_per_head(s + toe_ref[0], _causal_bias(tq, tk), 2)
    _, l, acc = _flash_step(carry, s, lambda p: jnp.dot(p, v, preferred_element_type=F32))
    o = acc / l
    o_ref[...] = _diff_finish(o[:tq], o[tq:], _diff_lambda(lam_ref, lam_init), g_ref[...], lam_init)


def _toeplitz_rel(rel_bias, tq, tk, n_off):
    far = rel_bias[REL_BUCKETS - 1].astype(F32)[:, None, None, None]
    return (_toeplitz_bias(rel_bias, tq, tk, n_off) - far) * LOG2E


def diff_prompt_attention(q, kt, v, rel_bias, lam, norm_g, lam_init, tq=256):
    B, T, W = q.shape
    n_heads = W // LANES
    toe = _toeplitz_rel(rel_bias, tq, tq, 2)
    toe = jnp.concatenate([toe[:n_heads], toe[n_heads:]], axis=2)
    return pl.pallas_call(
        functools.partial(_diff_prompt_kernel, tq=tq, lam_init=lam_init),
        grid=(B, n_heads, T // tq),
        in_specs=[
            pl.BlockSpec((None, tq, LANES), lambda b, h, i: (b, i, h)),
            pl.BlockSpec((None, LANES, T), lambda b, h, i: (b, h, 0)),
            pl.BlockSpec((None, T, LANES), lambda b, h, i: (b, 0, h)),
            pl.BlockSpec((None, 2, 2 * tq, tq), lambda b, h, i: (h, 0, 0, 0)),
            pl.BlockSpec((4, HEAD_DIM), lambda b, h, i: (0, 0)),
            pl.BlockSpec((1, LANES), lambda b, h, i: (0, 0)),
        ],
        out_specs=pl.BlockSpec((None, tq, LANES), lambda b, h, i: (b, i, h)),
        out_shape=jax.ShapeDtypeStruct((B, T, W), F32),
        compiler_params=_cparams("parallel", "parallel", "arbitrary"),
        name="diff_prompt",
    )(q, kt, v, toe, lam, norm_g.reshape(1, LANES))


def _diff_decode_kernel(pt_ref, q_ref, kn_ref, vn_ref, bias_ref, lam_ref, g_ref, *rest, n_pages, page, lam_init):
    kp, vp = rest[:n_pages], rest[n_pages:2 * n_pages]
    o_ref, s_scr = rest[2 * n_pages], rest[2 * n_pages + 1]
    nq, W = q_ref.shape
    n_heads = W // LANES
    hr = 2 * nq
    rows = n_heads * hr
    qbd = _rows_blockdiag(q_ref[...] * HEAD_DIM ** -0.5, 2 * n_heads, HEAD_DIM)

    def pv(p, v_of_head):
        return jnp.concatenate(
            [jnp.dot(p[h * hr:(h + 1) * hr], v_of_head(h), preferred_element_type=F32) for h in range(n_heads)], axis=0)

    m = jnp.full((rows, 1), NEG, F32)
    for p in range(n_pages):
        kt = kp[p][...].reshape(W, page).astype(BF16)
        s = jnp.dot(qbd, kt, preferred_element_type=F32) + bias_ref[:, p * page:(p + 1) * page]
        s_scr[:, p * page:(p + 1) * page] = s
        m = jnp.maximum(m, jnp.max(s, axis=-1, keepdims=True))
    ri = lax.broadcasted_iota(jnp.int32, (rows, page), 0) % nq
    ci = lax.broadcasted_iota(jnp.int32, (rows, page), 1)
    mask = ci <= ri
    kn = _pad_rows(kn_ref[...], page).astype(BF16)
    vn = _pad_rows(vn_ref[...], page).astype(BF16)
    sn = jnp.where(mask, _nt_dot(qbd, kn) + bias_ref[:, n_pages * page:], NEG)
    m = jnp.maximum(m, jnp.max(sn, axis=-1, keepdims=True))
    pn = jnp.where(mask, jnp.exp(sn - m), 0.0)
    l = jnp.sum(pn, axis=-1, keepdims=True)
    acc = pv(pn.astype(BF16), lambda h: vn[:, h * LANES:(h + 1) * LANES])
    for p in range(n_pages):
        pr = jnp.exp(s_scr[:, p * page:(p + 1) * page] - m)
        l = l + jnp.sum(pr, axis=-1, keepdims=True)
        acc = acc + pv(pr.astype(BF16), lambda h: vp[p][pl.ds(h, page, stride=n_heads), :].astype(BF16))
    o = acc / l
    lam_full = _diff_lambda(lam_ref, lam_init)
    outs = [_diff_finish(o[h * hr:h * hr + nq], o[h * hr + nq:(h + 1) * hr], lam_full, g_ref[...], lam_init)
            for h in range(n_heads)]
    o_ref[...] = jnp.concatenate(outs, axis=1)


def diff_decode_attention(q, kn, vn, kt_pool, v_pool, page_table, rel_bias, lam, norm_g, lam_init):
    S, nq, W = q.shape
    n_pages = page_table.shape[1]
    page = kt_pool.shape[-1]
    n_heads = W // LANES
    P = n_pages * page
    cols = [mp * n_heads + h for h in range(n_heads) for mp in range(2)]
    bias = _dense_bias(rel_bias, P + np.arange(nq), np.arange(P + page), cols)
    rows = bias.shape[0]
    seq = pl.BlockSpec((None, nq, W), lambda b, pt: (b, 0, 0))
    grid_spec = pltpu.PrefetchScalarGridSpec(
        num_scalar_prefetch=1,
        grid=(S,),
        in_specs=[seq, seq, seq,
                  pl.BlockSpec(bias.shape, lambda b, pt: (0, 0)),
                  pl.BlockSpec((4, HEAD_DIM), lambda b, pt: (0, 0)),
                  pl.BlockSpec((1, LANES), lambda b, pt: (0, 0))]
        + _page_specs(n_pages, (None,) + kt_pool.shape[1:], lambda pg: (pg, 0, 0, 0, 0))
        + _page_specs(n_pages, (None,) + v_pool.shape[1:], lambda pg: (pg, 0, 0)),
        out_specs=seq,
        scratch_shapes=[pltpu.VMEM((rows, P), F32)],
    )
    return pl.pallas_call(
        functools.partial(_diff_decode_kernel, n_pages=n_pages, page=page, lam_init=lam_init),
        grid_spec=grid_spec,
        out_shape=jax.ShapeDtypeStruct((S, nq, W), F32),
        compiler_params=_cparams("arbitrary"),
        name="diff_decode",
    )(page_table, q, kn, vn, bias, lam, norm_g.reshape(1, LANES), *([kt_pool] * n_pages), *([v_pool] * n_pages))


def _softplus(z):
    return jnp.maximum(z, 0.0) + jnp.log(1.0 + jnp.exp(-jnp.abs(z)))


def _suffix_sum(c, u):
    hi = c.astype(BF16)
    lo = (c - hi.astype(F32)).astype(BF16)
    return jnp.dot(hi, u, preferred_element_type=F32) + jnp.dot(lo, u, preferred_element_type=F32)


def _sb_local(z, u, mask=None):
    c = _softplus(z)
    if mask is not None:
        c = jnp.where(mask, c, 0.0)
    return z - c - _suffix_sum(c, u), jnp.sum(c, axis=-1, keepdims=True)


def _sb_weights(e, r, mask=None):
    a = jnp.exp(e - r)
    if mask is not None:
        a = jnp.where(mask, a, 0.0)
    return a.astype(BF16)


def _strict_upper(n):
    return jnp.asarray(np.arange(n)[:, None] > np.arange(n)[None, :], BF16)


def _sb_prompt_kernel(q_ref, kt_ref, vt_ref, u_ref, o_ref, *, tq, tk):
    qt = pl.program_id(2)
    n_sub = tq // tk
    rows = 2 * tq
    q2 = _split_heads(q_ref[...] * HEAD_DIM ** -0.5, tq)
    u = u_ref[...]

    def block(kt, carry, diag):
        r, acc = carry
        local = []
        for sub in range(n_sub):
            mask = None
            if diag:
                ri = lax.broadcasted_iota(jnp.int32, (rows, tk), 0) % tq
                ci = lax.broadcasted_iota(jnp.int32, (rows, tk), 1) + sub * tk
                mask = ci < ri
            z = jnp.dot(q2, _cols(kt_ref, kt * tq + sub * tk, tk), preferred_element_type=F32)
            local.append(_sb_local(z, u, mask) + (mask,))
        for sub in reversed(range(n_sub)):
            e, tot, mask = local[sub]
            acc = acc + _nt_dot(_sb_weights(e, r, mask), _cols(vt_ref, kt * tq + sub * tk, tk))
            r = r + tot
        return r, acc

    r, acc = block(qt, (jnp.zeros((rows, 1), F32), jnp.zeros((rows, LANES), F32)), True)

    def cond(c):
        return jnp.logical_and(c[0] >= 0, jnp.min(c[1]) < SB_DEAD)

    def body(c):
        r, acc = block(c[0], (c[1], c[2]), False)
        return c[0] - 1, r, acc

    _, _, acc = lax.while_loop(cond, body, (qt - 1, r, acc))
    lane = lax.broadcasted_iota(jnp.int32, (tq, LANES), 1)
    o_ref[...] = jnp.where(lane < HEAD_DIM, acc[:tq], acc[tq:])


def sb_prompt_attention(q, kvt, tq=512, tk=256):
    B, T, W = q.shape
    n_pairs = W // LANES
    return pl.pallas_call(
        functools.partial(_sb_prompt_kernel, tq=tq, tk=tk),
        grid=(B, n_pairs, T // tq),
        in_specs=[
            pl.BlockSpec((None, tq, LANES), lambda b, h, i: (b, i, h)),
            pl.BlockSpec((None, LANES, T), lambda b, h, i: (b, h, 0)),
            pl.BlockSpec((None, LANES, T), lambda b, h, i: (b, n_pairs + h, 0)),
            pl.BlockSpec((tk, tk), lambda b, h, i: (0, 0)),
        ],
        out_specs=pl.BlockSpec((None, tq, LANES), lambda b, h, i: (b, i, h)),
        out_shape=jax.ShapeDtypeStruct((B, T, W), F32),
        compiler_params=_cparams("parallel", "parallel", "arbitrary"),
        name="sb_prompt",
    )(q, kvt, kvt, _strict_upper(tk))


def _sb_decode_kernel(pt_ref, q_ref, kvn_ref, u_ref, *rest, n_pages, page):
    pages, o_ref, e_scr = rest[:n_pages], rest[n_pages], rest[n_pages + 1]
    nq, W = q_ref.shape
    n_heads = W // HEAD_DIM
    rows = n_heads * nq
    u = u_ref[...]
    qbd = _rows_blockdiag(q_ref[...] * HEAD_DIM ** -0.5, n_heads, HEAD_DIM)
    kvn = _pad_rows(kvn_ref[...], page).astype(BF16)
    ri = lax.broadcasted_iota(jnp.int32, (rows, page), 0) % nq
    ci = lax.broadcasted_iota(jnp.int32, (rows, page), 1)
    new_mask = ci < ri
    e_new, r = _sb_local(_nt_dot(qbd, kvn[:, :W]), u, new_mask)
    tots = []
    for p in range(n_pages):
        kt = pages[p][0].reshape(W, page).astype(BF16)
        e, tot = _sb_local(jnp.dot(qbd, kt, preferred_element_type=F32), u)
        e_scr[:, p * page:(p + 1) * page] = e
        tots.append(tot)
    acc = jnp.dot(_sb_weights(e_new, 0.0, new_mask), kvn[:, W:], preferred_element_type=F32)
    for p in reversed(range(n_pages)):
        vt = pages[p][1].reshape(W, page).astype(BF16)
        acc = acc + _nt_dot(_sb_weights(e_scr[:, p * page:(p + 1) * page], r), vt)
        r = r + tots[p]
    col = lax.broadcasted_iota(jnp.int32, (nq, W), 1) // HEAD_DIM
    o = jnp.zeros((nq, W), F32)
    for h in range(n_heads):
        o = o + jnp.where(col == h, acc[h * nq:(h + 1) * nq, :], 0.0)
    o_ref[...] = o


def sb_decode_attention(q, kvn, kvt_pool, page_table):
    S, nq, W = q.shape
    n_pages = page_table.shape[1]
    page = kvt_pool.shape[-1]
    grid_spec = pltpu.PrefetchScalarGridSpec(
        num_scalar_prefetch=1,
        grid=(S,),
        in_specs=[pl.BlockSpec((None, nq, W), lambda b, pt: (b, 0, 0)),
                  pl.BlockSpec((None, nq, 2 * W), lambda b, pt: (b, 0, 0)),
                  pl.BlockSpec((page, page), lambda b, pt: (0, 0))]
        + _page_specs(n_pages, (None,) + kvt_pool.shape[1:], lambda pg: (pg, 0, 0, 0, 0)),
        out_specs=pl.BlockSpec((None, nq, W), lambda b, pt: (b, 0, 0)),
        scratch_shapes=[pltpu.VMEM((W // HEAD_DIM * nq, n_pages * page), F32)],
    )
    return pl.pallas_call(
        functools.partial(_sb_decode_kernel, n_pages=n_pages, page=page),
        grid_spec=grid_spec,
        out_shape=jax.ShapeDtypeStruct((S, nq, W), F32),
        compiler_params=_cparams("arbitrary"),
        name="sb_decode",
    )(page_table, q, kvn, _strict_upper(page), *([kvt_pool] * n_pages))


N_CMP_TILES = 2 * NSA_KV_HEADS * HEAD_DIM // LANES


def _gelu_tanh(x):
    return 0.5 * x * (1.0 + jnp.tanh(math.sqrt(2.0 / math.pi) * (x + 0.044715 * x * x * x)))


def _compress_body(chunk_rows, w1_ref, pos_ref, w2_ref, o_ref, n_chunks):
    n_kinds = w1_ref.shape[0]
    tiles_per_kind = NSA_KV_HEADS * HEAD_DIM // LANES
    for kind in range(n_kinds):
        xcat = jnp.concatenate(
            [jnp.concatenate([chunk_rows(l, kind * tiles_per_kind + t) for l in range(CMP_STRIDE)], axis=1)
             for t in range(tiles_per_kind)], axis=0)
        u = []
        for a in range(2):
            xa = (xcat + pos_ref[kind, a]).astype(BF16)
            u.append(jnp.dot(xa, w1_ref[kind, a], preferred_element_type=F32))
        for t in range(tiles_per_kind):
            u0 = u[0][t * n_chunks:(t + 1) * n_chunks]
            u1 = pltpu.roll(u[1][t * n_chunks:(t + 1) * n_chunks], n_chunks - 1, 0)
            hid = _gelu_tanh(u0 + u1)
            out = jnp.dot(hid.astype(BF16), w2_ref[kind], preferred_element_type=F32)
            col = (kind * tiles_per_kind + t) * LANES
            o_ref[:, col:col + LANES] = out


def _compress_seq_kernel(*refs, n_chunks):
    x_refs = refs[:N_CMP_TILES]
    w1_ref, pos_ref, w2_ref, o_ref = refs[N_CMP_TILES:]

    def chunk_rows(l, tile):
        return x_refs[tile][pl.ds(l, n_chunks, stride=CMP_STRIDE), :]
    _compress_body(chunk_rows, w1_ref, pos_ref, w2_ref, o_ref, n_chunks)


def _compress_paged_kernel(pt_ref, w1_ref, pos_ref, w2_ref, *rest, n_pages, page):
    pages, o_ref, x_scr = rest[:n_pages], rest[n_pages], rest[n_pages + 1]
    tiles_per_kind = NSA_KV_HEADS * HEAD_DIM // LANES
    for p in range(n_pages):
        for t in range(N_CMP_TILES):
            kind, pair = t // tiles_per_kind, t % tiles_per_kind
            xt = pages[p][kind, 2 * pair:2 * pair + 2].reshape(LANES, page)
            x_scr[t, p * page:(p + 1) * page, :] = xt.T
    n_chunks = n_pages * page // CMP_STRIDE

    def chunk_rows(l, tile):
        return x_scr[tile, pl.ds(l, n_chunks, stride=CMP_STRIDE), :]
    _compress_body(chunk_rows, w1_ref, pos_ref, w2_ref, o_ref, n_chunks)


def _compress_weights(cmp_pos, cmp_w1, cmp_w2):
    eye2 = jnp.eye(2, dtype=F32)
    w1r = cmp_w1.reshape(2, 2, CMP_STRIDE, HEAD_DIM, -1)
    w1e = jnp.einsum('kaldj,hg->kalhdgj', w1r, eye2)
    w1e = w1e.reshape(2, 2, CMP_STRIDE * LANES, 2 * cmp_w1.shape[-1]).astype(BF16)
    pos = jnp.broadcast_to(cmp_pos.reshape(2, 2, CMP_STRIDE, 1, HEAD_DIM), (2, 2, CMP_STRIDE, 2, HEAD_DIM))
    pos = pos.reshape(2, 2, 1, CMP_STRIDE * LANES).astype(F32)
    w2e = jnp.einsum('kjd,hg->khjgd', cmp_w2, eye2).reshape(2, 2 * cmp_w2.shape[1], LANES).astype(BF16)
    return w1e, pos, w2e


def compress_seq(kv, cmp_w):
    B, L, _ = kv.shape
    w1e, pos, w2e = cmp_w
    n_chunks = L // CMP_STRIDE
    W = N_CMP_TILES * LANES
    return pl.pallas_call(
        functools.partial(_compress_seq_kernel, n_chunks=n_chunks),
        grid=(B,),
        in_specs=[pl.BlockSpec((None, L, LANES), functools.partial(lambda b, t: (b, 0, t), t=t)) for t in range(N_CMP_TILES)]
        + [pl.BlockSpec(w1e.shape, lambda b: (0, 0, 0, 0)),
           pl.BlockSpec(pos.shape, lambda b: (0, 0, 0, 0)),
           pl.BlockSpec(w2e.shape, lambda b: (0, 0, 0))],
        out_specs=pl.BlockSpec((None, n_chunks, W), lambda b: (b, 0, 0)),
        out_shape=jax.ShapeDtypeStruct((B, n_chunks, W), F32),
        compiler_params=_cparams("parallel"),
        name="nsa_compress_seq",
    )(*([kv] * N_CMP_TILES), w1e, pos, w2e)


def compress_paged(pool_t, layer, page_table, cmp_w):
    S, n_pages = page_table.shape
    page = pool_t.shape[-1]
    w1e, pos, w2e = cmp_w
    n_chunks = n_pages * page // CMP_STRIDE
    W = N_CMP_TILES * LANES
    blk = (None, None, 2) + pool_t.shape[3:]
    grid_spec = pltpu.PrefetchScalarGridSpec(
        num_scalar_prefetch=1,
        grid=(S,),
        in_specs=[pl.BlockSpec(w1e.shape, lambda b, pt: (0, 0, 0, 0)),
                  pl.BlockSpec(pos.shape, lambda b, pt: (0, 0, 0, 0)),
                  pl.BlockSpec(w2e.shape, lambda b, pt: (0, 0, 0))]
        + _page_specs(n_pages, blk, lambda pg: (layer, pg, 0, 0, 0, 0)),
        out_specs=pl.BlockSpec((None, n_chunks, W), lambda b, pt: (b, 0, 0)),
        scratch_shapes=[pltpu.VMEM((N_CMP_TILES, n_pages * page, LANES), F32)],
    )
    return pl.pallas_call(
        functools.partial(_compress_paged_kernel, n_pages=n_pages, page=page),
        grid_spec=grid_spec,
        out_shape=jax.ShapeDtypeStruct((S, n_chunks, W), F32),
        compiler_params=_cparams("arbitrary"),
        name="nsa_compress_paged",
    )(page_table, w1e, pos, w2e, *([pool_t] * n_pages))


def _masked_softmax(s, mask, exp=jnp.exp):
    s = jnp.where(mask, s, NEG)
    m = jnp.max(s, axis=-1, keepdims=True)
    e = jnp.where(mask, exp(s - m), 0.0)
    return e * (1.0 / jnp.maximum(jnp.sum(e, axis=-1, keepdims=True), 1e-30))


def _importance_t(ov_t, psum):
    hi = psum.astype(BF16)
    lo = (psum - hi.astype(F32)).astype(BF16)
    return _nt_dot(ov_t, hi) + _nt_dot(ov_t, lo)


def _select_topk_t(imp, n_rows):
    jj = lax.broadcasted_iota(jnp.int32, imp.shape, 0)
    cnt = jnp.zeros(imp.shape, F32)
    for i in range(n_rows):
        row = imp[i:i + 1, :]
        ahead = (row > imp) | ((row == imp) & (jj > i))
        cnt = cnt + jnp.where(ahead, 1.0, 0.0)
    return jnp.where(cnt < N_SEL, 1.0, 0.0)


def _select_topk_ref(imp_ref, n_active):
    imp = imp_ref[...]
    jj = lax.broadcasted_iota(jnp.int32, imp.shape, 0)

    def body(i, cnt):
        row = imp_ref[pl.ds(i, 1), :]
        ahead = (row > imp) | ((row == imp) & (jj > i))
        return cnt + jnp.where(ahead, 1.0, 0.0)

    cnt = lax.fori_loop(0, n_active, body, jnp.zeros(imp.shape, F32))
    return jnp.where(cnt < N_SEL, 1.0, 0.0)


def _block_expand(first_block, n_blocks, tk):
    j = lax.broadcasted_iota(jnp.int32, (n_blocks, tk), 0)
    col = lax.broadcasted_iota(jnp.int32, (n_blocks, tk), 1)
    return jnp.where(j == first_block + col // SEL_BLOCK, 1.0, 0.0).astype(BF16)


CMP_NEAR_BACK = -(-(CMP_LEN - 1 + FAR_DIST) // CMP_STRIDE)


def _nsa_prompt_kernel(q_ref, kst_ref, vst_ref, kwt_ref, vwt_ref, kc_ref, vc_ref, gate_ref, pat_ref, toe_ref,
                       ovt_ref, o_ref, imp_scr, *, tq, n_cmp):
    g = pl.program_id(1)
    qt = pl.program_id(2)
    tk = tq
    R = NSA_HPG
    par = g % 2
    lane = lax.broadcasted_iota(jnp.int32, (tq, LANES), 1)
    own = (lane // HEAD_DIM) == par
    q = q_ref[...] * (HEAD_DIM ** -0.5 * LOG2E)
    parts = []
    for r in range(R):
        t = q[:, (r // 2) * LANES:(r // 2 + 1) * LANES]
        src = jnp.where(par == r % 2, t, pltpu.roll(t, HEAD_DIM, 1))
        parts.append(jnp.where(own, src, 0.0))
    q4 = jnp.concatenate(parts, axis=0).astype(BF16)
    rows = R * tq

    def toe(off):
        return toe_ref[:, off].reshape(rows, tk)

    n_c = kc_ref.shape[0]
    cc = lax.broadcasted_iota(jnp.int32, (rows, n_c), 1)
    tt = qt * tq + lax.broadcasted_iota(jnp.int32, (rows, n_c), 0) % tq
    mask_c = (cc * CMP_STRIDE + CMP_LEN - 1 <= tt) & (cc < n_cmp)
    pat = pat_ref[...].reshape(rows, LANES)
    uu = lax.broadcasted_iota(jnp.int32, (LANES, n_c), 0)
    shift = jnp.where(lax.broadcasted_iota(jnp.int32, (LANES, n_c), 1) == qt * (tq // CMP_STRIDE) - CMP_NEAR_BACK + uu,
                      1.0, 0.0).astype(BF16)
    s_c = _nt_dot(q4, kc_ref[...].astype(BF16)) + jnp.dot(pat.astype(BF16), shift, preferred_element_type=F32)
    p_c = _masked_softmax(s_c, mask_c, jnp.exp2)
    o_c = jnp.dot(p_c.astype(BF16), vc_ref[...].astype(BF16), preferred_element_type=F32)
    psum = p_c[0:tq]
    for r in range(1, R):
        psum = psum + p_c[r * tq:(r + 1) * tq]
    n_blk = ovt_ref.shape[0]
    imp = _importance_t(ovt_ref[...], psum)
    jj = lax.broadcasted_iota(jnp.int32, (n_blk, tq), 0)
    tq_pos = qt * tq + lax.broadcasted_iota(jnp.int32, (n_blk, tq), 1)
    t_blk = tq_pos // SEL_BLOCK
    forced = (jj == 0) | (jj == t_blk) | (jj == t_blk - 1)
    imp_scr[...] = jnp.where(jj * SEL_BLOCK <= tq_pos, imp + jnp.where(forced, SEL_FORCE, 0.0), -1.0)
    sel_t = _select_topk_ref(imp_scr, jnp.minimum((qt + 1) * (tq // SEL_BLOCK), n_blk))
    sel = _pad_rows(sel_t, LANES).T
    sel_neg = ((sel - 1.0) * -NEG).astype(BF16)

    own_rows = (lax.broadcasted_iota(jnp.int32, (LANES, 1), 0) // HEAD_DIM) == par

    def branch(lhs, rhs, v_ref, first, tile_bias):
        def qk(kt):
            return jnp.dot(lhs, rhs(kt), preferred_element_type=F32)

        def finish_tile(kt, carry, s, toe_off, extra):
            b = tile_bias(kt)
            if extra is not None:
                b = extra if b is None else b + extra
            if toe_off is not None:
                s = s + toe(toe_off)
            if b is not None:
                s = _add_per_head(s, b, R)
            vt = jnp.where(own_rows, _cols(v_ref, kt * tk, tk), 1.0)
            return _flash_step(carry, s, lambda p: _nt_dot(p, vt))

        def step(kt, carry, toe_off):
            return finish_tile(kt, carry, qk(kt), toe_off, None)

        carry = _flash_init(rows, LANES, with_l=False)
        carry = lax.fori_loop(first, near0, lambda kt, c: step(kt, c, None), carry)
        carry = lax.fori_loop(jnp.maximum(near0, first), qt, lambda kt, c: step(kt, c, 1), carry)
        _, _, acc = finish_tile(qt, carry, qk(qt), 0, _causal_bias(tq, tk))
        return acc / pltpu.roll(acc, HEAD_DIM, 1)

    near0 = jnp.maximum(qt - 1, 0)

    sel_lhs = jnp.concatenate([q4, jnp.concatenate([sel_neg] * R, axis=0)], axis=1)

    def sel_rhs(kt):
        return jnp.concatenate([_cols(kst_ref, kt * tk, tk), _block_expand(kt * (tk // SEL_BLOCK), LANES, tk)], axis=0)

    o_s = branch(sel_lhs, sel_rhs, vst_ref, 0, lambda kt: None)

    n_back = WINDOW // tk
    ri = lax.broadcasted_iota(jnp.int32, (tq, tk), 0)
    ci = lax.broadcasted_iota(jnp.int32, (tq, tk), 1)

    def win_bias(kt):
        return jnp.where((ci > ri) | (kt != qt - n_back), 0.0, NEG)

    o_w = branch(q4, lambda kt: _cols(kwt_ref, kt * tk, tk), vwt_ref, jnp.maximum(qt - n_back, 0), win_bias)

    gs = jax.nn.sigmoid(gate_ref[...])
    n_h = NSA_KV_HEADS * R
    outs = []
    for r in range(R):
        o_r = jnp.zeros((tq, LANES), F32)
        for br, o_b in enumerate((o_c, o_s, o_w)):
            gcol = jnp.sum(jnp.where(lane == br * n_h + g * R + r, gs, 0.0), axis=-1, keepdims=True)
            o_r = o_r + gcol * o_b[r * tq:(r + 1) * tq]
        outs.append(jnp.where(par == r % 2, o_r, pltpu.roll(o_r, HEAD_DIM, 1)))
    for u in range(R // 2):
        o_ref[:, u * LANES:(u + 1) * LANES] = jnp.where(lane < HEAD_DIM, outs[2 * u], outs[2 * u + 1])


def _overlap_t(n_blk_rows, n_cmp_cols, n_cmp, n_slc):
    c0 = np.arange(n_cmp_cols)[None, :] * CMP_STRIDE
    j0 = np.arange(n_blk_rows)[:, None] * SEL_BLOCK
    ov = (c0 < j0 + SEL_BLOCK) & (c0 + CMP_LEN > j0)
    ov &= (np.arange(n_cmp_cols)[None, :] < n_cmp) & (np.arange(n_blk_rows)[:, None] < n_slc)
    return jnp.asarray(ov, BF16)


def nsa_prompt_attention(q, kvt, wint, cmp, gates, rel_bias, tq=256):
    B, T, W = q.shape
    G, R = NSA_KV_HEADS, NSA_HPG
    n_cmp = (T - CMP_LEN) // CMP_STRIDE + 1
    n_c = cmp.shape[1]
    n_slc = -(-T // SEL_BLOCK)
    n_blk = -(-n_slc // 8) * 8
    assert WINDOW % tq == 0 and n_blk <= LANES
    toe = _toeplitz_rel(rel_bias, tq, tq, 2)
    c31 = rel_bias[REL_BUCKETS - 1].astype(F32)
    n_near = CMP_NEAR_BACK + (tq - CMP_LEN) // CMP_STRIDE + 1
    assert n_near <= LANES and tq % CMP_STRIDE == 0
    dist = np.arange(tq)[:, None] - (CMP_LEN - 1) - CMP_STRIDE * (np.arange(LANES)[None, :] - CMP_NEAR_BACK)
    pat = (_bias_rows(rel_bias, dist) - c31[:, None, None]) * jnp.asarray((np.arange(LANES) < n_near) * LOG2E, F32)
    ovt = _overlap_t(n_blk, n_c, n_cmp, n_slc)
    pair_rows = lambda base: pl.BlockSpec((None, LANES, T), lambda b, g, i: (b, base + g // 2, 0))
    pair_cols = lambda base: pl.BlockSpec((None, n_c, LANES), lambda b, g, i: (b, 0, base + g // 2))
    return pl.pallas_call(
        functools.partial(_nsa_prompt_kernel, tq=tq, n_cmp=n_cmp),
        grid=(B, G, T // tq),
        in_specs=[
            pl.BlockSpec((None, tq, R * HEAD_DIM), lambda b, g, i: (b, i, g)),
            pair_rows(4), pair_rows(6), pair_rows(0), pair_rows(2),
            pair_cols(0), pair_cols(2),
            pl.BlockSpec((None, tq, LANES), lambda b, g, i: (b, i, 0)),
            pl.BlockSpec((R, tq, LANES), lambda b, g, i: (g, 0, 0)),
            pl.BlockSpec((R, 2, tq, tq), lambda b, g, i: (g, 0, 0, 0)),
            pl.BlockSpec(ovt.shape, lambda b, g, i: (0, 0)),
        ],
        out_specs=pl.BlockSpec((None, tq, R * HEAD_DIM), lambda b, g, i: (b, i, g)),
        out_shape=jax.ShapeDtypeStruct((B, T, W), F32),
        scratch_shapes=[pltpu.VMEM((n_blk, tq), F32)],
        compiler_params=_cparams("parallel", "parallel", "arbitrary"),
        name="nsa_prompt",
    )(q, kvt, kvt, wint, wint, cmp, cmp, gates, pat, toe, ovt)


def _nsa_decode_kernel(pt_ref, q_ref, kvn_ref, wn_ref, gate_ref, cmp_ref, win_ref, bc_ref, bs_ref, bw_ref, ovt_ref,
                       *rest, n_pages, page, n_cmp):
    pages = rest[:n_pages]
    o_ref, wout_ref, s_scr = rest[n_pages], rest[n_pages + 1], rest[n_pages + 2]
    G, R = NSA_KV_HEADS, NSA_HPG
    nq = q_ref.shape[0]
    n_h = G * R
    rows = n_h * nq
    GW = G * HEAD_DIM
    P = n_pages * page
    n_win = win_ref.shape[-1]
    lane8 = lax.broadcasted_iota(jnp.int32, (nq, LANES), 1)

    q = q_ref[...] * HEAD_DIM ** -0.5
    blocks = []
    for h in range(n_h):
        g = h // R
        t = q[:, (h // 2) * LANES:(h // 2 + 1) * LANES]
        if h % 2 != g % 2:
            t = pltpu.roll(t, HEAD_DIM, 1)
        t = jnp.where((lane8 // HEAD_DIM) == g % 2, t, 0.0)
        z = jnp.zeros((nq, LANES), F32)
        blocks.append(jnp.concatenate([t, z] if g // 2 == 0 else [z, t], axis=1))
    qg = jnp.concatenate(blocks, axis=0).astype(BF16)

    ri = lax.broadcasted_iota(jnp.int32, (rows, page), 0) % nq
    ci = lax.broadcasted_iota(jnp.int32, (rows, page), 1)
    new_mask = ci <= ri

    n_c = cmp_ref.shape[0]
    cmpv = cmp_ref[...]
    cc = lax.broadcasted_iota(jnp.int32, (rows, n_c), 1)
    s_c = _nt_dot(qg, cmpv[:, :GW].astype(BF16)) + bc_ref[...]
    p_c = _masked_softmax(s_c, cc < n_cmp)
    o_c = jnp.dot(p_c.astype(BF16), cmpv[:, GW:].astype(BF16), preferred_element_type=F32)
    ps = []
    for g in range(G):
        acc = p_c[g * R * nq:(g * R + 1) * nq]
        for r in range(1, R):
            acc = acc + p_c[(g * R + r) * nq:(g * R + r + 1) * nq]
        ps.append(acc)
    psum = _pad_rows(jnp.concatenate(ps, axis=0), LANES)
    n_blk = ovt_ref.shape[0]
    imp = _importance_t(ovt_ref[...], psum)
    jj = lax.broadcasted_iota(jnp.int32, (n_blk, LANES), 0)
    t_pos = P + lax.broadcasted_iota(jnp.int32, (n_blk, LANES), 1) % nq
    t_blk = t_pos // SEL_BLOCK
    forced = (jj == 0) | (jj == t_blk) | (jj == t_blk - 1)
    imp = jnp.where(jj * SEL_BLOCK <= t_pos, imp + jnp.where(forced, SEL_FORCE, 0.0), -1.0)
    sel_t = _select_topk_t(imp, n_blk)
    sel = _pad_rows(sel_t, LANES).T
    sel_rows = jnp.concatenate([sel[g * nq:(g + 1) * nq] for g in range(G) for _ in range(R)], axis=0).astype(BF16)

    per_tile = page // SEL_BLOCK
    m = jnp.full((rows, 1), NEG, F32)
    for p in range(n_pages):
        kt = pages[p][0].reshape(GW, page).astype(BF16)
        msk = jnp.dot(sel_rows, _block_expand(p * per_tile, LANES, page), preferred_element_type=F32) > 0.5
        s = jnp.where(msk, jnp.dot(qg, kt, preferred_element_type=F32) + bs_ref[:, p * page:(p + 1) * page], NEG)
        s_scr[:, p * page:(p + 1) * page] = s
        m = jnp.maximum(m, jnp.max(s, axis=-1, keepdims=True))
    kvn = _pad_rows(kvn_ref[...], page)
    msk = (jnp.dot(sel_rows, _block_expand(n_pages * per_tile, LANES, page), preferred_element_type=F32) > 0.5) & new_mask
    sn = jnp.where(msk, _nt_dot(qg, kvn[:, 2 * GW:3 * GW].astype(BF16)) + bs_ref[:, P:], NEG)
    m = jnp.maximum(m, jnp.max(sn, axis=-1, keepdims=True))
    pn = jnp.where(msk, jnp.exp(sn - m), 0.0)
    l = jnp.sum(pn, axis=-1, keepdims=True)
    acc = jnp.dot(pn.astype(BF16), kvn[:, 3 * GW:].astype(BF16), preferred_element_type=F32)
    for p in range(n_pages):
        pr = jnp.exp(s_scr[:, p * page:(p + 1) * page] - m)
        l = l + jnp.sum(pr, axis=-1, keepdims=True)
        acc = acc + _nt_dot(pr.astype(BF16), pages[p][1].reshape(GW, page).astype(BF16))
    o_s = acc / l

    kwt = win_ref[0].reshape(GW, n_win)
    vwt = win_ref[1].reshape(GW, n_win)
    rw = lax.broadcasted_iota(jnp.int32, (rows, n_win), 0) % nq
    cw = lax.broadcasted_iota(jnp.int32, (rows, n_win), 1)
    mask_w = (n_win + rw - cw) < WINDOW
    s_w = jnp.where(mask_w, jnp.dot(qg, kwt.astype(BF16), preferred_element_type=F32) + bw_ref[:, :n_win], NEG)
    wn = _pad_rows(wn_ref[...], page)
    s_n = jnp.where(new_mask, _nt_dot(qg, wn[:, :GW].astype(BF16)) + bw_ref[:, n_win:], NEG)
    m = jnp.maximum(jnp.max(s_w, axis=-1, keepdims=True), jnp.max(s_n, axis=-1, keepdims=True))
    p_w = jnp.where(mask_w, jnp.exp(s_w - m), 0.0)
    p_n = jnp.where(new_mask, jnp.exp(s_n - m), 0.0)
    l = jnp.sum(p_w, axis=-1, keepdims=True) + jnp.sum(p_n, axis=-1, keepdims=True)
    o_w = (_nt_dot(p_w.astype(BF16), vwt.astype(BF16))
           + jnp.dot(p_n.astype(BF16), wn[:, GW:].astype(BF16), preferred_element_type=F32)) / l

    gs = jax.nn.sigmoid(gate_ref[...])
    grep = jnp.concatenate([gs] * n_h, axis=0)
    glane = lax.broadcasted_iota(jnp.int32, (rows, LANES), 1)
    ghead = lax.broadcasted_iota(jnp.int32, (rows, LANES), 0) // nq
    o = jnp.zeros((rows, GW), F32)
    for br, o_b in enumerate((o_c, o_s, o_w)):
        o = o + jnp.sum(jnp.where(glane == br * n_h + ghead, grep, 0.0), axis=-1, keepdims=True) * o_b
    pieces = []
    for h in range(n_h):
        g = h // R
        t = o[h * nq:(h + 1) * nq, (g // 2) * LANES:(g // 2 + 1) * LANES]
        pieces.append(t if h % 2 == g % 2 else pltpu.roll(t, HEAD_DIM, 1))
    o_ref[...] = jnp.concatenate(
        [jnp.where(lane8 < HEAD_DIM, pieces[2 * u], pieces[2 * u + 1]) for u in range(n_h // 2)], axis=1)

    wt = win_ref[...].reshape(2 * GW, n_win)
    wnt = wn.T
    wout_ref[...] = jnp.concatenate([wt[:, nq:], wnt[:, :nq]], axis=1).reshape(wout_ref.shape)


def nsa_decode_attention(q, kvn, wn, gates, cmp, win_t, layer, pool_t, page_table, rel_bias):
    S, nq, W = q.shape
    G, R = NSA_KV_HEADS, NSA_HPG
    n_pages = page_table.shape[1]
    page = pool_t.shape[-1]
    P = n_pages * page
    n_win = win_t.shape[-1]
    assert nq < CMP_STRIDE and n_win == WINDOW and nq <= 8
    L = P + nq
    n_cmp = (L - CMP_LEN) // CMP_STRIDE + 1
    n_c = cmp.shape[1]
    n_slc = -(-L // SEL_BLOCK)
    n_blk = -(-n_slc // 8) * 8
    heads = range(G * R)
    qpos = P + np.arange(nq)
    bias_c = _dense_bias(rel_bias, qpos, np.arange(n_c) * CMP_STRIDE + CMP_LEN - 1, heads)
    bias_s = _dense_bias(rel_bias, qpos, np.arange(P + page), heads)
    bias_w = _dense_bias(rel_bias, qpos, P - n_win + np.arange(n_win + page), heads)
    ovt = _overlap_t(n_blk, n_c, n_cmp, n_slc)
    rows = G * R * nq
    const = lambda a: pl.BlockSpec(a.shape, lambda b, pt: (0,) * a.ndim)
    seq = lambda a: pl.BlockSpec((None,) + a.shape[1:], lambda b, pt: (b,) + (0,) * (a.ndim - 1))
    win_blk = (None, None) + win_t.shape[2:]
    grid_spec = pltpu.PrefetchScalarGridSpec(
        num_scalar_prefetch=1,
        grid=(S,),
        in_specs=[seq(q), seq(kvn), seq(wn), seq(gates), seq(cmp),
                  pl.BlockSpec(win_blk, lambda b, pt: (layer, b, 0, 0, 0, 0)),
                  const(bias_c), const(bias_s), const(bias_w), const(ovt)]
        + _page_specs(n_pages, (None, None, 2) + pool_t.shape[3:], lambda pg: (layer, pg, 1, 0, 0, 0)),
        out_specs=[seq(q), pl.BlockSpec((None,) + win_t.shape[2:], lambda b, pt: (b, 0, 0, 0, 0))],
        scratch_shapes=[pltpu.VMEM((rows, P), F32)],
    )
    return pl.pallas_call(
        functools.partial(_nsa_decode_kernel, n_pages=n_pages, page=page, n_cmp=n_cmp),
        grid_spec=grid_spec,
        out_shape=[jax.ShapeDtypeStruct(q.shape, F32), jax.ShapeDtypeStruct(win_t.shape[1:], F32)],
        compiler_params=_cparams("arbitrary"),
        name="nsa_decode",
    )(page_table, q, kvn, wn, gates, cmp, win_t, bias_c, bias_s, bias_w, ovt, *([pool_t] * n_pages))


def _w_cols(w, c0, width):
    piece = w[:, c0:c0 + width]
    if width % LANES:
        piece = jnp.pad(piece, ((0, 0), (0, LANES - width % LANES)))
    return piece.astype(BF16)


def _w_rows(w, c0, width):
    return w[:, c0:c0 + width].T.astype(BF16)


def _seq_major(a, nq, S):
    return jnp.transpose(a.reshape(nq, S, -1), (1, 0, 2))


def kernel(x_prompt, x_sample, cache_nsa_kv, state_nsa_win, cache_diff_k, cache_diff_v, cache_sb_kv, page_table, rel_bias, nsa_w_in, nsa_cmp_pos, nsa_cmp_w1, nsa_cmp_w2, nsa_w_out, diff_w_in, diff_lambda, diff_norm_g, diff_w_out, sb_w_in, sb_w_out, mlp_w_up, mlp_w_down, ln_g, ln_b):
    B, T, D = x_prompt.shape
    S, nq, _ = x_sample.shape
    depth = mlp_w_up.shape[0]
    alpha = (2 * depth) ** 0.25
    G, R, dk = NSA_KV_HEADS, NSA_HPG, HEAD_DIM
    n_pool, page = cache_nsa_kv.shape[1], cache_nsa_kv.shape[2]
    Hd, H = D // (2 * dk), D // dk
    xp = x_prompt.reshape(B * T, D)
    xs = jnp.transpose(x_sample, (1, 0, 2)).reshape(nq * S, D)
    nsa_pool_t = jnp.transpose(cache_nsa_kv, (0, 1, 3, 4, 5, 2))
    nsa_win_t = jnp.transpose(state_nsa_win, (0, 1, 3, 4, 5, 2))
    diff_k_t = jnp.transpose(cache_diff_k, (0, 1, 3, 4, 5, 2))
    sb_pool_t = jnp.transpose(cache_sb_kv, (0, 1, 3, 4, 5, 2))
    res = {k: [] for k in ("nsa_kv_p", "nsa_kv_s", "nsa_win_p", "nsa_win_s", "diff_k_p", "diff_k_s",
                           "diff_v_p", "diff_v_s", "sb_kv_p", "sb_kv_s")}
    tm_p = 512

    def to_tokens(a_t, lead):
        n, _, t = a_t.shape
        nd = len(lead)
        return jnp.transpose(a_t.reshape((n,) + lead + (t,)), (0, nd + 1) + tuple(range(1, nd + 1)))

    for i in range(depth):
        kind, j = i % N_MIXERS, i // N_MIXERS
        if kind == 0:
            w = nsa_w_in[j]
            nq_c, kv_c, win_c = G * R * dk, 4 * G * dk, 2 * G * dk
            row_p = [_w_cols(w, 0, nq_c), _w_cols(w, nq_c, 2 * G * dk), _w_cols(w, nq_c + kv_c + win_c, 3 * G * R)]
            t_ws = [_w_rows(w, nq_c, kv_c), _w_rows(w, nq_c + kv_c, win_c)]
            cmp_w = _compress_weights(nsa_cmp_pos[j], nsa_cmp_w1[j], nsa_cmp_w2[j])
            q, kc_rows, gates, kvt, wint = project(xp, row_p, t_ws, B, tm_p)
            cmp_p = compress_seq(kc_rows.reshape(B, T, -1), cmp_w)
            op = nsa_prompt_attention(q.reshape(B, T, -1), kvt, wint, cmp_p, gates.reshape(B, T, -1), rel_bias)
            row_s = [row_p[0], _w_cols(w, nq_c, kv_c), _w_cols(w, nq_c + kv_c, win_c), row_p[2]]
            qs, kvs, wns, gts, kvst = project(xs, row_s, t_ws[:1], nq, S)
            cmp_s = compress_paged(nsa_pool_t, j, page_table, cmp_w)
            os_, win_s = nsa_decode_attention(_seq_major(qs, nq, S), _seq_major(kvs, nq, S), _seq_major(wns, nq, S),
                                              _seq_major(gts, nq, S), cmp_s, nsa_win_t, j, nsa_pool_t, page_table,
                                              rel_bias)
            n_keep = min(WINDOW, T)
            res["nsa_kv_p"].append(to_tokens(kvt, (4, G, dk)))
            res["nsa_kv_s"].append(jnp.transpose(kvst.reshape(nq, 4, G, dk, S), (4, 0, 1, 2, 3)))
            res["nsa_win_p"].append(to_tokens(wint[:, :, T - n_keep:], (2, G, dk)))
            res["nsa_win_s"].append(jnp.transpose(win_s, (0, 4, 1, 2, 3)))
            w_out = nsa_w_out[j]
        elif kind == 1:
            lam_init = 0.8 - 0.6 * math.exp(-0.3 * i)
            w = diff_w_in[j]
            row_ws, t_ws = [_w_cols(w, 0, D), _w_cols(w, 2 * D, D)], [_w_rows(w, D, D)]
            q, v, kt = project(xp, row_ws, t_ws, B, tm_p)
            op = diff_prompt_attention(q.reshape(B, T, D), kt, v.reshape(B, T, D), rel_bias,
                                       diff_lambda[j], diff_norm_g[j], lam_init)
            qs, vsn, ksn, kst = project(xs, row_ws + [_w_cols(w, D, D)], t_ws, nq, S)
            vsn = _seq_major(vsn, nq, S)
            os_ = diff_decode_attention(_seq_major(qs, nq, S), _seq_major(ksn, nq, S), vsn, diff_k_t[j],
                                        cache_diff_v[j].reshape(n_pool, page * Hd, 2 * dk), page_table, rel_bias,
                                        diff_lambda[j], diff_norm_g[j], lam_init)
            res["diff_k_p"].append(to_tokens(kt, (Hd, 2, dk)))
            res["diff_k_s"].append(jnp.transpose(kst.reshape(nq, Hd, 2, dk, S), (4, 0, 1, 2, 3)))
            res["diff_v_p"].append(v.reshape(B, T, Hd, 2 * dk))
            res["diff_v_s"].append(vsn.reshape(S, nq, Hd, 2 * dk))
            w_out = diff_w_out[j]
        else:
            w = sb_w_in[j]
            row_ws, t_ws = [_w_cols(w, 0, D)], [_w_rows(w, D, 2 * D)]
            q, kvt = project(xp, row_ws, t_ws, B, tm_p)
            op = sb_prompt_attention(q.reshape(B, T, D), kvt)
            qs, kvs, kvst = project(xs, row_ws + [_w_cols(w, D, 2 * D)], t_ws, nq, S)
            os_ = sb_decode_attention(_seq_major(qs, nq, S), _seq_major(kvs, nq, S), sb_pool_t[j], page_table)
            res["sb_kv_p"].append(to_tokens(kvt, (2, H, dk)))
            res["sb_kv_s"].append(jnp.transpose(kvst.reshape(nq, 2, H, dk, S), (4, 0, 1, 2, 3)))
            w_out = sb_w_out[j]
        ln = jnp.stack([ln_g[i, 0], ln_b[i, 0], ln_g[i, 1], ln_b[i, 1]])
        w_out, w_up, w_down = w_out.astype(BF16), mlp_w_up[i].astype(BF16), mlp_w_down[i].astype(BF16)
        xp = post_mixer(op.reshape(B * T, D), w_out, xp, ln, w_up, w_down, alpha)
        os_ = jnp.transpose(os_, (1, 0, 2)).reshape(nq * S, D)
        xs = post_mixer(os_, w_out, xs, ln, w_up, w_down, alpha)
    return (xp.reshape(B, T, D), _seq_major(xs, nq, S),
            jnp.stack(res["nsa_kv_p"]), jnp.stack(res["nsa_kv_s"]), jnp.stack(res["nsa_win_p"]),
            jnp.stack(res["nsa_win_s"]), jnp.stack(res["diff_k_p"]), jnp.stack(res["diff_k_s"]),
            jnp.stack(res["diff_v_p"]), jnp.stack(res["diff_v_s"]), jnp.stack(res["sb_kv_p"]),
            jnp.stack(res["sb_kv_s"]))
```

```python
import functools
import math

import numpy as np
import jax
import jax.numpy as jnp
from jax import lax
from jax.experimental import pallas as pl
from jax.experimental.pallas import tpu as pltpu

F32 = jnp.float32
BF16 = jnp.bfloat16

HEAD_DIM = 64
NSA_KV_HEADS = 4
NSA_HPG = 4
CMP_LEN = 32
CMP_STRIDE = 16
SEL_BLOCK = 64
N_SEL = 16
SEL_FORCE = 1000.0
WINDOW = 512
REL_BUCKETS = 32
REL_MAX_DIST = 128
LN_EPS = 1e-5
NEG = -1e30
N_MIXERS = 3
SB_DEAD = 104.0

LANES = 128
VMEM_LIMIT = 48 * 1024 * 1024


def _cparams(*sem):
    return pltpu.CompilerParams(dimension_semantics=sem, vmem_limit_bytes=VMEM_LIMIT)


def _t5_bucket_np(dist):
    n = np.maximum(dist, 0)
    max_exact = REL_BUCKETS // 2
    nf = np.maximum(n, max_exact).astype(np.float32)
    large = max_exact + (np.log(nf / np.float32(max_exact)) / np.float32(math.log(REL_MAX_DIST / max_exact))
                         * np.float32(REL_BUCKETS - max_exact)).astype(np.int32)
    return np.where(n < max_exact, n, np.minimum(large, REL_BUCKETS - 1)).astype(np.int32)


FAR_DIST = int(np.min(np.nonzero(_t5_bucket_np(np.arange(4 * REL_MAX_DIST)) == REL_BUCKETS - 1)[0]))


def _bias_rows(rel_bias, dist):
    idx = jnp.asarray(_t5_bucket_np(dist))[None]
    rel = rel_bias.astype(F32)
    out = jnp.zeros((rel.shape[1],) + dist.shape, F32)
    for b in range(REL_BUCKETS):
        out = jnp.where(idx == b, rel[b].reshape((-1,) + (1,) * dist.ndim), out)
    return out


def _toeplitz_bias(rel_bias, tq, tk, n_off):
    i = np.arange(tq)[None, :, None]
    j = np.arange(tk)[None, None, :]
    off = np.arange(n_off)[:, None, None]
    return _bias_rows(rel_bias, off * tk + i - j)


def _dense_bias(rel_bias, qpos, kpos, cols):
    tab = _bias_rows(rel_bias, qpos[:, None] - kpos[None, :])[np.asarray(cols)]
    return tab.reshape(len(cols) * len(qpos), len(kpos))


def _layer_norm(z, g, b):
    mu = jnp.mean(z, axis=-1, keepdims=True)
    zc = z - mu
    var = jnp.mean(zc * zc, axis=-1, keepdims=True)
    return zc * lax.rsqrt(var + LN_EPS) * g + b


def _nt_dot(a, b):
    return lax.dot_general(a, b, (((1,), (1,)), ((), ())), preferred_element_type=F32)


def _project_kernel(*refs, n_row, n_t):
    x_ref = refs[0]
    row_w, t_w = refs[1:1 + n_row], refs[1 + n_row:1 + n_row + n_t]
    row_o, t_o = refs[1 + n_row + n_t:1 + 2 * n_row + n_t], refs[1 + 2 * n_row + n_t:]
    xb = x_ref[...].astype(BF16)
    for w_ref, o_ref in zip(row_w, row_o):
        o_ref[...] = jnp.dot(xb, w_ref[...], preferred_element_type=F32)
    for w_ref, o_ref in zip(t_w, t_o):
        o_ref[...] = _nt_dot(w_ref[...], xb)


def project(x, row_ws, t_ws, n_seq, tm):
    M, K = x.shape
    T = M // n_seq
    nt = T // tm
    n_row, n_t = len(row_ws), len(t_ws)
    const = lambda w: pl.BlockSpec(w.shape, lambda b, i: (0, 0))
    return pl.pallas_call(
        functools.partial(_project_kernel, n_row=n_row, n_t=n_t),
        grid=(n_seq, nt),
        in_specs=[pl.BlockSpec((tm, K), lambda b, i: (b * nt + i, 0))] + [const(w) for w in row_ws + t_ws],
        out_specs=[pl.BlockSpec((tm, w.shape[1]), lambda b, i: (b * nt + i, 0)) for w in row_ws]
        + [pl.BlockSpec((None, w.shape[0], tm), lambda b, i: (b, 0, i)) for w in t_ws],
        out_shape=[jax.ShapeDtypeStruct((M, w.shape[1]), F32) for w in row_ws]
        + [jax.ShapeDtypeStruct((n_seq, w.shape[0], T), F32) for w in t_ws],
        compiler_params=_cparams("parallel", "parallel"),
        name="project",
    )(x, *row_ws, *t_ws)


def _post_mixer_kernel(o_ref, wo_ref, x_ref, ln_ref, wu_ref, wd_ref, y_ref, x1_scr, xb_scr, acc_scr, *, alpha):
    f = pl.program_id(1)

    @pl.when(f == 0)
    def _():
        y = jnp.dot(o_ref[...].astype(BF16), wo_ref[...], preferred_element_type=F32)
        x1 = _layer_norm(alpha * x_ref[...] + y, ln_ref[0:1, :], ln_ref[1:2, :])
        x1_scr[...] = x1
        xb_scr[...] = x1.astype(BF16)
        acc_scr[...] = jnp.zeros_like(acc_scr)

    h = jnp.dot(xb_scr[...], wu_ref[...], preferred_element_type=F32)
    h = jnp.square(jnp.maximum(h, 0.0))
    acc_scr[...] += jnp.dot(h.astype(BF16), wd_ref[...], preferred_element_type=F32)

    @pl.when(f == pl.num_programs(1) - 1)
    def _():
        y_ref[...] = _layer_norm(alpha * x1_scr[...] + acc_scr[...], ln_ref[2:3, :], ln_ref[3:4, :])


def post_mixer(o, w_out, x, ln, w_up, w_down, alpha, tm=512, tf=1024):
    M, D = x.shape
    Fd = w_up.shape[1]
    return pl.pallas_call(
        functools.partial(_post_mixer_kernel, alpha=alpha),
        grid=(M // tm, Fd // tf),
        in_specs=[
            pl.BlockSpec((tm, D), lambda i, f: (i, 0)),
            pl.BlockSpec((D, D), lambda i, f: (0, 0)),
            pl.BlockSpec((tm, D), lambda i, f: (i, 0)),
            pl.BlockSpec((4, D), lambda i, f: (0, 0)),
            pl.BlockSpec((D, tf), lambda i, f: (0, f)),
            pl.BlockSpec((tf, D), lambda i, f: (f, 0)),
        ],
        out_specs=pl.BlockSpec((tm, D), lambda i, f: (i, 0)),
        out_shape=jax.ShapeDtypeStruct((M, D), F32),
        scratch_shapes=[pltpu.VMEM((tm, D), F32), pltpu.VMEM((tm, D), BF16), pltpu.VMEM((tm, D), F32)],
        compiler_params=_cparams("parallel", "arbitrary"),
        name="post_mixer",
    )(o, w_out, x, ln, w_up, w_down)


LOG2E = math.log2(math.e)


def _flash_step(carry, s, pv):
    m, l, acc = carry
    m_new = jnp.maximum(m, jnp.max(s, axis=-1, keepdims=True))
    alpha = jnp.exp2(m - m_new)
    p = jnp.exp2(s - m_new)
    if l is not None:
        l = alpha * l + jnp.sum(p, axis=-1, keepdims=True)
    acc = alpha * acc + pv(p.astype(BF16))
    return m_new, l, acc


def _flash_init(rows, width, with_l=True):
    return jnp.full((rows, 1), NEG, F32), jnp.zeros((rows, 1), F32) if with_l else None, jnp.zeros((rows, width), F32)


def _split_heads(q, tq):
    lane = lax.broadcasted_iota(jnp.int32, (tq, LANES), 1)
    lo = jnp.where(lane < HEAD_DIM, q, 0.0)
    hi = jnp.where(lane >= HEAD_DIM, q, 0.0)
    return jnp.concatenate([lo, hi], axis=0).astype(BF16)


def _cols(ref, start, width):
    return ref[:, pl.ds(pl.multiple_of(start, width), width)].astype(BF16)


def _rows(ref, start, height):
    return ref[pl.ds(pl.multiple_of(start, height), height), :].astype(BF16)


def _rows_blockdiag(q, n_blk, blk_w):
    rep = jnp.concatenate([q] * n_blk, axis=0)
    r = lax.broadcasted_iota(jnp.int32, rep.shape, 0) // q.shape[0]
    c = lax.broadcasted_iota(jnp.int32, rep.shape, 1) // blk_w
    return jnp.where(r == c, rep, 0.0).astype(BF16)


def _pad_rows(a, rows):
    return jnp.concatenate([a, jnp.zeros((rows - a.shape[0], a.shape[1]), a.dtype)], axis=0)


def _page_specs(n_pages, block, index):
    return [pl.BlockSpec(block, functools.partial(lambda b, pt, p: index(pt[b, p]), p=p)) for p in range(n_pages)]


def _diff_lambda(lam_ref, lam_init):
    lf = lam_ref[...]
    a = jnp.sum(lf[0:1, :] * lf[1:2, :], axis=-1, keepdims=True)
    b = jnp.sum(lf[2:3, :] * lf[3:4, :], axis=-1, keepdims=True)
    return jnp.exp(a) - jnp.exp(b) + lam_init


def _diff_finish(o1, o2, lam_full, g, lam_init):
    o = o1 - lam_full * o2
    o = o * lax.rsqrt(jnp.mean(o * o, axis=-1, keepdims=True) + LN_EPS) * g
    return o * (1.0 - lam_init)


def _causal_bias(tq, tk):
    ri = lax.broadcasted_iota(jnp.int32, (tq, tk), 0)
    ci = lax.broadcasted_iota(jnp.int32, (tq, tk), 1)
    return jnp.where(ci <= ri, 0.0, NEG)


def _add_per_head(s, b, n_blocks):
    tq, tk = b.shape
    return (s.reshape(n_blocks, tq, tk) + b[None]).reshape(n_blocks * tq, tk)


def _diff_prompt_kernel(q_ref, kt_ref, v_ref, toe_ref, lam_ref, g_ref, o_ref, *, tq, lam_init):
    qt = pl.program_id(2)
    tk = tq
    rows = 2 * tq
    q2 = _split_heads(q_ref[...] * (HEAD_DIM ** -0.5 * LOG2E), tq)

    def qk(kt):
        return jnp.dot(q2, _cols(kt_ref, kt * tk, tk), preferred_element_type=F32)

    def step(kt, carry, bias):
        s = qk(kt)
        if bias is not None:
            s = s + bias
        v = _rows(v_ref, kt * tk, tk)
        return _flash_step(carry, s, lambda p: jnp.dot(p, v, preferred_element_type=F32))

    near0 = jnp.maximum(qt - 1, 0)
    carry = lax.fori_loop(0, near0, lambda kt, c: step(kt, c, None), _flash_init(rows, LANES))
    carry = lax.fori_loop(near0, qt, lambda kt, c: step(kt, c, toe_ref[1]), carry)
    _, l, acc = step(qt, carry, _add_per_head(toe_ref[0], _causal_bias(tq, tk), 2))
    o = acc / l
    o_ref[...] = _diff_finish(o[:tq], o[tq:], _diff_lambda(lam_ref, lam_init), g_ref[...], lam_init)


def _toeplitz_rel(rel_bias, tq, tk, n_off):
    far = rel_bias[REL_BUCKETS - 1].astype(F32)[:, None, None, None]
    return (_toeplitz_bias(rel_bias, tq, tk, n_off) - far) * LOG2E


def diff_prompt_attention(q, kt, v, rel_bias, lam, norm_g, lam_init, tq=512):
    B, T, W = q.shape
    n_heads = W // LANES
    toe = _toeplitz_rel(rel_bias, tq, tq, 2)
    toe = jnp.concatenate([toe[:n_heads], toe[n_heads:]], axis=2)
    return pl.pallas_call(
        functools.partial(_diff_prompt_kernel, tq=tq, lam_init=lam_init),
        grid=(B, n_heads, T // tq),
        in_specs=[
            pl.BlockSpec((None, tq, LANES), lambda b, h, i: (b, i, h)),
            pl.BlockSpec((None, LANES, T), lambda b, h, i: (b, h, 0)),
            pl.BlockSpec((None, T, LANES), lambda b, h, i: (b, 0, h)),
            pl.BlockSpec((None, 2, 2 * tq, tq), lambda b, h, i: (h, 0, 0, 0)),
            pl.BlockSpec((4, HEAD_DIM), lambda b, h, i: (0, 0)),
            pl.BlockSpec((1, LANES), lambda b, h, i: (0, 0)),
        ],
        out_specs=pl.BlockSpec((None, tq, LANES), lambda b, h, i: (b, i, h)),
        out_shape=jax.ShapeDtypeStruct((B, T, W), F32),
        compiler_params=_cparams("parallel", "parallel", "arbitrary"),
        name="diff_prompt",
    )(q, kt, v, toe, lam, norm_g.reshape(1, LANES))


def _diff_decode_kernel(pt_ref, q_ref, kn_ref, vn_ref, bias_ref, lam_ref, g_ref, *rest, n_pages, page, lam_init):
    kp, vp = rest[:n_pages], rest[n_pages:2 * n_pages]
    o_ref, s_scr = rest[2 * n_pages], rest[2 * n_pages + 1]
    nq, W = q_ref.shape
    n_heads = W // LANES
    hr = 2 * nq
    rows = n_heads * hr
    qbd = _rows_blockdiag(q_ref[...] * HEAD_DIM ** -0.5, 2 * n_heads, HEAD_DIM)

    hblk = lax.broadcasted_iota(jnp.int32, (rows, page), 0) // hr

    def pv(p, v_of_head):
        p_bd = jnp.concatenate([jnp.where(hblk == h, p, jnp.zeros_like(p)) for h in range(n_heads)], axis=1)
        v_hk = jnp.concatenate([v_of_head(h) for h in range(n_heads)], axis=0)
        return jnp.dot(p_bd, v_hk, preferred_element_type=F32)

    m = jnp.full((rows, 1), NEG, F32)
    for p in range(n_pages):
        kt = kp[p][...].reshape(W, page).astype(BF16)
        s = jnp.dot(qbd, kt, preferred_element_type=F32) + bias_ref[:, p * page:(p + 1) * page]
        s_scr[:, p * page:(p + 1) * page] = s
        m = jnp.maximum(m, jnp.max(s, axis=-1, keepdims=True))
    ri = lax.broadcasted_iota(jnp.int32, (rows, page), 0) % nq
    ci = lax.broadcasted_iota(jnp.int32, (rows, page), 1)
    mask = ci <= ri
    kn = _pad_rows(kn_ref[...], page).astype(BF16)
    vn = _pad_rows(vn_ref[...], page).astype(BF16)
    sn = jnp.where(mask, _nt_dot(qbd, kn) + bias_ref[:, n_pages * page:], NEG)
    m = jnp.maximum(m, jnp.max(sn, axis=-1, keepdims=True))
    pn = jnp.where(mask, jnp.exp(sn - m), 0.0)
    l = jnp.sum(pn, axis=-1, keepdims=True)
    acc = pv(pn.astype(BF16), lambda h: vn[:, h * LANES:(h + 1) * LANES])
    for p in range(n_pages):
        pr = jnp.exp(s_scr[:, p * page:(p + 1) * page] - m)
        l = l + jnp.sum(pr, axis=-1, keepdims=True)
        acc = acc + pv(pr.astype(BF16), lambda h: vp[p][pl.ds(h, page, stride=n_heads), :].astype(BF16))
    o = acc / l
    lam_full = _diff_lambda(lam_ref, lam_init)
    outs = [_diff_finish(o[h * hr:h * hr + nq], o[h * hr + nq:(h + 1) * hr], lam_full, g_ref[...], lam_init)
            for h in range(n_heads)]
    o_ref[...] = jnp.concatenate(outs, axis=1)


def diff_decode_attention(q, kn, vn, kt_pool, v_pool, page_table, rel_bias, lam, norm_g, lam_init):
    S, nq, W = q.shape
    n_pages = page_table.shape[1]
    page = kt_pool.shape[-1]
    n_heads = W // LANES
    P = n_pages * page
    cols = [mp * n_heads + h for h in range(n_heads) for mp in range(2)]
    bias = _dense_bias(rel_bias, P + np.arange(nq), np.arange(P + page), cols)
    rows = bias.shape[0]
    seq = pl.BlockSpec((None, nq, W), lambda b, pt: (b, 0, 0))
    grid_spec = pltpu.PrefetchScalarGridSpec(
        num_scalar_prefetch=1,
        grid=(S,),
        in_specs=[seq, seq, seq,
                  pl.BlockSpec(bias.shape, lambda b, pt: (0, 0)),
                  pl.BlockSpec((4, HEAD_DIM), lambda b, pt: (0, 0)),
                  pl.BlockSpec((1, LANES), lambda b, pt: (0, 0))]
        + _page_specs(n_pages, (None,) + kt_pool.shape[1:], lambda pg: (pg, 0, 0, 0, 0))
        + _page_specs(n_pages, (None,) + v_pool.shape[1:], lambda pg: (pg, 0, 0)),
        out_specs=seq,
        scratch_shapes=[pltpu.VMEM((rows, P), F32)],
    )
    return pl.pallas_call(
        functools.partial(_diff_decode_kernel, n_pages=n_pages, page=page, lam_init=lam_init),
        grid_spec=grid_spec,
        out_shape=jax.ShapeDtypeStruct((S, nq, W), F32),
        compiler_params=_cparams("arbitrary"),
        name="diff_decode",
    )(page_table, q, kn, vn, bias, lam, norm_g.reshape(1, LANES), *([kt_pool] * n_pages), *([v_pool] * n_pages))


def _softplus(z):
    return jnp.maximum(z, 0.0) + jnp.log(1.0 + jnp.exp(-jnp.abs(z)))


def _suffix_sum(c, u):
    hi = c.astype(BF16)
    lo = (c - hi.astype(F32)).astype(BF16)
    return jnp.dot(hi, u, preferred_element_type=F32) + jnp.dot(lo, u, preferred_element_type=F32)


def _sb_local(z, u, mask=None):
    c = _softplus(z)
    if mask is not None:
        c = jnp.where(mask, c, 0.0)
    return z - c - _suffix_sum(c, u), jnp.sum(c, axis=-1, keepdims=True)


def _sb_weights(e, r, mask=None):
    a = jnp.exp(e - r)
    if mask is not None:
        a = jnp.where(mask, a, 0.0)
    return a.astype(BF16)


def _strict_upper(n):
    return jnp.asarray(np.arange(n)[:, None] > np.arange(n)[None, :], BF16)


def _sb_prompt_kernel(q_ref, kt_ref, vt_ref, u_ref, o_ref, *, tq, tk):
    qt = pl.program_id(2)
    n_sub = tq // tk
    rows = 2 * tq
    q2 = _split_heads(q_ref[...] * HEAD_DIM ** -0.5, tq)
    u = u_ref[...]

    def block(kt, carry, diag):
        r, acc = carry
        local = []
        for sub in range(n_sub):
            mask = None
            if diag:
                ri = lax.broadcasted_iota(jnp.int32, (rows, tk), 0) % tq
                ci = lax.broadcasted_iota(jnp.int32, (rows, tk), 1) + sub * tk
                mask = ci < ri
            z = jnp.dot(q2, _cols(kt_ref, kt * tq + sub * tk, tk), preferred_element_type=F32)
            local.append(_sb_local(z, u, mask) + (mask,))
        for sub in reversed(range(n_sub)):
            e, tot, mask = local[sub]
            acc = acc + _nt_dot(_sb_weights(e, r, mask), _cols(vt_ref, kt * tq + sub * tk, tk))
            r = r + tot
        return r, acc

    r, acc = block(qt, (jnp.zeros((rows, 1), F32), jnp.zeros((rows, LANES), F32)), True)

    def cond(c):
        return jnp.logical_and(c[0] >= 0, jnp.min(c[1]) < SB_DEAD)

    def body(c):
        r, acc = block(c[0], (c[1], c[2]), False)
        return c[0] - 1, r, acc

    _, _, acc = lax.while_loop(cond, body, (qt - 1, r, acc))
    lane = lax.broadcasted_iota(jnp.int32, (tq, LANES), 1)
    o_ref[...] = jnp.where(lane < HEAD_DIM, acc[:tq], acc[tq:])


def sb_prompt_attention(q, kvt, tq=512, tk=256):
    B, T, W = q.shape
    n_pairs = W // LANES
    return pl.pallas_call(
        functools.partial(_sb_prompt_kernel, tq=tq, tk=tk),
        grid=(B, n_pairs, T // tq),
        in_specs=[
            pl.BlockSpec((None, tq, LANES), lambda b, h, i: (b, i, h)),
            pl.BlockSpec((None, LANES, T), lambda b, h, i: (b, h, 0)),
            pl.BlockSpec((None, LANES, T), lambda b, h, i: (b, n_pairs + h, 0)),
            pl.BlockSpec((tk, tk), lambda b, h, i: (0, 0)),
        ],
        out_specs=pl.BlockSpec((None, tq, LANES), lambda b, h, i: (b, i, h)),
        out_shape=jax.ShapeDtypeStruct((B, T, W), F32),
        compiler_params=_cparams("parallel", "parallel", "arbitrary"),
        name="sb_prompt",
    )(q, kvt, kvt, _strict_upper(tk))


def _sb_decode_kernel(pt_ref, q_ref, kvn_ref, u_ref, *rest, n_pages, page):
    pages, o_ref, e_scr = rest[:n_pages], rest[n_pages], rest[n_pages + 1]
    nq, W = q_ref.shape
    n_heads = W // HEAD_DIM
    rows = n_heads * nq
    u = u_ref[...]
    qbd = _rows_blockdiag(q_ref[...] * HEAD_DIM ** -0.5, n_heads, HEAD_DIM)
    kvn = _pad_rows(kvn_ref[...], page).astype(BF16)
    ri = lax.broadcasted_iota(jnp.int32, (rows, page), 0) % nq
    ci = lax.broadcasted_iota(jnp.int32, (rows, page), 1)
    new_mask = ci < ri
    e_new, r = _sb_local(_nt_dot(qbd, kvn[:, :W]), u, new_mask)
    tots = []
    for p in range(n_pages):
        kt = pages[p][0].reshape(W, page).astype(BF16)
        e, tot = _sb_local(jnp.dot(qbd, kt, preferred_element_type=F32), u)
        e_scr[:, p * page:(p + 1) * page] = e
        tots.append(tot)
    acc = jnp.dot(_sb_weights(e_new, 0.0, new_mask), kvn[:, W:], preferred_element_type=F32)
    for p in reversed(range(n_pages)):
        vt = pages[p][1].reshape(W, page).astype(BF16)
        acc = acc + _nt_dot(_sb_weights(e_scr[:, p * page:(p + 1) * page], r), vt)
        r = r + tots[p]
    col = lax.broadcasted_iota(jnp.int32, (nq, W), 1) // HEAD_DIM
    o = jnp.zeros((nq, W), F32)
    for h in range(n_heads):
        o = o + jnp.where(col == h, acc[h * nq:(h + 1) * nq, :], 0.0)
    o_ref[...] = o


def sb_decode_attention(q, kvn, kvt_pool, page_table):
    S, nq, W = q.shape
    n_pages = page_table.shape[1]
    page = kvt_pool.shape[-1]
    grid_spec = pltpu.PrefetchScalarGridSpec(
        num_scalar_prefetch=1,
        grid=(S,),
        in_specs=[pl.BlockSpec((None, nq, W), lambda b, pt: (b, 0, 0)),
                  pl.BlockSpec((None, nq, 2 * W), lambda b, pt: (b, 0, 0)),
                  pl.BlockSpec((page, page), lambda b, pt: (0, 0))]
        + _page_specs(n_pages, (None,) + kvt_pool.shape[1:], lambda pg: (pg, 0, 0, 0, 0)),
        out_specs=pl.BlockSpec((None, nq, W), lambda b, pt: (b, 0, 0)),
        scratch_shapes=[pltpu.VMEM((W // HEAD_DIM * nq, n_pages * page), F32)],
    )
    return pl.pallas_call(
        functools.partial(_sb_decode_kernel, n_pages=n_pages, page=page),
        grid_spec=grid_spec,
        out_shape=jax.ShapeDtypeStruct((S, nq, W), F32),
        compiler_params=_cparams("arbitrary"),
        name="sb_decode",
    )(page_table, q, kvn, _strict_upper(page), *([kvt_pool] * n_pages))


N_CMP_TILES = 2 * NSA_KV_HEADS * HEAD_DIM // LANES


def _gelu_tanh(x):
    return 0.5 * x * (1.0 + jnp.tanh(math.sqrt(2.0 / math.pi) * (x + 0.044715 * x * x * x)))


def _compress_body(chunk_rows, w1_ref, pos_ref, w2_ref, o_ref, n_chunks):
    n_kinds = w1_ref.shape[0]
    tiles_per_kind = NSA_KV_HEADS * HEAD_DIM // LANES
    for kind in range(n_kinds):
        xcat = jnp.concatenate(
            [jnp.concatenate([chunk_rows(l, kind * tiles_per_kind + t) for l in range(CMP_STRIDE)], axis=1)
             for t in range(tiles_per_kind)], axis=0)
        u = []
        for a in range(2):
            xa = (xcat + pos_ref[kind, a]).astype(BF16)
            u.append(jnp.dot(xa, w1_ref[kind, a], preferred_element_type=F32))
        for t in range(tiles_per_kind):
            u0 = u[0][t * n_chunks:(t + 1) * n_chunks]
            u1 = pltpu.roll(u[1][t * n_chunks:(t + 1) * n_chunks], n_chunks - 1, 0)
            hid = _gelu_tanh(u0 + u1)
            out = jnp.dot(hid.astype(BF16), w2_ref[kind], preferred_element_type=F32)
            col = (kind * tiles_per_kind + t) * LANES
            o_ref[:, col:col + LANES] = out


def _compress_seq_kernel(*refs, n_chunks):
    x_refs = refs[:N_CMP_TILES]
    w1_ref, pos_ref, w2_ref, o_ref = refs[N_CMP_TILES:]

    def chunk_rows(l, tile):
        return x_refs[tile][pl.ds(l, n_chunks, stride=CMP_STRIDE), :]
    _compress_body(chunk_rows, w1_ref, pos_ref, w2_ref, o_ref, n_chunks)


def _compress_paged_kernel(pt_ref, w1_ref, pos_ref, w2_ref, *rest, n_pages, page):
    pages, o_ref, x_scr = rest[:n_pages], rest[n_pages], rest[n_pages + 1]
    tiles_per_kind = NSA_KV_HEADS * HEAD_DIM // LANES
    for p in range(n_pages):
        for t in range(N_CMP_TILES):
            kind, pair = t // tiles_per_kind, t % tiles_per_kind
            xt = pages[p][kind, 2 * pair:2 * pair + 2].reshape(LANES, page)
            x_scr[t, p * page:(p + 1) * page, :] = xt.T
    n_chunks = n_pages * page // CMP_STRIDE

    def chunk_rows(l, tile):
        return x_scr[tile, pl.ds(l, n_chunks, stride=CMP_STRIDE), :]
    _compress_body(chunk_rows, w1_ref, pos_ref, w2_ref, o_ref, n_chunks)


def _compress_weights(cmp_pos, cmp_w1, cmp_w2):
    eye2 = jnp.eye(2, dtype=F32)
    w1r = cmp_w1.reshape(2, 2, CMP_STRIDE, HEAD_DIM, -1)
    w1e = jnp.einsum('kaldj,hg->kalhdgj', w1r, eye2)
    w1e = w1e.reshape(2, 2, CMP_STRIDE * LANES, 2 * cmp_w1.shape[-1]).astype(BF16)
    pos = jnp.broadcast_to(cmp_pos.reshape(2, 2, CMP_STRIDE, 1, HEAD_DIM), (2, 2, CMP_STRIDE, 2, HEAD_DIM))
    pos = pos.reshape(2, 2, 1, CMP_STRIDE * LANES).astype(F32)
    w2e = jnp.einsum('kjd,hg->khjgd', cmp_w2, eye2).reshape(2, 2 * cmp_w2.shape[1], LANES).astype(BF16)
    return w1e, pos, w2e


def compress_seq(kv, cmp_w):
    B, L, _ = kv.shape
    w1e, pos, w2e = cmp_w
    n_chunks = L // CMP_STRIDE
    W = N_CMP_TILES * LANES
    return pl.pallas_call(
        functools.partial(_compress_seq_kernel, n_chunks=n_chunks),
        grid=(B,),
        in_specs=[pl.BlockSpec((None, L, LANES), functools.partial(lambda b, t: (b, 0, t), t=t)) for t in range(N_CMP_TILES)]
        + [pl.BlockSpec(w1e.shape, lambda b: (0, 0, 0, 0)),
           pl.BlockSpec(pos.shape, lambda b: (0, 0, 0, 0)),
           pl.BlockSpec(w2e.shape, lambda b: (0, 0, 0))],
        out_specs=pl.BlockSpec((None, n_chunks, W), lambda b: (b, 0, 0)),
        out_shape=jax.ShapeDtypeStruct((B, n_chunks, W), F32),
        compiler_params=_cparams("parallel"),
        name="nsa_compress_seq",
    )(*([kv] * N_CMP_TILES), w1e, pos, w2e)


def compress_paged(pool_t, layer, page_table, cmp_w):
    S, n_pages = page_table.shape
    page = pool_t.shape[-1]
    w1e, pos, w2e = cmp_w
    n_chunks = n_pages * page // CMP_STRIDE
    W = N_CMP_TILES * LANES
    blk = (None, None, 2) + pool_t.shape[3:]
    grid_spec = pltpu.PrefetchScalarGridSpec(
        num_scalar_prefetch=1,
        grid=(S,),
        in_specs=[pl.BlockSpec(w1e.shape, lambda b, pt: (0, 0, 0, 0)),
                  pl.BlockSpec(pos.shape, lambda b, pt: (0, 0, 0, 0)),
                  pl.BlockSpec(w2e.shape, lambda b, pt: (0, 0, 0))]
        + _page_specs(n_pages, blk, lambda pg: (layer, pg, 0, 0, 0, 0)),
        out_specs=pl.BlockSpec((None, n_chunks, W), lambda b, pt: (b, 0, 0)),
        scratch_shapes=[pltpu.VMEM((N_CMP_TILES, n_pages * page, LANES), F32)],
    )
    return pl.pallas_call(
        functools.partial(_compress_paged_kernel, n_pages=n_pages, page=page),
        grid_spec=grid_spec,
        out_shape=jax.ShapeDtypeStruct((S, n_chunks, W), F32),
        compiler_params=_cparams("arbitrary"),
        name="nsa_compress_paged",
    )(page_table, w1e, pos, w2e, *([pool_t] * n_pages))


def _masked_softmax(s, mask, exp=jnp.exp):
    s = jnp.where(mask, s, NEG)
    m = jnp.max(s, axis=-1, keepdims=True)
    e = jnp.where(mask, exp(s - m), 0.0)
    return e * (1.0 / jnp.maximum(jnp.sum(e, axis=-1, keepdims=True), 1e-30))


def _importance_t(ov_t, psum):
    hi = psum.astype(BF16)
    lo = (psum - hi.astype(F32)).astype(BF16)
    return _nt_dot(ov_t, hi) + _nt_dot(ov_t, lo)


def _select_topk_t(imp, n_rows):
    jj = lax.broadcasted_iota(jnp.int32, imp.shape, 0)
    cnt = jnp.zeros(imp.shape, F32)
    for i in range(n_rows):
        row = imp[i:i + 1, :]
        ahead = (row > imp) | ((row == imp) & (jj > i))
        cnt = cnt + jnp.where(ahead, 1.0, 0.0)
    return jnp.where(cnt < N_SEL, 1.0, 0.0)


def _select_topk_ref(imp_ref, n_active):
    imp = imp_ref[...]
    jj = lax.broadcasted_iota(jnp.int32, imp.shape, 0)

    def body(i, cnt):
        row = imp_ref[pl.ds(i, 1), :]
        ahead = (row > imp) | ((row == imp) & (jj > i))
        return cnt + jnp.where(ahead, 1.0, 0.0)

    cnt = lax.fori_loop(0, n_active, body, jnp.zeros(imp.shape, F32))
    return jnp.where(cnt < N_SEL, 1.0, 0.0)


def _block_expand(first_block, n_blocks, tk):
    j = lax.broadcasted_iota(jnp.int32, (n_blocks, tk), 0)
    col = lax.broadcasted_iota(jnp.int32, (n_blocks, tk), 1)
    return jnp.where(j == first_block + col // SEL_BLOCK, 1.0, 0.0).astype(BF16)


CMP_NEAR_BACK = -(-(CMP_LEN - 1 + FAR_DIST) // CMP_STRIDE)


def _nsa_prompt_kernel(q_ref, kst_ref, vst_ref, kwt_ref, vwt_ref, kc_ref, vc_ref, gate_ref, pat_ref, toe_ref,
                       ovt_ref, o_ref, imp_scr, *, tq, n_cmp):
    g = pl.program_id(1)
    qt = pl.program_id(2)
    tk = tq
    R = NSA_HPG
    par = g % 2
    lane = lax.broadcasted_iota(jnp.int32, (tq, LANES), 1)
    own = (lane // HEAD_DIM) == par
    q = q_ref[...] * (HEAD_DIM ** -0.5 * LOG2E)
    parts = []
    for r in range(R):
        t = q[:, (r // 2) * LANES:(r // 2 + 1) * LANES]
        src = jnp.where(par == r % 2, t, pltpu.roll(t, HEAD_DIM, 1))
        parts.append(jnp.where(own, src, 0.0))
    q4 = jnp.concatenate(parts, axis=0).astype(BF16)
    rows = R * tq

    def toe(off):
        return toe_ref[:, off].reshape(rows, tk)

    n_c = kc_ref.shape[0]
    cc = lax.broadcasted_iota(jnp.int32, (rows, n_c), 1)
    tt = qt * tq + lax.broadcasted_iota(jnp.int32, (rows, n_c), 0) % tq
    mask_c = (cc * CMP_STRIDE + CMP_LEN - 1 <= tt) & (cc < n_cmp)
    pat = pat_ref[...].reshape(rows, LANES)
    uu = lax.broadcasted_iota(jnp.int32, (LANES, n_c), 0)
    shift = jnp.where(lax.broadcasted_iota(jnp.int32, (LANES, n_c), 1) == qt * (tq // CMP_STRIDE) - CMP_NEAR_BACK + uu,
                      1.0, 0.0).astype(BF16)
    s_c = _nt_dot(q4, kc_ref[...].astype(BF16)) + jnp.dot(pat.astype(BF16), shift, preferred_element_type=F32)
    p_c = _masked_softmax(s_c, mask_c, jnp.exp2)
    o_c = jnp.dot(p_c.astype(BF16), vc_ref[...].astype(BF16), preferred_element_type=F32)
    psum = p_c[0:tq]
    for r in range(1, R):
        psum = psum + p_c[r * tq:(r + 1) * tq]
    n_blk = ovt_ref.shape[0]
    imp = _importance_t(ovt_ref[...], psum)
    jj = lax.broadcasted_iota(jnp.int32, (n_blk, tq), 0)
    tq_pos = qt * tq + lax.broadcasted_iota(jnp.int32, (n_blk, tq), 1)
    t_blk = tq_pos // SEL_BLOCK
    forced = (jj == 0) | (jj == t_blk) | (jj == t_blk - 1)
    imp_scr[...] = jnp.where(jj * SEL_BLOCK <= tq_pos, imp + jnp.where(forced, SEL_FORCE, 0.0), -1.0)
    sel_t = _select_topk_ref(imp_scr, jnp.minimum((qt + 1) * (tq // SEL_BLOCK), n_blk))
    sel = _pad_rows(sel_t, LANES).T
    sel_neg = ((sel - 1.0) * -NEG).astype(BF16)

    own_rows = (lax.broadcasted_iota(jnp.int32, (LANES, 1), 0) // HEAD_DIM) == par

    def branch(lhs, rhs, v_ref, first, tile_bias):
        def qk(kt):
            return jnp.dot(lhs, rhs(kt), preferred_element_type=F32)

        def finish_tile(kt, carry, s, toe_off, extra):
            b = tile_bias(kt)
            if extra is not None:
                b = extra if b is None else b + extra
            if toe_off is not None:
                s = s + toe(toe_off)
            if b is not None:
                s = _add_per_head(s, b, R)
            vt = jnp.where(own_rows, _cols(v_ref, kt * tk, tk), 1.0)
            return _flash_step(carry, s, lambda p: _nt_dot(p, vt))

        def step(kt, carry, toe_off):
            return finish_tile(kt, carry, qk(kt), toe_off, None)

        carry = _flash_init(rows, LANES, with_l=False)
        carry = lax.fori_loop(first, near0, lambda kt, c: step(kt, c, None), carry)
        carry = lax.fori_loop(jnp.maximum(near0, first), qt, lambda kt, c: step(kt, c, 1), carry)
        _, _, acc = finish_tile(qt, carry, qk(qt), 0, _causal_bias(tq, tk))
        return acc / pltpu.roll(acc, HEAD_DIM, 1)

    near0 = jnp.maximum(qt - 1, 0)

    sel_lhs = jnp.concatenate([q4, jnp.concatenate([sel_neg] * R, axis=0)], axis=1)

    def sel_rhs(kt):
        return jnp.concatenate([_cols(kst_ref, kt * tk, tk), _block_expand(kt * (tk // SEL_BLOCK), LANES, tk)], axis=0)

    o_s = branch(sel_lhs, sel_rhs, vst_ref, 0, lambda kt: None)

    n_back = WINDOW // tk
    ri = lax.broadcasted_iota(jnp.int32, (tq, tk), 0)
    ci = lax.broadcasted_iota(jnp.int32, (tq, tk), 1)

    def win_bias(kt):
        return jnp.where((ci > ri) | (kt != qt - n_back), 0.0, NEG)

    o_w = branch(q4, lambda kt: _cols(kwt_ref, kt * tk, tk), vwt_ref, jnp.maximum(qt - n_back, 0), win_bias)

    gs = jax.nn.sigmoid(gate_ref[...])
    n_h = NSA_KV_HEADS * R
    outs = []
    for r in range(R):
        o_r = jnp.zeros((tq, LANES), F32)
        for br, o_b in enumerate((o_c, o_s, o_w)):
            gcol = jnp.sum(jnp.where(lane == br * n_h + g * R + r, gs, 0.0), axis=-1, keepdims=True)
            o_r = o_r + gcol * o_b[r * tq:(r + 1) * tq]
        outs.append(jnp.where(par == r % 2, o_r, pltpu.roll(o_r, HEAD_DIM, 1)))
    for u in range(R // 2):
        o_ref[:, u * LANES:(u + 1) * LANES] = jnp.where(lane < HEAD_DIM, outs[2 * u], outs[2 * u + 1])


def _overlap_t(n_blk_rows, n_cmp_cols, n_cmp, n_slc):
    c0 = np.arange(n_cmp_cols)[None, :] * CMP_STRIDE
    j0 = np.arange(n_blk_rows)[:, None] * SEL_BLOCK
    ov = (c0 < j0 + SEL_BLOCK) & (c0 + CMP_LEN > j0)
    ov &= (np.arange(n_cmp_cols)[None, :] < n_cmp) & (np.arange(n_blk_rows)[:, None] < n_slc)
    return jnp.asarray(ov, BF16)


def nsa_prompt_attention(q, kvt, wint, cmp, gates, rel_bias, tq=256):
    B, T, W = q.shape
    G, R = NSA_KV_HEADS, NSA_HPG
    n_cmp = (T - CMP_LEN) // CMP_STRIDE + 1
    n_c = cmp.shape[1]
    n_slc = -(-T // SEL_BLOCK)
    n_blk = -(-n_slc // 8) * 8
    assert WINDOW % tq == 0 and n_blk <= LANES
    toe = _toeplitz_rel(rel_bias, tq, tq, 2)
    c31 = rel_bias[REL_BUCKETS - 1].astype(F32)
    n_near = CMP_NEAR_BACK + (tq - CMP_LEN) // CMP_STRIDE + 1
    assert n_near <= LANES and tq % CMP_STRIDE == 0
    dist = np.arange(tq)[:, None] - (CMP_LEN - 1) - CMP_STRIDE * (np.arange(LANES)[None, :] - CMP_NEAR_BACK)
    pat = (_bias_rows(rel_bias, dist) - c31[:, None, None]) * jnp.asarray((np.arange(LANES) < n_near) * LOG2E, F32)
    ovt = _overlap_t(n_blk, n_c, n_cmp, n_slc)
    pair_rows = lambda base: pl.BlockSpec((None, LANES, T), lambda b, g, i: (b, base + g // 2, 0))
    pair_cols = lambda base: pl.BlockSpec((None, n_c, LANES), lambda b, g, i: (b, 0, base + g // 2))
    return pl.pallas_call(
        functools.partial(_nsa_prompt_kernel, tq=tq, n_cmp=n_cmp),
        grid=(B, G, T // tq),
        in_specs=[
            pl.BlockSpec((None, tq, R * HEAD_DIM), lambda b, g, i: (b, i, g)),
            pair_rows(4), pair_rows(6), pair_rows(0), pair_rows(2),
            pair_cols(0), pair_cols(2),
            pl.BlockSpec((None, tq, LANES), lambda b, g, i: (b, i, 0)),
            pl.BlockSpec((R, tq, LANES), lambda b, g, i: (g, 0, 0)),
            pl.BlockSpec((R, 2, tq, tq), lambda b, g, i: (g, 0, 0, 0)),
            pl.BlockSpec(ovt.shape, lambda b, g, i: (0, 0)),
        ],
        out_specs=pl.BlockSpec((None, tq, R * HEAD_DIM), lambda b, g, i: (b, i, g)),
        out_shape=jax.ShapeDtypeStruct((B, T, W), F32),
        scratch_shapes=[pltpu.VMEM((n_blk, tq), F32)],
        compiler_params=_cparams("parallel", "parallel", "arbitrary"),
        name="nsa_prompt",
    )(q, kvt, kvt, wint, wint, cmp, cmp, gates, pat, toe, ovt)


def _nsa_decode_kernel(pt_ref, q_ref, kvn_ref, wn_ref, gate_ref, cmp_ref, win_ref, bc_ref, bs_ref, bw_ref, ovt_ref,
                       *rest, n_pages, page, n_cmp):
    pages = rest[:n_pages]
    o_ref, wout_ref, s_scr = rest[n_pages], rest[n_pages + 1], rest[n_pages + 2]
    G, R = NSA_KV_HEADS, NSA_HPG
    nq = q_ref.shape[0]
    n_h = G * R
    rows = n_h * nq
    GW = G * HEAD_DIM
    P = n_pages * page
    n_win = win_ref.shape[-1]
    lane8 = lax.broadcasted_iota(jnp.int32, (nq, LANES), 1)

    q = q_ref[...] * HEAD_DIM ** -0.5
    blocks = []
    for h in range(n_h):
        g = h // R
        t = q[:, (h // 2) * LANES:(h // 2 + 1) * LANES]
        if h % 2 != g % 2:
            t = pltpu.roll(t, HEAD_DIM, 1)
        t = jnp.where((lane8 // HEAD_DIM) == g % 2, t, 0.0)
        z = jnp.zeros((nq, LANES), F32)
        blocks.append(jnp.concatenate([t, z] if g // 2 == 0 else [z, t], axis=1))
    qg = jnp.concatenate(blocks, axis=0).astype(BF16)

    ri = lax.broadcasted_iota(jnp.int32, (rows, page), 0) % nq
    ci = lax.broadcasted_iota(jnp.int32, (rows, page), 1)
    new_mask = ci <= ri

    n_c = cmp_ref.shape[0]
    cmpv = cmp_ref[...]
    cc = lax.broadcasted_iota(jnp.int32, (rows, n_c), 1)
    s_c = _nt_dot(qg, cmpv[:, :GW].astype(BF16)) + bc_ref[...]
    p_c = _masked_softmax(s_c, cc < n_cmp)
    o_c = jnp.dot(p_c.astype(BF16), cmpv[:, GW:].astype(BF16), preferred_element_type=F32)
    ps = []
    for g in range(G):
        acc = p_c[g * R * nq:(g * R + 1) * nq]
        for r in range(1, R):
            acc = acc + p_c[(g * R + r) * nq:(g * R + r + 1) * nq]
        ps.append(acc)
    psum = _pad_rows(jnp.concatenate(ps, axis=0), LANES)
    n_blk = ovt_ref.shape[0]
    imp = _importance_t(ovt_ref[...], psum)
    jj = lax.broadcasted_iota(jnp.int32, (n_blk, LANES), 0)
    t_pos = P + lax.broadcasted_iota(jnp.int32, (n_blk, LANES), 1) % nq
    t_blk = t_pos // SEL_BLOCK
    forced = (jj == 0) | (jj == t_blk) | (jj == t_blk - 1)
    imp = jnp.where(jj * SEL_BLOCK <= t_pos, imp + jnp.where(forced, SEL_FORCE, 0.0), -1.0)
    sel_t = _select_topk_t(imp, n_blk)
    sel = _pad_rows(sel_t, LANES).T
    sel_rows = jnp.concatenate([sel[g * nq:(g + 1) * nq] for g in range(G) for _ in range(R)], axis=0).astype(BF16)

    per_tile = page // SEL_BLOCK
    m = jnp.full((rows, 1), NEG, F32)
    for p in range(n_pages):
        kt = pages[p][0].reshape(GW, page).astype(BF16)
        msk = jnp.dot(sel_rows, _block_expand(p * per_tile, LANES, page), preferred_element_type=F32) > 0.5
        s = jnp.where(msk, jnp.dot(qg, kt, preferred_element_type=F32) + bs_ref[:, p * page:(p + 1) * page], NEG)
        s_scr[:, p * page:(p + 1) * page] = s
        m = jnp.maximum(m, jnp.max(s, axis=-1, keepdims=True))
    kvn = _pad_rows(kvn_ref[...], page)
    msk = (jnp.dot(sel_rows, _block_expand(n_pages * per_tile, LANES, page), preferred_element_type=F32) > 0.5) & new_mask
    sn = jnp.where(msk, _nt_dot(qg, kvn[:, 2 * GW:3 * GW].astype(BF16)) + bs_ref[:, P:], NEG)
    m = jnp.maximum(m, jnp.max(sn, axis=-1, keepdims=True))
    pn = jnp.where(msk, jnp.exp(sn - m), 0.0)
    l = jnp.sum(pn, axis=-1, keepdims=True)
    acc = jnp.dot(pn.astype(BF16), kvn[:, 3 * GW:].astype(BF16), preferred_element_type=F32)
    for p in range(n_pages):
        pr = jnp.exp(s_scr[:, p * page:(p + 1) * page] - m)
        l = l + jnp.sum(pr, axis=-1, keepdims=True)
        acc = acc + _nt_dot(pr.astype(BF16), pages[p][1].reshape(GW, page).astype(BF16))
    o_s = acc / l

    kwt = win_ref[0].reshape(GW, n_win)
    vwt = win_ref[1].reshape(GW, n_win)
    rw = lax.broadcasted_iota(jnp.int32, (rows, n_win), 0) % nq
    cw = lax.broadcasted_iota(jnp.int32, (rows, n_win), 1)
    mask_w = (n_win + rw - cw) < WINDOW
    s_w = jnp.where(mask_w, jnp.dot(qg, kwt.astype(BF16), preferred_element_type=F32) + bw_ref[:, :n_win], NEG)
    wn = _pad_rows(wn_ref[...], page)
    s_n = jnp.where(new_mask, _nt_dot(qg, wn[:, :GW].astype(BF16)) + bw_ref[:, n_win:], NEG)
    m = jnp.maximum(jnp.max(s_w, axis=-1, keepdims=True), jnp.max(s_n, axis=-1, keepdims=True))
    p_w = jnp.where(mask_w, jnp.exp(s_w - m), 0.0)
    p_n = jnp.where(new_mask, jnp.exp(s_n - m), 0.0)
    l = jnp.sum(p_w, axis=-1, keepdims=True) + jnp.sum(p_n, axis=-1, keepdims=True)
    o_w = (_nt_dot(p_w.astype(BF16), vwt.astype(BF16))
           + jnp.dot(p_n.astype(BF16), wn[:, GW:].astype(BF16), preferred_element_type=F32)) / l

    gs = jax.nn.sigmoid(gate_ref[...])
    grep = jnp.concatenate([gs] * n_h, axis=0)
    glane = lax.broadcasted_iota(jnp.int32, (rows, LANES), 1)
    ghead = lax.broadcasted_iota(jnp.int32, (rows, LANES), 0) // nq
    o = jnp.zeros((rows, GW), F32)
    for br, o_b in enumerate((o_c, o_s, o_w)):
        o = o + jnp.sum(jnp.where(glane == br * n_h + ghead, grep, 0.0), axis=-1, keepdims=True) * o_b
    pieces = []
    for h in range(n_h):
        g = h // R
        t = o[h * nq:(h + 1) * nq, (g // 2) * LANES:(g // 2 + 1) * LANES]
        pieces.append(t if h % 2 == g % 2 else pltpu.roll(t, HEAD_DIM, 1))
    o_ref[...] = jnp.concatenate(
        [jnp.where(lane8 < HEAD_DIM, pieces[2 * u], pieces[2 * u + 1]) for u in range(n_h // 2)], axis=1)

    wt = win_ref[...].reshape(2 * GW, n_win)
    wnt = wn.T
    wout_ref[...] = jnp.concatenate([wt[:, nq:], wnt[:, :nq]], axis=1).reshape(wout_ref.shape)


def nsa_decode_attention(q, kvn, wn, gates, cmp, win_t, layer, pool_t, page_table, rel_bias):
    S, nq, W = q.shape
    G, R = NSA_KV_HEADS, NSA_HPG
    n_pages = page_table.shape[1]
    page = pool_t.shape[-1]
    P = n_pages * page
    n_win = win_t.shape[-1]
    assert nq < CMP_STRIDE and n_win == WINDOW and nq <= 8
    L = P + nq
    n_cmp = (L - CMP_LEN) // CMP_STRIDE + 1
    n_c = cmp.shape[1]
    n_slc = -(-L // SEL_BLOCK)
    n_blk = -(-n_slc // 8) * 8
    heads = range(G * R)
    qpos = P + np.arange(nq)
    bias_c = _dense_bias(rel_bias, qpos, np.arange(n_c) * CMP_STRIDE + CMP_LEN - 1, heads)
    bias_s = _dense_bias(rel_bias, qpos, np.arange(P + page), heads)
    bias_w = _dense_bias(rel_bias, qpos, P - n_win + np.arange(n_win + page), heads)
    ovt = _overlap_t(n_blk, n_c, n_cmp, n_slc)
    rows = G * R * nq
    const = lambda a: pl.BlockSpec(a.shape, lambda b, pt: (0,) * a.ndim)
    seq = lambda a: pl.BlockSpec((None,) + a.shape[1:], lambda b, pt: (b,) + (0,) * (a.ndim - 1))
    win_blk = (None, None) + win_t.shape[2:]
    grid_spec = pltpu.PrefetchScalarGridSpec(
        num_scalar_prefetch=1,
        grid=(S,),
        in_specs=[seq(q), seq(kvn), seq(wn), seq(gates), seq(cmp),
                  pl.BlockSpec(win_blk, lambda b, pt: (layer, b, 0, 0, 0, 0)),
                  const(bias_c), const(bias_s), const(bias_w), const(ovt)]
        + _page_specs(n_pages, (None, None, 2) + pool_t.shape[3:], lambda pg: (layer, pg, 1, 0, 0, 0)),
        out_specs=[seq(q), pl.BlockSpec((None,) + win_t.shape[2:], lambda b, pt: (b, 0, 0, 0, 0))],
        scratch_shapes=[pltpu.VMEM((rows, P), F32)],
    )
    return pl.pallas_call(
        functools.partial(_nsa_decode_kernel, n_pages=n_pages, page=page, n_cmp=n_cmp),
        grid_spec=grid_spec,
        out_shape=[jax.ShapeDtypeStruct(q.shape, F32), jax.ShapeDtypeStruct(win_t.shape[1:], F32)],
        compiler_params=_cparams("arbitrary"),
        name="nsa_decode",
    )(page_table, q, kvn, wn, gates, cmp, win_t, bias_c, bias_s, bias_w, ovt, *([pool_t] * n_pages))


def _w_cols(w, c0, width):
    piece = w[:, c0:c0 + width]
    if width % LANES:
        piece = jnp.pad(piece, ((0, 0), (0, LANES - width % LANES)))
    return piece.astype(BF16)


def _w_rows(w, c0, width):
    return w[:, c0:c0 + width].T.astype(BF16)


def _seq_major(a, nq, S):
    return jnp.transpose(a.reshape(nq, S, -1), (1, 0, 2))


def kernel(x_prompt, x_sample, cache_nsa_kv, state_nsa_win, cache_diff_k, cache_diff_v, cache_sb_kv, page_table, rel_bias, nsa_w_in, nsa_cmp_pos, nsa_cmp_w1, nsa_cmp_w2, nsa_w_out, diff_w_in, diff_lambda, diff_norm_g, diff_w_out, sb_w_in, sb_w_out, mlp_w_up, mlp_w_down, ln_g, ln_b):
    B, T, D = x_prompt.shape
    S, nq, _ = x_sample.shape
    depth = mlp_w_up.shape[0]
    alpha = (2 * depth) ** 0.25
    G, R, dk = NSA_KV_HEADS, NSA_HPG, HEAD_DIM
    n_pool, page = cache_nsa_kv.shape[1], cache_nsa_kv.shape[2]
    Hd, H = D // (2 * dk), D // dk
    xp = x_prompt.reshape(B * T, D)
    xs = jnp.transpose(x_sample, (1, 0, 2)).reshape(nq * S, D)
    nsa_pool_t = jnp.transpose(cache_nsa_kv, (0, 1, 3, 4, 5, 2))
    nsa_win_t = jnp.transpose(state_nsa_win, (0, 1, 3, 4, 5, 2))
    diff_k_t = jnp.transpose(cache_diff_k, (0, 1, 3, 4, 5, 2))
    sb_pool_t = jnp.transpose(cache_sb_kv, (0, 1, 3, 4, 5, 2))
    res = {k: [] for k in ("nsa_kv_p", "nsa_kv_s", "nsa_win_p", "nsa_win_s", "diff_k_p", "diff_k_s",
                           "diff_v_p", "diff_v_s", "sb_kv_p", "sb_kv_s")}
    tm_p = 512

    def to_tokens(a_t, lead):
        n, _, t = a_t.shape
        nd = len(lead)
        return jnp.transpose(a_t.reshape((n,) + lead + (t,)), (0, nd + 1) + tuple(range(1, nd + 1)))

    for i in range(depth):
        kind, j = i % N_MIXERS, i // N_MIXERS
        if kind == 0:
            w = nsa_w_in[j]
            nq_c, kv_c, win_c = G * R * dk, 4 * G * dk, 2 * G * dk
            row_p = [_w_cols(w, 0, nq_c), _w_cols(w, nq_c, 2 * G * dk), _w_cols(w, nq_c + kv_c + win_c, 3 * G * R)]
            t_ws = [_w_rows(w, nq_c, kv_c), _w_rows(w, nq_c + kv_c, win_c)]
            cmp_w = _compress_weights(nsa_cmp_pos[j], nsa_cmp_w1[j], nsa_cmp_w2[j])
            q, kc_rows, gates, kvt, wint = project(xp, row_p, t_ws, B, tm_p)
            cmp_p = compress_seq(kc_rows.reshape(B, T, -1), cmp_w)
            op = nsa_prompt_attention(q.reshape(B, T, -1), kvt, wint, cmp_p, gates.reshape(B, T, -1), rel_bias)
            row_s = [row_p[0], _w_cols(w, nq_c, kv_c), _w_cols(w, nq_c + kv_c, win_c), row_p[2]]
            qs, kvs, wns, gts, kvst = project(xs, row_s, t_ws[:1], nq, S)
            cmp_s = compress_paged(nsa_pool_t, j, page_table, cmp_w)
            os_, win_s = nsa_decode_attention(_seq_major(qs, nq, S), _seq_major(kvs, nq, S), _seq_major(wns, nq, S),
                                              _seq_major(gts, nq, S), cmp_s, nsa_win_t, j, nsa_pool_t, page_table,
                                              rel_bias)
            n_keep = min(WINDOW, T)
            res["nsa_kv_p"].append(to_tokens(kvt, (4, G, dk)))
            res["nsa_kv_s"].append(jnp.transpose(kvst.reshape(nq, 4, G, dk, S), (4, 0, 1, 2, 3)))
            res["nsa_win_p"].append(to_tokens(wint[:, :, T - n_keep:], (2, G, dk)))
            res["nsa_win_s"].append(jnp.transpose(win_s, (0, 4, 1, 2, 3)))
            w_out = nsa_w_out[j]
        elif kind == 1:
            lam_init = 0.8 - 0.6 * math.exp(-0.3 * i)
            w = diff_w_in[j]
            row_ws, t_ws = [_w_cols(w, 0, D), _w_cols(w, 2 * D, D)], [_w_rows(w, D, D)]
            q, v, kt = project(xp, row_ws, t_ws, B, tm_p)
            op = diff_prompt_attention(q.reshape(B, T, D), kt, v.reshape(B, T, D), rel_bias,
                                       diff_lambda[j], diff_norm_g[j], lam_init)
            qs, vsn, ksn, kst = project(xs, row_ws + [_w_cols(w, D, D)], t_ws, nq, S)
            vsn = _seq_major(vsn, nq, S)
            os_ = diff_decode_attention(_seq_major(qs, nq, S), _seq_major(ksn, nq, S), vsn, diff_k_t[j],
                                        cache_diff_v[j].reshape(n_pool, page * Hd, 2 * dk), page_table, rel_bias,
                                        diff_lambda[j], diff_norm_g[j], lam_init)
            res["diff_k_p"].append(to_tokens(kt, (Hd, 2, dk)))
            res["diff_k_s"].append(jnp.transpose(kst.reshape(nq, Hd, 2, dk, S), (4, 0, 1, 2, 3)))
            res["diff_v_p"].append(v.reshape(B, T, Hd, 2 * dk))
            res["diff_v_s"].append(vsn.reshape(S, nq, Hd, 2 * dk))
            w_out = diff_w_out[j]
        else:
            w = sb_w_in[j]
            row_ws, t_ws = [_w_cols(w, 0, D)], [_w_rows(w, D, 2 * D)]
            q, kvt = project(xp, row_ws, t_ws, B, tm_p)
            op = sb_prompt_attention(q.reshape(B, T, D), kvt)
            qs, kvs, kvst = project(xs, row_ws + [_w_cols(w, D, 2 * D)], t_ws, nq, S)
            os_ = sb_decode_attention(_seq_major(qs, nq, S), _seq_major(kvs, nq, S), sb_pool_t[j], page_table)
            res["sb_kv_p"].append(to_tokens(kvt, (2, H, dk)))
            res["sb_kv_s"].append(jnp.transpose(kvst.reshape(nq, 2, H, dk, S), (4, 0, 1, 2, 3)))
            w_out = sb_w_out[j]
        ln = jnp.stack([ln_g[i, 0], ln_b[i, 0], ln_g[i, 1], ln_b[i, 1]])
        w_out, w_up, w_down = w_out.astype(BF16), mlp_w_up[i].astype(BF16), mlp_w_down[i].astype(BF16)
        xp = post_mixer(op.reshape(B * T, D), w_out, xp, ln, w_up, w_down, alpha)
        os_ = jnp.transpose(os_, (1, 0, 2)).reshape(nq * S, D)
        xs = post_mixer(os_, w_out, xs, ln, w_up, w_down, alpha)
    return (xp.reshape(B, T, D), _seq_major(xs, nq, S),
            jnp.stack(res["nsa_kv_p"]), jnp.stack(res["nsa_kv_s"]), jnp.stack(res["nsa_win_p"]),
            jnp.stack(res["nsa_win_s"]), jnp.stack(res["diff_k_p"]), jnp.stack(res["diff_k_s"]),
            jnp.stack(res["diff_v_p"]), jnp.stack(res["diff_v_s"]), jnp.stack(res["sb_kv_p"]),
            jnp.stack(res["sb_kv_s"]))
```

```python
import functools
import math

import numpy as np
import jax
import jax.numpy as jnp
from jax import lax
from jax.experimental import pallas as pl
from jax.experimental.pallas import tpu as pltpu

F32 = jnp.float32
BF16 = jnp.bfloat16

HEAD_DIM = 64
NSA_KV_HEADS = 4
NSA_HPG = 4
CMP_LEN = 32
CMP_STRIDE = 16
SEL_BLOCK = 64
N_SEL = 16
SEL_FORCE = 1000.0
WINDOW = 512
REL_BUCKETS = 32
REL_MAX_DIST = 128
LN_EPS = 1e-5
NEG = -1e30
N_MIXERS = 3
SB_DEAD = 104.0

LANES = 128
VMEM_LIMIT = 48 * 1024 * 1024


def _cparams(*sem):
    return pltpu.CompilerParams(dimension_semantics=sem, vmem_limit_bytes=VMEM_LIMIT)


def _t5_bucket_np(dist):
    n = np.maximum(dist, 0)
    max_exact = REL_BUCKETS // 2
    nf = np.maximum(n, max_exact).astype(np.float32)
    large = max_exact + (np.log(nf / np.float32(max_exact)) / np.float32(math.log(REL_MAX_DIST / max_exact))
                         * np.float32(REL_BUCKETS - max_exact)).astype(np.int32)
    return np.where(n < max_exact, n, np.minimum(large, REL_BUCKETS - 1)).astype(np.int32)


FAR_DIST = int(np.min(np.nonzero(_t5_bucket_np(np.arange(4 * REL_MAX_DIST)) == REL_BUCKETS - 1)[0]))


def _bias_rows(rel_bias, dist):
    idx = jnp.asarray(_t5_bucket_np(dist))[None]
    rel = rel_bias.astype(F32)
    out = jnp.zeros((rel.shape[1],) + dist.shape, F32)
    for b in range(REL_BUCKETS):
        out = jnp.where(idx == b, rel[b].reshape((-1,) + (1,) * dist.ndim), out)
    return out


def _toeplitz_bias(rel_bias, tq, tk, n_off):
    i = np.arange(tq)[None, :, None]
    j = np.arange(tk)[None, None, :]
    off = np.arange(n_off)[:, None, None]
    return _bias_rows(rel_bias, off * tk + i - j)


def _dense_bias(rel_bias, qpos, kpos, cols):
    tab = _bias_rows(rel_bias, qpos[:, None] - kpos[None, :])[np.asarray(cols)]
    return tab.reshape(len(cols) * len(qpos), len(kpos))


def _layer_norm(z, g, b):
    mu = jnp.mean(z, axis=-1, keepdims=True)
    zc = z - mu
    var = jnp.mean(zc * zc, axis=-1, keepdims=True)
    return zc * lax.rsqrt(var + LN_EPS) * g + b


def _nt_dot(a, b):
    return lax.dot_general(a, b, (((1,), (1,)), ((), ())), preferred_element_type=F32)


def _project_kernel(*refs, n_row, n_t):
    x_ref = refs[0]
    row_w, t_w = refs[1:1 + n_row], refs[1 + n_row:1 + n_row + n_t]
    row_o, t_o = refs[1 + n_row + n_t:1 + 2 * n_row + n_t], refs[1 + 2 * n_row + n_t:]
    xb = x_ref[...].astype(BF16)
    for w_ref, o_ref in zip(row_w, row_o):
        o_ref[...] = jnp.dot(xb, w_ref[...], preferred_element_type=F32)
    for w_ref, o_ref in zip(t_w, t_o):
        o_ref[...] = _nt_dot(w_ref[...], xb)


def project(x, row_ws, t_ws, n_seq, tm):
    M, K = x.shape
    T = M // n_seq
    nt = T // tm
    n_row, n_t = len(row_ws), len(t_ws)
    const = lambda w: pl.BlockSpec(w.shape, lambda b, i: (0, 0))
    return pl.pallas_call(
        functools.partial(_project_kernel, n_row=n_row, n_t=n_t),
        grid=(n_seq, nt),
        in_specs=[pl.BlockSpec((tm, K), lambda b, i: (b * nt + i, 0))] + [const(w) for w in row_ws + t_ws],
        out_specs=[pl.BlockSpec((tm, w.shape[1]), lambda b, i: (b * nt + i, 0)) for w in row_ws]
        + [pl.BlockSpec((None, w.shape[0], tm), lambda b, i: (b, 0, i)) for w in t_ws],
        out_shape=[jax.ShapeDtypeStruct((M, w.shape[1]), F32) for w in row_ws]
        + [jax.ShapeDtypeStruct((n_seq, w.shape[0], T), F32) for w in t_ws],
        compiler_params=_cparams("parallel", "parallel"),
        name="project",
    )(x, *row_ws, *t_ws)


def _post_mixer_kernel(o_ref, wo_ref, x_ref, ln_ref, wu_ref, wd_ref, y_ref, x1_scr, xb_scr, acc_scr, *, alpha):
    f = pl.program_id(1)

    @pl.when(f == 0)
    def _():
        y = jnp.dot(o_ref[...].astype(BF16), wo_ref[...], preferred_element_type=F32)
        x1 = _layer_norm(alpha * x_ref[...] + y, ln_ref[0:1, :], ln_ref[1:2, :])
        x1_scr[...] = x1
        xb_scr[...] = x1.astype(BF16)
        acc_scr[...] = jnp.zeros_like(acc_scr)

    h = jnp.dot(xb_scr[...], wu_ref[...], preferred_element_type=F32)
    h = jnp.square(jnp.maximum(h, 0.0))
    acc_scr[...] += jnp.dot(h.astype(BF16), wd_ref[...], preferred_element_type=F32)

    @pl.when(f == pl.num_programs(1) - 1)
    def _():
        y_ref[...] = _layer_norm(alpha * x1_scr[...] + acc_scr[...], ln_ref[2:3, :], ln_ref[3:4, :])


def post_mixer(o, w_out, x, ln, w_up, w_down, alpha, tm=512, tf=1024):
    M, D = x.shape
    Fd = w_up.shape[1]
    return pl.pallas_call(
        functools.partial(_post_mixer_kernel, alpha=alpha),
        grid=(M // tm, Fd // tf),
        in_specs=[
            pl.BlockSpec((tm, D), lambda i, f: (i, 0)),
            pl.BlockSpec((D, D), lambda i, f: (0, 0)),
            pl.BlockSpec((tm, D), lambda i, f: (i, 0)),
            pl.BlockSpec((4, D), lambda i, f: (0, 0)),
            pl.BlockSpec((D, tf), lambda i, f: (0, f)),
            pl.BlockSpec((tf, D), lambda i, f: (f, 0)),
        ],
        out_specs=pl.BlockSpec((tm, D), lambda i, f: (i, 0)),
        out_shape=jax.ShapeDtypeStruct((M, D), F32),
        scratch_shapes=[pltpu.VMEM((tm, D), F32), pltpu.VMEM((tm, D), BF16), pltpu.VMEM((tm, D), F32)],
        compiler_params=_cparams("parallel", "arbitrary"),
        name="post_mixer",
    )(o, w_out, x, ln, w_up, w_down)


LOG2E = math.log2(math.e)


def _flash_step(carry, s, pv):
    m, l, acc = carry
    m_new = jnp.maximum(m, jnp.max(s, axis=-1, keepdims=True))
    alpha = jnp.exp2(m - m_new)
    p = jnp.exp2(s - m_new)
    if l is not None:
        l = alpha * l + jnp.sum(p, axis=-1, keepdims=True)
    acc = alpha * acc + pv(p.astype(BF16))
    return m_new, l, acc


def _flash_init(rows, width, with_l=True):
    return jnp.full((rows, 1), NEG, F32), jnp.zeros((rows, 1), F32) if with_l else None, jnp.zeros((rows, width), F32)


def _split_heads(q, tq):
    lane = lax.broadcasted_iota(jnp.int32, (tq, LANES), 1)
    lo = jnp.where(lane < HEAD_DIM, q, 0.0)
    hi = jnp.where(lane >= HEAD_DIM, q, 0.0)
    return jnp.concatenate([lo, hi], axis=0).astype(BF16)


def _cols(ref, start, width):
    return ref[:, pl.ds(pl.multiple_of(start, width), width)].astype(BF16)


def _rows(ref, start, height):
    return ref[pl.ds(pl.multiple_of(start, height), height), :].astype(BF16)


def _rows_blockdiag(q, n_blk, blk_w):
    rep = jnp.concatenate([q] * n_blk, axis=0)
    r = lax.broadcasted_iota(jnp.int32, rep.shape, 0) // q.shape[0]
    c = lax.broadcasted_iota(jnp.int32, rep.shape, 1) // blk_w
    return jnp.where(r == c, rep, 0.0).astype(BF16)


def _pad_rows(a, rows):
    return jnp.concatenate([a, jnp.zeros((rows - a.shape[0], a.shape[1]), a.dtype)], axis=0)


def _page_specs(n_pages, block, index):
    return [pl.BlockSpec(block, functools.partial(lambda b, pt, p: index(pt[b, p]), p=p)) for p in range(n_pages)]


def _diff_lambda(lam_ref, lam_init):
    lf = lam_ref[...]
    a = jnp.sum(lf[0:1, :] * lf[1:2, :], axis=-1, keepdims=True)
    b = jnp.sum(lf[2:3, :] * lf[3:4, :], axis=-1, keepdims=True)
    return jnp.exp(a) - jnp.exp(b) + lam_init


def _diff_finish(o1, o2, lam_full, g, lam_init):
    o = o1 - lam_full * o2
    o = o * lax.rsqrt(jnp.mean(o * o, axis=-1, keepdims=True) + LN_EPS) * g
    return o * (1.0 - lam_init)


def _causal_bias(tq, tk):
    ri = lax.broadcasted_iota(jnp.int32, (tq, tk), 0)
    ci = lax.broadcasted_iota(jnp.int32, (tq, tk), 1)
    return jnp.where(ci <= ri, 0.0, NEG)


def _add_per_head(s, b, n_blocks):
    tq, tk = b.shape
    return (s.reshape(n_blocks, tq, tk) + b[None]).reshape(n_blocks * tq, tk)


def _diff_prompt_kernel(q_ref, kt_ref, v_ref, toe_ref, lam_ref, g_ref, o_ref, *, tq, lam_init):
    qt = pl.program_id(2)
    tk = tq
    rows = 2 * tq
    q2 = _split_heads(q_ref[...] * (HEAD_DIM ** -0.5 * LOG2E), tq)

    def qk(kt):
        return jnp.dot(q2, _cols(kt_ref, kt * tk, tk), preferred_element_type=F32)

    def step(kt, carry, bias):
        s = qk(kt)
        if bias is not None:
            s = s + bias
        v = _rows(v_ref, kt * tk, tk)
        return _flash_step(carry, s, lambda p: jnp.dot(p, v, preferred_element_type=F32))

    near0 = jnp.maximum(qt - 1, 0)
    carry = lax.fori_loop(0, near0, lambda kt, c: step(kt, c, None), _flash_init(rows, LANES))
    carry = lax.fori_loop(near0, qt, lambda kt, c: step(kt, c, toe_ref[1]), carry)
    _, l, acc = step(qt, carry, _add_per_head(toe_ref[0], _causal_bias(tq, tk), 2))
    o = acc / l
    o_ref[...] = _diff_finish(o[:tq], o[tq:], _diff_lambda(lam_ref, lam_init), g_ref[...], lam_init)


def _toeplitz_rel(rel_bias, tq, tk, n_off):
    far = rel_bias[REL_BUCKETS - 1].astype(F32)[:, None, None, None]
    return (_toeplitz_bias(rel_bias, tq, tk, n_off) - far) * LOG2E


def diff_prompt_attention(q, kt, v, rel_bias, lam, norm_g, lam_init, tq=512):
    B, T, W = q.shape
    n_heads = W // LANES
    toe = _toeplitz_rel(rel_bias, tq, tq, 2)
    toe = jnp.concatenate([toe[:n_heads], toe[n_heads:]], axis=2)
    return pl.pallas_call(
        functools.partial(_diff_prompt_kernel, tq=tq, lam_init=lam_init),
        grid=(B, n_heads, T // tq),
        in_specs=[
            pl.BlockSpec((None, tq, LANES), lambda b, h, i: (b, i, h)),
            pl.BlockSpec((None, LANES, T), lambda b, h, i: (b, h, 0)),
            pl.BlockSpec((None, T, LANES), lambda b, h, i: (b, 0, h)),
            pl.BlockSpec((None, 2, 2 * tq, tq), lambda b, h, i: (h, 0, 0, 0)),
            pl.BlockSpec((4, HEAD_DIM), lambda b, h, i: (0, 0)),
            pl.BlockSpec((1, LANES), lambda b, h, i: (0, 0)),
        ],
        out_specs=pl.BlockSpec((None, tq, LANES), lambda b, h, i: (b, i, h)),
        out_shape=jax.ShapeDtypeStruct((B, T, W), F32),
        compiler_params=_cparams("parallel", "parallel", "arbitrary"),
        name="diff_prompt",
    )(q, kt, v, toe, lam, norm_g.reshape(1, LANES))


def _diff_decode_kernel(pt_ref, q_ref, kn_ref, vn_ref, bias_ref, lam_ref, g_ref, *rest, n_pages, page, lam_init):
    kp, vp = rest[:n_pages], rest[n_pages:2 * n_pages]
    o_ref, s_scr = rest[2 * n_pages], rest[2 * n_pages + 1]
    nq, W = q_ref.shape
    n_heads = W // LANES
    hr = 2 * nq
    rows = n_heads * hr
    qbd = _rows_blockdiag(q_ref[...] * HEAD_DIM ** -0.5, 2 * n_heads, HEAD_DIM)

    hblk = lax.broadcasted_iota(jnp.int32, (rows, page), 0) // hr

    def pv(p, v_of_head):
        p_bd = jnp.concatenate([jnp.where(hblk == h, p, jnp.zeros_like(p)) for h in range(n_heads)], axis=1)
        v_hk = jnp.concatenate([v_of_head(h) for h in range(n_heads)], axis=0)
        return jnp.dot(p_bd, v_hk, preferred_element_type=F32)

    m = jnp.full((rows, 1), NEG, F32)
    for p in range(n_pages):
        kt = kp[p][...].reshape(W, page).astype(BF16)
        s = jnp.dot(qbd, kt, preferred_element_type=F32) + bias_ref[:, p * page:(p + 1) * page]
        s_scr[:, p * page:(p + 1) * page] = s
        m = jnp.maximum(m, jnp.max(s, axis=-1, keepdims=True))
    ri = lax.broadcasted_iota(jnp.int32, (rows, page), 0) % nq
    ci = lax.broadcasted_iota(jnp.int32, (rows, page), 1)
    mask = ci <= ri
    kn = _pad_rows(kn_ref[...], page).astype(BF16)
    vn = _pad_rows(vn_ref[...], page).astype(BF16)
    sn = jnp.where(mask, _nt_dot(qbd, kn) + bias_ref[:, n_pages * page:], NEG)
    m = jnp.maximum(m, jnp.max(sn, axis=-1, keepdims=True))
    pn = jnp.where(mask, jnp.exp(sn - m), 0.0)
    l = jnp.sum(pn, axis=-1, keepdims=True)
    acc = pv(pn.astype(BF16), lambda h: vn[:, h * LANES:(h + 1) * LANES])
    for p in range(n_pages):
        pr = jnp.exp(s_scr[:, p * page:(p + 1) * page] - m)
        l = l + jnp.sum(pr, axis=-1, keepdims=True)
        acc = acc + pv(pr.astype(BF16), lambda h: vp[p][pl.ds(h, page, stride=n_heads), :].astype(BF16))
    o = acc / l
    lam_full = _diff_lambda(lam_ref, lam_init)
    outs = [_diff_finish(o[h * hr:h * hr + nq], o[h * hr + nq:(h + 1) * hr], lam_full, g_ref[...], lam_init)
            for h in range(n_heads)]
    o_ref[...] = jnp.concatenate(outs, axis=1)


def diff_decode_attention(q, kn, vn, kt_pool, v_pool, page_table, rel_bias, lam, norm_g, lam_init):
    S, nq, W = q.shape
    n_pages = page_table.shape[1]
    page = kt_pool.shape[-1]
    n_heads = W // LANES
    P = n_pages * page
    cols = [mp * n_heads + h for h in range(n_heads) for mp in range(2)]
    bias = _dense_bias(rel_bias, P + np.arange(nq), np.arange(P + page), cols)
    rows = bias.shape[0]
    seq = pl.BlockSpec((None, nq, W), lambda b, pt: (b, 0, 0))
    grid_spec = pltpu.PrefetchScalarGridSpec(
        num_scalar_prefetch=1,
        grid=(S,),
        in_specs=[seq, seq, seq,
                  pl.BlockSpec(bias.shape, lambda b, pt: (0, 0)),
                  pl.BlockSpec((4, HEAD_DIM), lambda b, pt: (0, 0)),
                  pl.BlockSpec((1, LANES), lambda b, pt: (0, 0))]
        + _page_specs(n_pages, (None,) + kt_pool.shape[1:], lambda pg: (pg, 0, 0, 0, 0))
        + _page_specs(n_pages, (None,) + v_pool.shape[1:], lambda pg: (pg, 0, 0)),
        out_specs=seq,
        scratch_shapes=[pltpu.VMEM((rows, P), F32)],
    )
    return pl.pallas_call(
        functools.partial(_diff_decode_kernel, n_pages=n_pages, page=page, lam_init=lam_init),
        grid_spec=grid_spec,
        out_shape=jax.ShapeDtypeStruct((S, nq, W), F32),
        compiler_params=_cparams("arbitrary"),
        name="diff_decode",
    )(page_table, q, kn, vn, bias, lam, norm_g.reshape(1, LANES), *([kt_pool] * n_pages), *([v_pool] * n_pages))


def _softplus(z):
    return jnp.maximum(z, 0.0) + jnp.log(1.0 + jnp.exp(-jnp.abs(z)))


def _suffix_sum(c, u):
    hi = c.astype(BF16)
    lo = (c - hi.astype(F32)).astype(BF16)
    return jnp.dot(hi, u, preferred_element_type=F32) + jnp.dot(lo, u, preferred_element_type=F32)


def _sb_local(z, u, mask=None):
    c = _softplus(z)
    if mask is not None:
        c = jnp.where(mask, c, 0.0)
    return z - c - _suffix_sum(c, u), jnp.sum(c, axis=-1, keepdims=True)


def _sb_weights(e, r, mask=None):
    a = jnp.exp(e - r)
    if mask is not None:
        a = jnp.where(mask, a, 0.0)
    return a.astype(BF16)


def _strict_upper(n):
    return jnp.asarray(np.arange(n)[:, None] > np.arange(n)[None, :], BF16)


def _sb_prompt_kernel(q_ref, kt_ref, vt_ref, u_ref, o_ref, *, tq, tk):
    qt = pl.program_id(2)
    n_sub = tq // tk
    rows = 2 * tq
    q2 = _split_heads(q_ref[...] * HEAD_DIM ** -0.5, tq)
    u = u_ref[...]

    def block(kt, carry, diag):
        r, acc = carry
        local = []
        for sub in range(n_sub):
            mask = None
            if diag:
                ri = lax.broadcasted_iota(jnp.int32, (rows, tk), 0) % tq
                ci = lax.broadcasted_iota(jnp.int32, (rows, tk), 1) + sub * tk
                mask = ci < ri
            z = jnp.dot(q2, _cols(kt_ref, kt * tq + sub * tk, tk), preferred_element_type=F32)
            local.append(_sb_local(z, u, mask) + (mask,))
        for sub in reversed(range(n_sub)):
            e, tot, mask = local[sub]
            acc = acc + _nt_dot(_sb_weights(e, r, mask), _cols(vt_ref, kt * tq + sub * tk, tk))
            r = r + tot
        return r, acc

    r, acc = block(qt, (jnp.zeros((rows, 1), F32), jnp.zeros((rows, LANES), F32)), True)

    def cond(c):
        return jnp.logical_and(c[0] >= 0, jnp.min(c[1]) < SB_DEAD)

    def body(c):
        r, acc = block(c[0], (c[1], c[2]), False)
        return c[0] - 1, r, acc

    _, _, acc = lax.while_loop(cond, body, (qt - 1, r, acc))
    lane = lax.broadcasted_iota(jnp.int32, (tq, LANES), 1)
    o_ref[...] = jnp.where(lane < HEAD_DIM, acc[:tq], acc[tq:])


def sb_prompt_attention(q, kvt, tq=512, tk=256):
    B, T, W = q.shape
    n_pairs = W // LANES
    return pl.pallas_call(
        functools.partial(_sb_prompt_kernel, tq=tq, tk=tk),
        grid=(B, n_pairs, T // tq),
        in_specs=[
            pl.BlockSpec((None, tq, LANES), lambda b, h, i: (b, i, h)),
            pl.BlockSpec((None, LANES, T), lambda b, h, i: (b, h, 0)),
            pl.BlockSpec((None, LANES, T), lambda b, h, i: (b, n_pairs + h, 0)),
            pl.BlockSpec((tk, tk), lambda b, h, i: (0, 0)),
        ],
        out_specs=pl.BlockSpec((None, tq, LANES), lambda b, h, i: (b, i, h)),
        out_shape=jax.ShapeDtypeStruct((B, T, W), F32),
        compiler_params=_cparams("parallel", "parallel", "arbitrary"),
        name="sb_prompt",
    )(q, kvt, kvt, _strict_upper(tk))


def _sb_decode_kernel(pt_ref, q_ref, kvn_ref, u_ref, *rest, n_pages, page):
    pages, o_ref, e_scr = rest[:n_pages], rest[n_pages], rest[n_pages + 1]
    nq, W = q_ref.shape
    n_heads = W // HEAD_DIM
    rows = n_heads * nq
    u = u_ref[...]
    qbd = _rows_blockdiag(q_ref[...] * HEAD_DIM ** -0.5, n_heads, HEAD_DIM)
    kvn = _pad_rows(kvn_ref[...], page).astype(BF16)
    ri = lax.broadcasted_iota(jnp.int32, (rows, page), 0) % nq
    ci = lax.broadcasted_iota(jnp.int32, (rows, page), 1)
    new_mask = ci < ri
    e_new, r = _sb_local(_nt_dot(qbd, kvn[:, :W]), u, new_mask)
    tots = []
    for p in range(n_pages):
        kt = pages[p][0].reshape(W, page).astype(BF16)
        e, tot = _sb_local(jnp.dot(qbd, kt, preferred_element_type=F32), u)
        e_scr[:, p * page:(p + 1) * page] = e
        tots.append(tot)
    acc = jnp.dot(_sb_weights(e_new, 0.0, new_mask), kvn[:, W:], preferred_element_type=F32)
    for p in reversed(range(n_pages)):
        vt = pages[p][1].reshape(W, page).astype(BF16)
        acc = acc + _nt_dot(_sb_weights(e_scr[:, p * page:(p + 1) * page], r), vt)
        r = r + tots[p]
    col = lax.broadcasted_iota(jnp.int32, (nq, W), 1) // HEAD_DIM
    o = jnp.zeros((nq, W), F32)
    for h in range(n_heads):
        o = o + jnp.where(col == h, acc[h * nq:(h + 1) * nq, :], 0.0)
    o_ref[...] = o


def sb_decode_attention(q, kvn, kvt_pool, page_table):
    S, nq, W = q.shape
    n_pages = page_table.shape[1]
    page = kvt_pool.shape[-1]
    grid_spec = pltpu.PrefetchScalarGridSpec(
        num_scalar_prefetch=1,
        grid=(S,),
        in_specs=[pl.BlockSpec((None, nq, W), lambda b, pt: (b, 0, 0)),
                  pl.BlockSpec((None, nq, 2 * W), lambda b, pt: (b, 0, 0)),
                  pl.BlockSpec((page, page), lambda b, pt: (0, 0))]
        + _page_specs(n_pages, (None,) + kvt_pool.shape[1:], lambda pg: (pg, 0, 0, 0, 0)),
        out_specs=pl.BlockSpec((None, nq, W), lambda b, pt: (b, 0, 0)),
        scratch_shapes=[pltpu.VMEM((W // HEAD_DIM * nq, n_pages * page), F32)],
    )
    return pl.pallas_call(
        functools.partial(_sb_decode_kernel, n_pages=n_pages, page=page),
        grid_spec=grid_spec,
        out_shape=jax.ShapeDtypeStruct((S, nq, W), F32),
        compiler_params=_cparams("arbitrary"),
        name="sb_decode",
    )(page_table, q, kvn, _strict_upper(page), *([kvt_pool] * n_pages))


N_CMP_TILES = 2 * NSA_KV_HEADS * HEAD_DIM // LANES


def _gelu_tanh(x):
    return 0.5 * x * (1.0 + jnp.tanh(math.sqrt(2.0 / math.pi) * (x + 0.044715 * x * x * x)))


def _compress_body(chunk_rows, w1_ref, pos_ref, w2_ref, o_ref, n_chunks):
    n_kinds = w1_ref.shape[0]
    tiles_per_kind = NSA_KV_HEADS * HEAD_DIM // LANES
    for kind in range(n_kinds):
        xcat = jnp.concatenate(
            [jnp.concatenate([chunk_rows(l, kind * tiles_per_kind + t) for l in range(CMP_STRIDE)], axis=1)
             for t in range(tiles_per_kind)], axis=0)
        u = []
        for a in range(2):
            xa = (xcat + pos_ref[kind, a]).astype(BF16)
            u.append(jnp.dot(xa, w1_ref[kind, a], preferred_element_type=F32))
        for t in range(tiles_per_kind):
            u0 = u[0][t * n_chunks:(t + 1) * n_chunks]
            u1 = pltpu.roll(u[1][t * n_chunks:(t + 1) * n_chunks], n_chunks - 1, 0)
            hid = _gelu_tanh(u0 + u1)
            out = jnp.dot(hid.astype(BF16), w2_ref[kind], preferred_element_type=F32)
            col = (kind * tiles_per_kind + t) * LANES
            o_ref[:, col:col + LANES] = out


def _compress_seq_kernel(*refs, n_chunks):
    x_refs = refs[:N_CMP_TILES]
    w1_ref, pos_ref, w2_ref, o_ref = refs[N_CMP_TILES:]

    def chunk_rows(l, tile):
        return x_refs[tile][pl.ds(l, n_chunks, stride=CMP_STRIDE), :]
    _compress_body(chunk_rows, w1_ref, pos_ref, w2_ref, o_ref, n_chunks)


def _compress_paged_kernel(pt_ref, w1_ref, pos_ref, w2_ref, *rest, n_pages, page):
    pages, o_ref, x_scr = rest[:n_pages], rest[n_pages], rest[n_pages + 1]
    tiles_per_kind = NSA_KV_HEADS * HEAD_DIM // LANES
    for p in range(n_pages):
        for t in range(N_CMP_TILES):
            kind, pair = t // tiles_per_kind, t % tiles_per_kind
            xt = pages[p][kind, 2 * pair:2 * pair + 2].reshape(LANES, page)
            x_scr[t, p * page:(p + 1) * page, :] = xt.T
    n_chunks = n_pages * page // CMP_STRIDE

    def chunk_rows(l, tile):
        return x_scr[tile, pl.ds(l, n_chunks, stride=CMP_STRIDE), :]
    _compress_body(chunk_rows, w1_ref, pos_ref, w2_ref, o_ref, n_chunks)


def _compress_weights(cmp_pos, cmp_w1, cmp_w2):
    eye2 = jnp.eye(2, dtype=F32)
    w1r = cmp_w1.reshape(2, 2, CMP_STRIDE, HEAD_DIM, -1)
    w1e = jnp.einsum('kaldj,hg->kalhdgj', w1r, eye2)
    w1e = w1e.reshape(2, 2, CMP_STRIDE * LANES, 2 * cmp_w1.shape[-1]).astype(BF16)
    pos = jnp.broadcast_to(cmp_pos.reshape(2, 2, CMP_STRIDE, 1, HEAD_DIM), (2, 2, CMP_STRIDE, 2, HEAD_DIM))
    pos = pos.reshape(2, 2, 1, CMP_STRIDE * LANES).astype(F32)
    w2e = jnp.einsum('kjd,hg->khjgd', cmp_w2, eye2).reshape(2, 2 * cmp_w2.shape[1], LANES).astype(BF16)
    return w1e, pos, w2e


def compress_seq(kv, cmp_w):
    B, L, _ = kv.shape
    w1e, pos, w2e = cmp_w
    n_chunks = L // CMP_STRIDE
    W = N_CMP_TILES * LANES
    return pl.pallas_call(
        functools.partial(_compress_seq_kernel, n_chunks=n_chunks),
        grid=(B,),
        in_specs=[pl.BlockSpec((None, L, LANES), functools.partial(lambda b, t: (b, 0, t), t=t)) for t in range(N_CMP_TILES)]
        + [pl.BlockSpec(w1e.shape, lambda b: (0, 0, 0, 0)),
           pl.BlockSpec(pos.shape, lambda b: (0, 0, 0, 0)),
           pl.BlockSpec(w2e.shape, lambda b: (0, 0, 0))],
        out_specs=pl.BlockSpec((None, n_chunks, W), lambda b: (b, 0, 0)),
        out_shape=jax.ShapeDtypeStruct((B, n_chunks, W), F32),
        compiler_params=_cparams("parallel"),
        name="nsa_compress_seq",
    )(*([kv] * N_CMP_TILES), w1e, pos, w2e)


def compress_paged(pool_t, layer, page_table, cmp_w):
    S, n_pages = page_table.shape
    page = pool_t.shape[-1]
    w1e, pos, w2e = cmp_w
    n_chunks = n_pages * page // CMP_STRIDE
    W = N_CMP_TILES * LANES
    blk = (None, None, 2) + pool_t.shape[3:]
    grid_spec = pltpu.PrefetchScalarGridSpec(
        num_scalar_prefetch=1,
        grid=(S,),
        in_specs=[pl.BlockSpec(w1e.shape, lambda b, pt: (0, 0, 0, 0)),
                  pl.BlockSpec(pos.shape, lambda b, pt: (0, 0, 0, 0)),
                  pl.BlockSpec(w2e.shape, lambda b, pt: (0, 0, 0))]
        + _page_specs(n_pages, blk, lambda pg: (layer, pg, 0, 0, 0, 0)),
        out_specs=pl.BlockSpec((None, n_chunks, W), lambda b, pt: (b, 0, 0)),
        scratch_shapes=[pltpu.VMEM((N_CMP_TILES, n_pages * page, LANES), F32)],
    )
    return pl.pallas_call(
        functools.partial(_compress_paged_kernel, n_pages=n_pages, page=page),
        grid_spec=grid_spec,
        out_shape=jax.ShapeDtypeStruct((S, n_chunks, W), F32),
        compiler_params=_cparams("arbitrary"),
        name="nsa_compress_paged",
    )(page_table, w1e, pos, w2e, *([pool_t] * n_pages))


def _masked_softmax(s, mask, exp=jnp.exp):
    s = jnp.where(mask, s, NEG)
    m = jnp.max(s, axis=-1, keepdims=True)
    e = jnp.where(mask, exp(s - m), 0.0)
    return e * (1.0 / jnp.maximum(jnp.sum(e, axis=-1, keepdims=True), 1e-30))


def _importance_t(ov_t, psum):
    hi = psum.astype(BF16)
    lo = (psum - hi.astype(F32)).astype(BF16)
    return _nt_dot(ov_t, hi) + _nt_dot(ov_t, lo)


def _select_topk_t(imp, n_rows):
    jj = lax.broadcasted_iota(jnp.int32, imp.shape, 0)
    cnt = jnp.zeros(imp.shape, F32)
    for i in range(n_rows):
        row = imp[i:i + 1, :]
        ahead = (row > imp) | ((row == imp) & (jj > i))
        cnt = cnt + jnp.where(ahead, 1.0, 0.0)
    return jnp.where(cnt < N_SEL, 1.0, 0.0)


def _select_topk_ref(imp_ref, n_active):
    imp = imp_ref[...]
    jj = lax.broadcasted_iota(jnp.int32, imp.shape, 0)

    def body(i, cnt):
        row = imp_ref[pl.ds(i, 1), :]
        ahead = (row > imp) | ((row == imp) & (jj > i))
        return cnt + jnp.where(ahead, 1.0, 0.0)

    cnt = lax.fori_loop(0, n_active, body, jnp.zeros(imp.shape, F32))
    return jnp.where(cnt < N_SEL, 1.0, 0.0)


def _block_expand(first_block, n_blocks, tk):
    j = lax.broadcasted_iota(jnp.int32, (n_blocks, tk), 0)
    col = lax.broadcasted_iota(jnp.int32, (n_blocks, tk), 1)
    return jnp.where(j == first_block + col // SEL_BLOCK, 1.0, 0.0).astype(BF16)


CMP_NEAR_BACK = -(-(CMP_LEN - 1 + FAR_DIST) // CMP_STRIDE)


def _nsa_prompt_kernel(q_ref, kst_ref, vst_ref, kwt_ref, vwt_ref, kc_ref, vc_ref, gate_ref, pat_ref, toe_ref,
                       ovt_ref, o_ref, imp_scr, *, tq, n_cmp):
    g = pl.program_id(1)
    qt = pl.program_id(2)
    tk = tq
    R = NSA_HPG
    par = g % 2
    lane = lax.broadcasted_iota(jnp.int32, (tq, LANES), 1)
    own = (lane // HEAD_DIM) == par
    q = q_ref[...] * (HEAD_DIM ** -0.5 * LOG2E)
    parts = []
    for r in range(R):
        t = q[:, (r // 2) * LANES:(r // 2 + 1) * LANES]
        src = jnp.where(par == r % 2, t, pltpu.roll(t, HEAD_DIM, 1))
        parts.append(jnp.where(own, src, 0.0))
    q4 = jnp.concatenate(parts, axis=0).astype(BF16)
    rows = R * tq

    def toe(off):
        return toe_ref[:, off].reshape(rows, tk)

    n_c = kc_ref.shape[0]
    cc = lax.broadcasted_iota(jnp.int32, (rows, n_c), 1)
    tt = qt * tq + lax.broadcasted_iota(jnp.int32, (rows, n_c), 0) % tq
    mask_c = (cc * CMP_STRIDE + CMP_LEN - 1 <= tt) & (cc < n_cmp)
    pat = pat_ref[...].reshape(rows, LANES)
    uu = lax.broadcasted_iota(jnp.int32, (LANES, n_c), 0)
    shift = jnp.where(lax.broadcasted_iota(jnp.int32, (LANES, n_c), 1) == qt * (tq // CMP_STRIDE) - CMP_NEAR_BACK + uu,
                      1.0, 0.0).astype(BF16)
    s_c = _nt_dot(q4, kc_ref[...].astype(BF16)) + jnp.dot(pat.astype(BF16), shift, preferred_element_type=F32)
    p_c = _masked_softmax(s_c, mask_c, jnp.exp2)
    o_c = jnp.dot(p_c.astype(BF16), vc_ref[...].astype(BF16), preferred_element_type=F32)
    psum = p_c[0:tq]
    for r in range(1, R):
        psum = psum + p_c[r * tq:(r + 1) * tq]
    n_blk = ovt_ref.shape[0]
    imp = _importance_t(ovt_ref[...], psum)
    jj = lax.broadcasted_iota(jnp.int32, (n_blk, tq), 0)
    tq_pos = qt * tq + lax.broadcasted_iota(jnp.int32, (n_blk, tq), 1)
    t_blk = tq_pos // SEL_BLOCK
    forced = (jj == 0) | (jj == t_blk) | (jj == t_blk - 1)
    imp_scr[...] = jnp.where(jj * SEL_BLOCK <= tq_pos, imp + jnp.where(forced, SEL_FORCE, 0.0), -1.0)
    sel_t = _select_topk_ref(imp_scr, jnp.minimum((qt + 1) * (tq // SEL_BLOCK), n_blk))
    sel = _pad_rows(sel_t, LANES).T
    sel_neg = ((sel - 1.0) * -NEG).astype(BF16)

    own_rows = (lax.broadcasted_iota(jnp.int32, (LANES, 1), 0) // HEAD_DIM) == par

    def branch(lhs, rhs, v_ref, first, tile_bias):
        def qk(kt):
            return jnp.dot(lhs, rhs(kt), preferred_element_type=F32)

        def finish_tile(kt, carry, s, toe_off, extra):
            b = tile_bias(kt)
            if extra is not None:
                b = extra if b is None else b + extra
            if toe_off is not None:
                s = s + toe(toe_off)
            if b is not None:
                s = _add_per_head(s, b, R)
            vt = jnp.where(own_rows, _cols(v_ref, kt * tk, tk), 1.0)
            return _flash_step(carry, s, lambda p: _nt_dot(p, vt))

        def step(kt, carry, toe_off):
            return finish_tile(kt, carry, qk(kt), toe_off, None)

        carry = _flash_init(rows, LANES, with_l=False)
        carry = lax.fori_loop(first, near0, lambda kt, c: step(kt, c, None), carry)
        carry = lax.fori_loop(jnp.maximum(near0, first), qt, lambda kt, c: step(kt, c, 1), carry)
        _, _, acc = finish_tile(qt, carry, qk(qt), 0, _causal_bias(tq, tk))
        return acc / pltpu.roll(acc, HEAD_DIM, 1)

    near0 = jnp.maximum(qt - 1, 0)

    sel_lhs = jnp.concatenate([q4, jnp.concatenate([sel_neg] * R, axis=0)], axis=1)

    def sel_rhs(kt):
        return jnp.concatenate([_cols(kst_ref, kt * tk, tk), _block_expand(kt * (tk // SEL_BLOCK), LANES, tk)], axis=0)

    o_s = branch(sel_lhs, sel_rhs, vst_ref, 0, lambda kt: None)

    n_back = WINDOW // tk
    ri = lax.broadcasted_iota(jnp.int32, (tq, tk), 0)
    ci = lax.broadcasted_iota(jnp.int32, (tq, tk), 1)

    def win_bias(kt):
        return jnp.where((ci > ri) | (kt != qt - n_back), 0.0, NEG)

    o_w = branch(q4, lambda kt: _cols(kwt_ref, kt * tk, tk), vwt_ref, jnp.maximum(qt - n_back, 0), win_bias)

    gs = jax.nn.sigmoid(gate_ref[...])
    n_h = NSA_KV_HEADS * R
    outs = []
    for r in range(R):
        o_r = jnp.zeros((tq, LANES), F32)
        for br, o_b in enumerate((o_c, o_s, o_w)):
            gcol = jnp.sum(jnp.where(lane == br * n_h + g * R + r, gs, 0.0), axis=-1, keepdims=True)
            o_r = o_r + gcol * o_b[r * tq:(r + 1) * tq]
        outs.append(jnp.where(par == r % 2, o_r, pltpu.roll(o_r, HEAD_DIM, 1)))
    for u in range(R // 2):
        o_ref[:, u * LANES:(u + 1) * LANES] = jnp.where(lane < HEAD_DIM, outs[2 * u], outs[2 * u + 1])


def _overlap_t(n_blk_rows, n_cmp_cols, n_cmp, n_slc):
    c0 = np.arange(n_cmp_cols)[None, :] * CMP_STRIDE
    j0 = np.arange(n_blk_rows)[:, None] * SEL_BLOCK
    ov = (c0 < j0 + SEL_BLOCK) & (c0 + CMP_LEN > j0)
    ov &= (np.arange(n_cmp_cols)[None, :] < n_cmp) & (np.arange(n_blk_rows)[:, None] < n_slc)
    return jnp.asarray(ov, BF16)


def nsa_prompt_attention(q, kvt, wint, cmp, gates, rel_bias, tq=512):
    B, T, W = q.shape
    G, R = NSA_KV_HEADS, NSA_HPG
    n_cmp = (T - CMP_LEN) // CMP_STRIDE + 1
    n_c = cmp.shape[1]
    n_slc = -(-T // SEL_BLOCK)
    n_blk = -(-n_slc // 8) * 8
    assert WINDOW % tq == 0 and n_blk <= LANES
    toe = _toeplitz_rel(rel_bias, tq, tq, 2)
    c31 = rel_bias[REL_BUCKETS - 1].astype(F32)
    n_near = CMP_NEAR_BACK + (tq - CMP_LEN) // CMP_STRIDE + 1
    assert n_near <= LANES and tq % CMP_STRIDE == 0
    dist = np.arange(tq)[:, None] - (CMP_LEN - 1) - CMP_STRIDE * (np.arange(LANES)[None, :] - CMP_NEAR_BACK)
    pat = (_bias_rows(rel_bias, dist) - c31[:, None, None]) * jnp.asarray((np.arange(LANES) < n_near) * LOG2E, F32)
    ovt = _overlap_t(n_blk, n_c, n_cmp, n_slc)
    once = pl.Buffered(1)
    pair_rows = lambda base: pl.BlockSpec((None, LANES, T), lambda b, g, i: (b, base + g // 2, 0), pipeline_mode=once)
    pair_cols = lambda base: pl.BlockSpec((None, n_c, LANES), lambda b, g, i: (b, 0, base + g // 2))
    return pl.pallas_call(
        functools.partial(_nsa_prompt_kernel, tq=tq, n_cmp=n_cmp),
        grid=(B, G, T // tq),
        in_specs=[
            pl.BlockSpec((None, tq, R * HEAD_DIM), lambda b, g, i: (b, i, g)),
            pair_rows(4), pair_rows(6), pair_rows(0), pair_rows(2),
            pair_cols(0), pair_cols(2),
            pl.BlockSpec((None, tq, LANES), lambda b, g, i: (b, i, 0)),
            pl.BlockSpec((R, tq, LANES), lambda b, g, i: (g, 0, 0), pipeline_mode=once),
            pl.BlockSpec((R, 2, tq, tq), lambda b, g, i: (g, 0, 0, 0), pipeline_mode=once),
            pl.BlockSpec(ovt.shape, lambda b, g, i: (0, 0)),
        ],
        out_specs=pl.BlockSpec((None, tq, R * HEAD_DIM), lambda b, g, i: (b, i, g)),
        out_shape=jax.ShapeDtypeStruct((B, T, W), F32),
        scratch_shapes=[pltpu.VMEM((n_blk, tq), F32)],
        compiler_params=_cparams("parallel", "parallel", "arbitrary"),
        name="nsa_prompt",
    )(q, kvt, kvt, wint, wint, cmp, cmp, gates, pat, toe, ovt)


def _nsa_decode_kernel(pt_ref, q_ref, kvn_ref, wn_ref, gate_ref, cmp_ref, win_ref, bc_ref, bs_ref, bw_ref, ovt_ref,
                       *rest, n_pages, page, n_cmp):
    pages = rest[:n_pages]
    o_ref, wout_ref, s_scr = rest[n_pages], rest[n_pages + 1], rest[n_pages + 2]
    G, R = NSA_KV_HEADS, NSA_HPG
    nq = q_ref.shape[0]
    n_h = G * R
    rows = n_h * nq
    GW = G * HEAD_DIM
    P = n_pages * page
    n_win = win_ref.shape[-1]
    lane8 = lax.broadcasted_iota(jnp.int32, (nq, LANES), 1)

    q = q_ref[...] * HEAD_DIM ** -0.5
    blocks = []
    for h in range(n_h):
        g = h // R
        t = q[:, (h // 2) * LANES:(h // 2 + 1) * LANES]
        if h % 2 != g % 2:
            t = pltpu.roll(t, HEAD_DIM, 1)
        t = jnp.where((lane8 // HEAD_DIM) == g % 2, t, 0.0)
        z = jnp.zeros((nq, LANES), F32)
        blocks.append(jnp.concatenate([t, z] if g // 2 == 0 else [z, t], axis=1))
    qg = jnp.concatenate(blocks, axis=0).astype(BF16)

    ri = lax.broadcasted_iota(jnp.int32, (rows, page), 0) % nq
    ci = lax.broadcasted_iota(jnp.int32, (rows, page), 1)
    new_mask = ci <= ri

    n_c = cmp_ref.shape[0]
    cmpv = cmp_ref[...]
    cc = lax.broadcasted_iota(jnp.int32, (rows, n_c), 1)
    s_c = _nt_dot(qg, cmpv[:, :GW].astype(BF16)) + bc_ref[...]
    p_c = _masked_softmax(s_c, cc < n_cmp)
    o_c = jnp.dot(p_c.astype(BF16), cmpv[:, GW:].astype(BF16), preferred_element_type=F32)
    ps = []
    for g in range(G):
        acc = p_c[g * R * nq:(g * R + 1) * nq]
        for r in range(1, R):
            acc = acc + p_c[(g * R + r) * nq:(g * R + r + 1) * nq]
        ps.append(acc)
    psum = _pad_rows(jnp.concatenate(ps, axis=0), LANES)
    n_blk = ovt_ref.shape[0]
    imp = _importance_t(ovt_ref[...], psum)
    jj = lax.broadcasted_iota(jnp.int32, (n_blk, LANES), 0)
    t_pos = P + lax.broadcasted_iota(jnp.int32, (n_blk, LANES), 1) % nq
    t_blk = t_pos // SEL_BLOCK
    forced = (jj == 0) | (jj == t_blk) | (jj == t_blk - 1)
    imp = jnp.where(jj * SEL_BLOCK <= t_pos, imp + jnp.where(forced, SEL_FORCE, 0.0), -1.0)
    sel_t = _select_topk_t(imp, n_blk)
    sel = _pad_rows(sel_t, LANES).T
    sel_rows = jnp.concatenate([sel[g * nq:(g + 1) * nq] for g in range(G) for _ in range(R)], axis=0).astype(BF16)

    per_tile = page // SEL_BLOCK
    m = jnp.full((rows, 1), NEG, F32)
    for p in range(n_pages):
        kt = pages[p][0].reshape(GW, page).astype(BF16)
        msk = jnp.dot(sel_rows, _block_expand(p * per_tile, LANES, page), preferred_element_type=F32) > 0.5
        s = jnp.where(msk, jnp.dot(qg, kt, preferred_element_type=F32) + bs_ref[:, p * page:(p + 1) * page], NEG)
        s_scr[:, p * page:(p + 1) * page] = s
        m = jnp.maximum(m, jnp.max(s, axis=-1, keepdims=True))
    kvn = _pad_rows(kvn_ref[...], page)
    msk = (jnp.dot(sel_rows, _block_expand(n_pages * per_tile, LANES, page), preferred_element_type=F32) > 0.5) & new_mask
    sn = jnp.where(msk, _nt_dot(qg, kvn[:, 2 * GW:3 * GW].astype(BF16)) + bs_ref[:, P:], NEG)
    m = jnp.maximum(m, jnp.max(sn, axis=-1, keepdims=True))
    pn = jnp.where(msk, jnp.exp(sn - m), 0.0)
    l = jnp.sum(pn, axis=-1, keepdims=True)
    acc = jnp.dot(pn.astype(BF16), kvn[:, 3 * GW:].astype(BF16), preferred_element_type=F32)
    for p in range(n_pages):
        pr = jnp.exp(s_scr[:, p * page:(p + 1) * page] - m)
        l = l + jnp.sum(pr, axis=-1, keepdims=True)
        acc = acc + _nt_dot(pr.astype(BF16), pages[p][1].reshape(GW, page).astype(BF16))
    o_s = acc / l

    kwt = win_ref[0].reshape(GW, n_win)
    vwt = win_ref[1].reshape(GW, n_win)
    rw = lax.broadcasted_iota(jnp.int32, (rows, n_win), 0) % nq
    cw = lax.broadcasted_iota(jnp.int32, (rows, n_win), 1)
    mask_w = (n_win + rw - cw) < WINDOW
    s_w = jnp.where(mask_w, jnp.dot(qg, kwt.astype(BF16), preferred_element_type=F32) + bw_ref[:, :n_win], NEG)
    wn = _pad_rows(wn_ref[...], page)
    s_n = jnp.where(new_mask, _nt_dot(qg, wn[:, :GW].astype(BF16)) + bw_ref[:, n_win:], NEG)
    m = jnp.maximum(jnp.max(s_w, axis=-1, keepdims=True), jnp.max(s_n, axis=-1, keepdims=True))
    p_w = jnp.where(mask_w, jnp.exp(s_w - m), 0.0)
    p_n = jnp.where(new_mask, jnp.exp(s_n - m), 0.0)
    l = jnp.sum(p_w, axis=-1, keepdims=True) + jnp.sum(p_n, axis=-1, keepdims=True)
    o_w = (_nt_dot(p_w.astype(BF16), vwt.astype(BF16))
           + jnp.dot(p_n.astype(BF16), wn[:, GW:].astype(BF16), preferred_element_type=F32)) / l

    gs = jax.nn.sigmoid(gate_ref[...])
    grep = jnp.concatenate([gs] * n_h, axis=0)
    glane = lax.broadcasted_iota(jnp.int32, (rows, LANES), 1)
    ghead = lax.broadcasted_iota(jnp.int32, (rows, LANES), 0) // nq
    o = jnp.zeros((rows, GW), F32)
    for br, o_b in enumerate((o_c, o_s, o_w)):
        o = o + jnp.sum(jnp.where(glane == br * n_h + ghead, grep, 0.0), axis=-1, keepdims=True) * o_b
    pieces = []
    for h in range(n_h):
        g = h // R
        t = o[h * nq:(h + 1) * nq, (g // 2) * LANES:(g // 2 + 1) * LANES]
        pieces.append(t if h % 2 == g % 2 else pltpu.roll(t, HEAD_DIM, 1))
    o_ref[...] = jnp.concatenate(
        [jnp.where(lane8 < HEAD_DIM, pieces[2 * u], pieces[2 * u + 1]) for u in range(n_h // 2)], axis=1)

    wt = win_ref[...].reshape(2 * GW, n_win)
    wnt = wn.T
    wout_ref[...] = jnp.concatenate([wt[:, nq:], wnt[:, :nq]], axis=1).reshape(wout_ref.shape)


def nsa_decode_attention(q, kvn, wn, gates, cmp, win_t, layer, pool_t, page_table, rel_bias):
    S, nq, W = q.shape
    G, R = NSA_KV_HEADS, NSA_HPG
    n_pages = page_table.shape[1]
    page = pool_t.shape[-1]
    P = n_pages * page
    n_win = win_t.shape[-1]
    assert nq < CMP_STRIDE and n_win == WINDOW and nq <= 8
    L = P + nq
    n_cmp = (L - CMP_LEN) // CMP_STRIDE + 1
    n_c = cmp.shape[1]
    n_slc = -(-L // SEL_BLOCK)
    n_blk = -(-n_slc // 8) * 8
    heads = range(G * R)
    qpos = P + np.arange(nq)
    bias_c = _dense_bias(rel_bias, qpos, np.arange(n_c) * CMP_STRIDE + CMP_LEN - 1, heads)
    bias_s = _dense_bias(rel_bias, qpos, np.arange(P + page), heads)
    bias_w = _dense_bias(rel_bias, qpos, P - n_win + np.arange(n_win + page), heads)
    ovt = _overlap_t(n_blk, n_c, n_cmp, n_slc)
    rows = G * R * nq
    const = lambda a: pl.BlockSpec(a.shape, lambda b, pt: (0,) * a.ndim)
    seq = lambda a: pl.BlockSpec((None,) + a.shape[1:], lambda b, pt: (b,) + (0,) * (a.ndim - 1))
    win_blk = (None, None) + win_t.shape[2:]
    grid_spec = pltpu.PrefetchScalarGridSpec(
        num_scalar_prefetch=1,
        grid=(S,),
        in_specs=[seq(q), seq(kvn), seq(wn), seq(gates), seq(cmp),
                  pl.BlockSpec(win_blk, lambda b, pt: (layer, b, 0, 0, 0, 0)),
                  const(bias_c), const(bias_s), const(bias_w), const(ovt)]
        + _page_specs(n_pages, (None, None, 2) + pool_t.shape[3:], lambda pg: (layer, pg, 1, 0, 0, 0)),
        out_specs=[seq(q), pl.BlockSpec((None,) + win_t.shape[2:], lambda b, pt: (b, 0, 0, 0, 0))],
        scratch_shapes=[pltpu.VMEM((rows, P), F32)],
    )
    return pl.pallas_call(
        functools.partial(_nsa_decode_kernel, n_pages=n_pages, page=page, n_cmp=n_cmp),
        grid_spec=grid_spec,
        out_shape=[jax.ShapeDtypeStruct(q.shape, F32), jax.ShapeDtypeStruct(win_t.shape[1:], F32)],
        compiler_params=_cparams("arbitrary"),
        name="nsa_decode",
    )(page_table, q, kvn, wn, gates, cmp, win_t, bias_c, bias_s, bias_w, ovt, *([pool_t] * n_pages))


def _w_cols(w, c0, width):
    piece = w[:, c0:c0 + width]
    if width % LANES:
        piece = jnp.pad(piece, ((0, 0), (0, LANES - width % LANES)))
    return piece.astype(BF16)


def _w_rows(w, c0, width):
    return w[:, c0:c0 + width].T.astype(BF16)


def _seq_major(a, nq, S):
    return jnp.transpose(a.reshape(nq, S, -1), (1, 0, 2))


def kernel(x_prompt, x_sample, cache_nsa_kv, state_nsa_win, cache_diff_k, cache_diff_v, cache_sb_kv, page_table, rel_bias, nsa_w_in, nsa_cmp_pos, nsa_cmp_w1, nsa_cmp_w2, nsa_w_out, diff_w_in, diff_lambda, diff_norm_g, diff_w_out, sb_w_in, sb_w_out, mlp_w_up, mlp_w_down, ln_g, ln_b):
    B, T, D = x_prompt.shape
    S, nq, _ = x_sample.shape
    depth = mlp_w_up.shape[0]
    alpha = (2 * depth) ** 0.25
    G, R, dk = NSA_KV_HEADS, NSA_HPG, HEAD_DIM
    n_pool, page = cache_nsa_kv.shape[1], cache_nsa_kv.shape[2]
    Hd, H = D // (2 * dk), D // dk
    xp = x_prompt.reshape(B * T, D)
    xs = jnp.transpose(x_sample, (1, 0, 2)).reshape(nq * S, D)
    nsa_pool_t = jnp.transpose(cache_nsa_kv, (0, 1, 3, 4, 5, 2))
    nsa_win_t = jnp.transpose(state_nsa_win, (0, 1, 3, 4, 5, 2))
    diff_k_t = jnp.transpose(cache_diff_k, (0, 1, 3, 4, 5, 2))
    sb_pool_t = jnp.transpose(cache_sb_kv, (0, 1, 3, 4, 5, 2))
    res = {k: [] for k in ("nsa_kv_p", "nsa_kv_s", "nsa_win_p", "nsa_win_s", "diff_k_p", "diff_k_s",
                           "diff_v_p", "diff_v_s", "sb_kv_p", "sb_kv_s")}
    tm_p = 512

    def to_tokens(a_t, lead):
        n, _, t = a_t.shape
        nd = len(lead)
        return jnp.transpose(a_t.reshape((n,) + lead + (t,)), (0, nd + 1) + tuple(range(1, nd + 1)))

    for i in range(depth):
        kind, j = i % N_MIXERS, i // N_MIXERS
        if kind == 0:
            w = nsa_w_in[j]
            nq_c, kv_c, win_c = G * R * dk, 4 * G * dk, 2 * G * dk
            row_p = [_w_cols(w, 0, nq_c), _w_cols(w, nq_c, 2 * G * dk), _w_cols(w, nq_c + kv_c + win_c, 3 * G * R)]
            t_ws = [_w_rows(w, nq_c, kv_c), _w_rows(w, nq_c + kv_c, win_c)]
            cmp_w = _compress_weights(nsa_cmp_pos[j], nsa_cmp_w1[j], nsa_cmp_w2[j])
            q, kc_rows, gates, kvt, wint = project(xp, row_p, t_ws, B, tm_p)
            cmp_p = compress_seq(kc_rows.reshape(B, T, -1), cmp_w)
            op = nsa_prompt_attention(q.reshape(B, T, -1), kvt, wint, cmp_p, gates.reshape(B, T, -1), rel_bias)
            row_s = [row_p[0], _w_cols(w, nq_c, kv_c), _w_cols(w, nq_c + kv_c, win_c), row_p[2]]
            qs, kvs, wns, gts, kvst = project(xs, row_s, t_ws[:1], nq, S)
            cmp_s = compress_paged(nsa_pool_t, j, page_table, cmp_w)
            os_, win_s = nsa_decode_attention(_seq_major(qs, nq, S), _seq_major(kvs, nq, S), _seq_major(wns, nq, S),
                                              _seq_major(gts, nq, S), cmp_s, nsa_win_t, j, nsa_pool_t, page_table,
                                              rel_bias)
            n_keep = min(WINDOW, T)
            res["nsa_kv_p"].append(to_tokens(kvt, (4, G, dk)))
            res["nsa_kv_s"].append(jnp.transpose(kvst.reshape(nq, 4, G, dk, S), (4, 0, 1, 2, 3)))
            res["nsa_win_p"].append(to_tokens(wint[:, :, T - n_keep:], (2, G, dk)))
            res["nsa_win_s"].append(jnp.transpose(win_s, (0, 4, 1, 2, 3)))
            w_out = nsa_w_out[j]
        elif kind == 1:
            lam_init = 0.8 - 0.6 * math.exp(-0.3 * i)
            w = diff_w_in[j]
            row_ws, t_ws = [_w_cols(w, 0, D), _w_cols(w, 2 * D, D)], [_w_rows(w, D, D)]
            q, v, kt = project(xp, row_ws, t_ws, B, tm_p)
            op = diff_prompt_attention(q.reshape(B, T, D), kt, v.reshape(B, T, D), rel_bias,
                                       diff_lambda[j], diff_norm_g[j], lam_init)
            qs, vsn, ksn, kst = project(xs, row_ws + [_w_cols(w, D, D)], t_ws, nq, S)
            vsn = _seq_major(vsn, nq, S)
            os_ = diff_decode_attention(_seq_major(qs, nq, S), _seq_major(ksn, nq, S), vsn, diff_k_t[j],
                                        cache_diff_v[j].reshape(n_pool, page * Hd, 2 * dk), page_table, rel_bias,
                                        diff_lambda[j], diff_norm_g[j], lam_init)
            res["diff_k_p"].append(to_tokens(kt, (Hd, 2, dk)))
            res["diff_k_s"].append(jnp.transpose(kst.reshape(nq, Hd, 2, dk, S), (4, 0, 1, 2, 3)))
            res["diff_v_p"].append(v.reshape(B, T, Hd, 2 * dk))
            res["diff_v_s"].append(vsn.reshape(S, nq, Hd, 2 * dk))
            w_out = diff_w_out[j]
        else:
            w = sb_w_in[j]
            row_ws, t_ws = [_w_cols(w, 0, D)], [_w_rows(w, D, 2 * D)]
            q, kvt = project(xp, row_ws, t_ws, B, tm_p)
            op = sb_prompt_attention(q.reshape(B, T, D), kvt)
            qs, kvs, kvst = project(xs, row_ws + [_w_cols(w, D, 2 * D)], t_ws, nq, S)
            os_ = sb_decode_attention(_seq_major(qs, nq, S), _seq_major(kvs, nq, S), sb_pool_t[j], page_table)
            res["sb_kv_p"].append(to_tokens(kvt, (2, H, dk)))
            res["sb_kv_s"].append(jnp.transpose(kvst.reshape(nq, 2, H, dk, S), (4, 0, 1, 2, 3)))
            w_out = sb_w_out[j]
        ln = jnp.stack([ln_g[i, 0], ln_b[i, 0], ln_g[i, 1], ln_b[i, 1]])
        w_out, w_up, w_down = w_out.astype(BF16), mlp_w_up[i].astype(BF16), mlp_w_down[i].astype(BF16)
        xp = post_mixer(op.reshape(B * T, D), w_out, xp, ln, w_up, w_down, alpha)
        os_ = jnp.transpose(os_, (1, 0, 2)).reshape(nq * S, D)
        xs = post_mixer(os_, w_out, xs, ln, w_up, w_down, alpha)
    return (xp.reshape(B, T, D), _seq_major(xs, nq, S),
            jnp.stack(res["nsa_kv_p"]), jnp.stack(res["nsa_kv_s"]), jnp.stack(res["nsa_win_p"]),
            jnp.stack(res["nsa_win_s"]), jnp.stack(res["diff_k_p"]), jnp.stack(res["diff_k_s"]),
            jnp.stack(res["diff_v_p"]), jnp.stack(res["diff_v_s"]), jnp.stack(res["sb_kv_p"]),
            jnp.stack(res["sb_kv_s"]))
```

```python
import functools
import math

import numpy as np
import jax
import jax.numpy as jnp
from jax import lax
from jax.experimental import pallas as pl
from jax.experimental.pallas import tpu as pltpu

F32 = jnp.float32
BF16 = jnp.bfloat16

HEAD_DIM = 64
NSA_KV_HEADS = 4
NSA_HPG = 4
CMP_LEN = 32
CMP_STRIDE = 16
SEL_BLOCK = 64
N_SEL = 16
SEL_FORCE = 1000.0
WINDOW = 512
REL_BUCKETS = 32
REL_MAX_DIST = 128
LN_EPS = 1e-5
NEG = -1e30
N_MIXERS = 3
SB_DEAD = 104.0

LANES = 128
VMEM_LIMIT = 48 * 1024 * 1024


def _cparams(*sem):
    return pltpu.CompilerParams(dimension_semantics=sem, vmem_limit_bytes=VMEM_LIMIT)


def _t5_bucket_np(dist):
    n = np.maximum(dist, 0)
    max_exact = REL_BUCKETS // 2
    nf = np.maximum(n, max_exact).astype(np.float32)
    large = max_exact + (np.log(nf / np.float32(max_exact)) / np.float32(math.log(REL_MAX_DIST / max_exact))
                         * np.float32(REL_BUCKETS - max_exact)).astype(np.int32)
    return np.where(n < max_exact, n, np.minimum(large, REL_BUCKETS - 1)).astype(np.int32)


FAR_DIST = int(np.min(np.nonzero(_t5_bucket_np(np.arange(4 * REL_MAX_DIST)) == REL_BUCKETS - 1)[0]))


def _bias_rows(rel_bias, dist):
    idx = jnp.asarray(_t5_bucket_np(dist))[None]
    rel = rel_bias.astype(F32)
    out = jnp.zeros((rel.shape[1],) + dist.shape, F32)
    for b in range(REL_BUCKETS):
        out = jnp.where(idx == b, rel[b].reshape((-1,) + (1,) * dist.ndim), out)
    return out


def _toeplitz_bias(rel_bias, tq, tk, n_off):
    i = np.arange(tq)[None, :, None]
    j = np.arange(tk)[None, None, :]
    off = np.arange(n_off)[:, None, None]
    return _bias_rows(rel_bias, off * tk + i - j)


def _dense_bias(rel_bias, qpos, kpos, cols):
    tab = _bias_rows(rel_bias, qpos[:, None] - kpos[None, :])[np.asarray(cols)]
    return tab.reshape(len(cols) * len(qpos), len(kpos))


def _layer_norm(z, g, b):
    mu = jnp.mean(z, axis=-1, keepdims=True)
    zc = z - mu
    var = jnp.mean(zc * zc, axis=-1, keepdims=True)
    return zc * lax.rsqrt(var + LN_EPS) * g + b


def _nt_dot(a, b):
    return lax.dot_general(a, b, (((1,), (1,)), ((), ())), preferred_element_type=F32)


def _project_kernel(*refs, n_row, n_t):
    x_ref = refs[0]
    row_w, t_w = refs[1:1 + n_row], refs[1 + n_row:1 + n_row + n_t]
    row_o, t_o = refs[1 + n_row + n_t:1 + 2 * n_row + n_t], refs[1 + 2 * n_row + n_t:]
    xb = x_ref[...].astype(BF16)
    for w_ref, o_ref in zip(row_w, row_o):
        o_ref[...] = jnp.dot(xb, w_ref[...], preferred_element_type=F32)
    for w_ref, o_ref in zip(t_w, t_o):
        o_ref[...] = _nt_dot(w_ref[...], xb)


def project(x, row_ws, t_ws, n_seq, tm):
    M, K = x.shape
    T = M // n_seq
    nt = T // tm
    n_row, n_t = len(row_ws), len(t_ws)
    const = lambda w: pl.BlockSpec(w.shape, lambda b, i: (0, 0))
    return pl.pallas_call(
        functools.partial(_project_kernel, n_row=n_row, n_t=n_t),
        grid=(n_seq, nt),
        in_specs=[pl.BlockSpec((tm, K), lambda b, i: (b * nt + i, 0))] + [const(w) for w in row_ws + t_ws],
        out_specs=[pl.BlockSpec((tm, w.shape[1]), lambda b, i: (b * nt + i, 0)) for w in row_ws]
        + [pl.BlockSpec((None, w.shape[0], tm), lambda b, i: (b, 0, i)) for w in t_ws],
        out_shape=[jax.ShapeDtypeStruct((M, w.shape[1]), F32) for w in row_ws]
        + [jax.ShapeDtypeStruct((n_seq, w.shape[0], T), F32) for w in t_ws],
        compiler_params=_cparams("parallel", "parallel"),
        name="project",
    )(x, *row_ws, *t_ws)


def _post_mixer_kernel(o_ref, wo_ref, x_ref, ln_ref, wu_ref, wd_ref, y_ref, x1_scr, xb_scr, acc_scr, *, alpha):
    f = pl.program_id(1)

    @pl.when(f == 0)
    def _():
        y = jnp.dot(o_ref[...].astype(BF16), wo_ref[...], preferred_element_type=F32)
        x1 = _layer_norm(alpha * x_ref[...] + y, ln_ref[0:1, :], ln_ref[1:2, :])
        x1_scr[...] = x1
        xb_scr[...] = x1.astype(BF16)
        acc_scr[...] = jnp.zeros_like(acc_scr)

    h = jnp.dot(xb_scr[...], wu_ref[...], preferred_element_type=F32)
    h = jnp.square(jnp.maximum(h, 0.0))
    acc_scr[...] += jnp.dot(h.astype(BF16), wd_ref[...], preferred_element_type=F32)

    @pl.when(f == pl.num_programs(1) - 1)
    def _():
        y_ref[...] = _layer_norm(alpha * x1_scr[...] + acc_scr[...], ln_ref[2:3, :], ln_ref[3:4, :])


def post_mixer(o, w_out, x, ln, w_up, w_down, alpha, tm=512, tf=1024):
    M, D = x.shape
    Fd = w_up.shape[1]
    return pl.pallas_call(
        functools.partial(_post_mixer_kernel, alpha=alpha),
        grid=(M // tm, Fd // tf),
        in_specs=[
            pl.BlockSpec((tm, D), lambda i, f: (i, 0)),
            pl.BlockSpec((D, D), lambda i, f: (0, 0)),
            pl.BlockSpec((tm, D), lambda i, f: (i, 0)),
            pl.BlockSpec((4, D), lambda i, f: (0, 0)),
            pl.BlockSpec((D, tf), lambda i, f: (0, f)),
            pl.BlockSpec((tf, D), lambda i, f: (f, 0)),
        ],
        out_specs=pl.BlockSpec((tm, D), lambda i, f: (i, 0)),
        out_shape=jax.ShapeDtypeStruct((M, D), F32),
        scratch_shapes=[pltpu.VMEM((tm, D), F32), pltpu.VMEM((tm, D), BF16), pltpu.VMEM((tm, D), F32)],
        compiler_params=_cparams("parallel", "arbitrary"),
        name="post_mixer",
    )(o, w_out, x, ln, w_up, w_down)


LOG2E = math.log2(math.e)


def _flash_step(carry, s, pv):
    m, l, acc = carry
    m_new = jnp.maximum(m, jnp.max(s, axis=-1, keepdims=True))
    alpha = jnp.exp2(m - m_new)
    p = jnp.exp2(s - m_new)
    if l is not None:
        l = alpha * l + jnp.sum(p, axis=-1, keepdims=True)
    acc = alpha * acc + pv(p.astype(BF16))
    return m_new, l, acc


def _flash_init(rows, width, with_l=True):
    return jnp.full((rows, 1), NEG, F32), jnp.zeros((rows, 1), F32) if with_l else None, jnp.zeros((rows, width), F32)


def _split_heads(q, tq):
    lane = lax.broadcasted_iota(jnp.int32, (tq, LANES), 1)
    lo = jnp.where(lane < HEAD_DIM, q, 0.0)
    hi = jnp.where(lane >= HEAD_DIM, q, 0.0)
    return jnp.concatenate([lo, hi], axis=0).astype(BF16)


def _cols(ref, start, width):
    return ref[:, pl.ds(pl.multiple_of(start, width), width)].astype(BF16)


def _rows(ref, start, height):
    return ref[pl.ds(pl.multiple_of(start, height), height), :].astype(BF16)


def _rows_blockdiag(q, n_blk, blk_w):
    rep = jnp.concatenate([q] * n_blk, axis=0)
    r = lax.broadcasted_iota(jnp.int32, rep.shape, 0) // q.shape[0]
    c = lax.broadcasted_iota(jnp.int32, rep.shape, 1) // blk_w
    return jnp.where(r == c, rep, 0.0).astype(BF16)


def _pad_rows(a, rows):
    return jnp.concatenate([a, jnp.zeros((rows - a.shape[0], a.shape[1]), a.dtype)], axis=0)


def _page_specs(n_pages, block, index):
    return [pl.BlockSpec(block, functools.partial(lambda b, pt, p: index(pt[b, p]), p=p)) for p in range(n_pages)]


def _diff_lambda(lam_ref, lam_init):
    lf = lam_ref[...]
    a = jnp.sum(lf[0:1, :] * lf[1:2, :], axis=-1, keepdims=True)
    b = jnp.sum(lf[2:3, :] * lf[3:4, :], axis=-1, keepdims=True)
    return jnp.exp(a) - jnp.exp(b) + lam_init


def _diff_finish(o1, o2, lam_full, g, lam_init):
    o = o1 - lam_full * o2
    o = o * lax.rsqrt(jnp.mean(o * o, axis=-1, keepdims=True) + LN_EPS) * g
    return o * (1.0 - lam_init)


def _causal_bias(tq, tk):
    ri = lax.broadcasted_iota(jnp.int32, (tq, tk), 0)
    ci = lax.broadcasted_iota(jnp.int32, (tq, tk), 1)
    return jnp.where(ci <= ri, 0.0, NEG)


def _add_per_head(s, b, n_blocks):
    tq, tk = b.shape
    return (s.reshape(n_blocks, tq, tk) + b[None]).reshape(n_blocks * tq, tk)


def _diff_prompt_kernel(q_ref, kt_ref, v_ref, toe_ref, lam_ref, g_ref, o_ref, *, tq, lam_init):
    qt = pl.program_id(2)
    tk = tq
    rows = 2 * tq
    q2 = _split_heads(q_ref[...] * (HEAD_DIM ** -0.5 * LOG2E), tq)

    def qk(kt):
        return jnp.dot(q2, _cols(kt_ref, kt * tk, tk), preferred_element_type=F32)

    def step(kt, carry, bias):
        s = qk(kt)
        if bias is not None:
            s = s + bias
        v = _rows(v_ref, kt * tk, tk)
        return _flash_step(carry, s, lambda p: jnp.dot(p, v, preferred_element_type=F32))

    near0 = jnp.maximum(qt - 1, 0)
    carry = lax.fori_loop(0, near0, lambda kt, c: step(kt, c, None), _flash_init(rows, LANES))
    carry = lax.fori_loop(near0, qt, lambda kt, c: step(kt, c, toe_ref[1]), carry)
    _, l, acc = step(qt, carry, _add_per_head(toe_ref[0], _causal_bias(tq, tk), 2))
    o = acc / l
    o_ref[...] = _diff_finish(o[:tq], o[tq:], _diff_lambda(lam_ref, lam_init), g_ref[...], lam_init)


def _toeplitz_rel(rel_bias, tq, tk, n_off):
    far = rel_bias[REL_BUCKETS - 1].astype(F32)[:, None, None, None]
    return (_toeplitz_bias(rel_bias, tq, tk, n_off) - far) * LOG2E


def diff_prompt_attention(q, kt, v, rel_bias, lam, norm_g, lam_init, tq=512):
    B, T, W = q.shape
    n_heads = W // LANES
    toe = _toeplitz_rel(rel_bias, tq, tq, 2)
    toe = jnp.concatenate([toe[:n_heads], toe[n_heads:]], axis=2)
    return pl.pallas_call(
        functools.partial(_diff_prompt_kernel, tq=tq, lam_init=lam_init),
        grid=(B, n_heads, T // tq),
        in_specs=[
            pl.BlockSpec((None, tq, LANES), lambda b, h, i: (b, i, h)),
            pl.BlockSpec((None, LANES, T), lambda b, h, i: (b, h, 0)),
            pl.BlockSpec((None, T, LANES), lambda b, h, i: (b, 0, h)),
            pl.BlockSpec((None, 2, 2 * tq, tq), lambda b, h, i: (h, 0, 0, 0)),
            pl.BlockSpec((4, HEAD_DIM), lambda b, h, i: (0, 0)),
            pl.BlockSpec((1, LANES), lambda b, h, i: (0, 0)),
        ],
        out_specs=pl.BlockSpec((None, tq, LANES), lambda b, h, i: (b, i, h)),
        out_shape=jax.ShapeDtypeStruct((B, T, W), F32),
        compiler_params=_cparams("parallel", "parallel", "arbitrary"),
        name="diff_prompt",
    )(q, kt, v, toe, lam, norm_g.reshape(1, LANES))


def _diff_decode_kernel(pt_ref, q_ref, kn_ref, vn_ref, bias_ref, lam_ref, g_ref, *rest, n_pages, page, lam_init):
    kp, vp = rest[:n_pages], rest[n_pages:2 * n_pages]
    o_ref, s_scr = rest[2 * n_pages], rest[2 * n_pages + 1]
    nq, W = q_ref.shape
    n_heads = W // LANES
    hr = 2 * nq
    rows = n_heads * hr
    qbd = _rows_blockdiag(q_ref[...] * HEAD_DIM ** -0.5, 2 * n_heads, HEAD_DIM)

    hblk = lax.broadcasted_iota(jnp.int32, (rows, page), 0) // hr

    def pv(p, v_of_head):
        p_bd = jnp.concatenate([jnp.where(hblk == h, p, jnp.zeros_like(p)) for h in range(n_heads)], axis=1)
        v_hk = jnp.concatenate([v_of_head(h) for h in range(n_heads)], axis=0)
        return jnp.dot(p_bd, v_hk, preferred_element_type=F32)

    m = jnp.full((rows, 1), NEG, F32)
    for p in range(n_pages):
        kt = kp[p][...].reshape(W, page).astype(BF16)
        s = jnp.dot(qbd, kt, preferred_element_type=F32) + bias_ref[:, p * page:(p + 1) * page]
        s_scr[:, p * page:(p + 1) * page] = s
        m = jnp.maximum(m, jnp.max(s, axis=-1, keepdims=True))
    ri = lax.broadcasted_iota(jnp.int32, (rows, page), 0) % nq
    ci = lax.broadcasted_iota(jnp.int32, (rows, page), 1)
    mask = ci <= ri
    kn = _pad_rows(kn_ref[...], page).astype(BF16)
    vn = _pad_rows(vn_ref[...], page).astype(BF16)
    sn = jnp.where(mask, _nt_dot(qbd, kn) + bias_ref[:, n_pages * page:], NEG)
    m = jnp.maximum(m, jnp.max(sn, axis=-1, keepdims=True))
    pn = jnp.where(mask, jnp.exp(sn - m), 0.0)
    l = jnp.sum(pn, axis=-1, keepdims=True)
    acc = pv(pn.astype(BF16), lambda h: vn[:, h * LANES:(h + 1) * LANES])
    for p in range(n_pages):
        pr = jnp.exp(s_scr[:, p * page:(p + 1) * page] - m)
        l = l + jnp.sum(pr, axis=-1, keepdims=True)
        acc = acc + pv(pr.astype(BF16), lambda h: vp[p][pl.ds(h, page, stride=n_heads), :].astype(BF16))
    o = acc / l
    lam_full = _diff_lambda(lam_ref, lam_init)
    outs = [_diff_finish(o[h * hr:h * hr + nq], o[h * hr + nq:(h + 1) * hr], lam_full, g_ref[...], lam_init)
            for h in range(n_heads)]
    o_ref[...] = jnp.concatenate(outs, axis=1)


def diff_decode_attention(q, kn, vn, kt_pool, v_pool, page_table, rel_bias, lam, norm_g, lam_init):
    S, nq, W = q.shape
    n_pages = page_table.shape[1]
    page = kt_pool.shape[-1]
    n_heads = W // LANES
    P = n_pages * page
    cols = [mp * n_heads + h for h in range(n_heads) for mp in range(2)]
    bias = _dense_bias(rel_bias, P + np.arange(nq), np.arange(P + page), cols)
    rows = bias.shape[0]
    seq = pl.BlockSpec((None, nq, W), lambda b, pt: (b, 0, 0))
    grid_spec = pltpu.PrefetchScalarGridSpec(
        num_scalar_prefetch=1,
        grid=(S,),
        in_specs=[seq, seq, seq,
                  pl.BlockSpec(bias.shape, lambda b, pt: (0, 0)),
                  pl.BlockSpec((4, HEAD_DIM), lambda b, pt: (0, 0)),
                  pl.BlockSpec((1, LANES), lambda b, pt: (0, 0))]
        + _page_specs(n_pages, (None,) + kt_pool.shape[1:], lambda pg: (pg, 0, 0, 0, 0))
        + _page_specs(n_pages, (None,) + v_pool.shape[1:], lambda pg: (pg, 0, 0)),
        out_specs=seq,
        scratch_shapes=[pltpu.VMEM((rows, P), F32)],
    )
    return pl.pallas_call(
        functools.partial(_diff_decode_kernel, n_pages=n_pages, page=page, lam_init=lam_init),
        grid_spec=grid_spec,
        out_shape=jax.ShapeDtypeStruct((S, nq, W), F32),
        compiler_params=_cparams("arbitrary"),
        name="diff_decode",
    )(page_table, q, kn, vn, bias, lam, norm_g.reshape(1, LANES), *([kt_pool] * n_pages), *([v_pool] * n_pages))


def _softplus(z):
    return jnp.maximum(z, 0.0) + jnp.log(1.0 + jnp.exp(-jnp.abs(z)))


def _suffix_sum(c, u):
    hi = c.astype(BF16)
    lo = (c - hi.astype(F32)).astype(BF16)
    return jnp.dot(hi, u, preferred_element_type=F32) + jnp.dot(lo, u, preferred_element_type=F32)


def _sb_local(z, u, mask=None):
    c = _softplus(z)
    if mask is not None:
        c = jnp.where(mask, c, 0.0)
    return z - c - _suffix_sum(c, u), jnp.sum(c, axis=-1, keepdims=True)


def _sb_weights(e, r, mask=None):
    a = jnp.exp(e - r)
    if mask is not None:
        a = jnp.where(mask, a, 0.0)
    return a.astype(BF16)


def _strict_upper(n):
    return jnp.asarray(np.arange(n)[:, None] > np.arange(n)[None, :], BF16)


def _sb_prompt_kernel(q_ref, kt_ref, vt_ref, u_ref, o_ref, *, tq, tk):
    qt = pl.program_id(2)
    n_sub = tq // tk
    rows = 2 * tq
    q2 = _split_heads(q_ref[...] * HEAD_DIM ** -0.5, tq)
    u = u_ref[...]

    def block(kt, carry, diag):
        r, acc = carry
        local = []
        for sub in range(n_sub):
            mask = None
            if diag:
                ri = lax.broadcasted_iota(jnp.int32, (rows, tk), 0) % tq
                ci = lax.broadcasted_iota(jnp.int32, (rows, tk), 1) + sub * tk
                mask = ci < ri
            z = jnp.dot(q2, _cols(kt_ref, kt * tq + sub * tk, tk), preferred_element_type=F32)
            local.append(_sb_local(z, u, mask) + (mask,))
        for sub in reversed(range(n_sub)):
            e, tot, mask = local[sub]
            acc = acc + _nt_dot(_sb_weights(e, r, mask), _cols(vt_ref, kt * tq + sub * tk, tk))
            r = r + tot
        return r, acc

    r, acc = block(qt, (jnp.zeros((rows, 1), F32), jnp.zeros((rows, LANES), F32)), True)

    def cond(c):
        return jnp.logical_and(c[0] >= 0, jnp.min(c[1]) < SB_DEAD)

    def body(c):
        r, acc = block(c[0], (c[1], c[2]), False)
        return c[0] - 1, r, acc

    _, _, acc = lax.while_loop(cond, body, (qt - 1, r, acc))
    lane = lax.broadcasted_iota(jnp.int32, (tq, LANES), 1)
    o_ref[...] = jnp.where(lane < HEAD_DIM, acc[:tq], acc[tq:])


def sb_prompt_attention(q, kvt, tq=256, tk=256):
    B, T, W = q.shape
    n_pairs = W // LANES
    return pl.pallas_call(
        functools.partial(_sb_prompt_kernel, tq=tq, tk=tk),
        grid=(B, n_pairs, T // tq),
        in_specs=[
            pl.BlockSpec((None, tq, LANES), lambda b, h, i: (b, i, h)),
            pl.BlockSpec((None, LANES, T), lambda b, h, i: (b, h, 0)),
            pl.BlockSpec((None, LANES, T), lambda b, h, i: (b, n_pairs + h, 0)),
            pl.BlockSpec((tk, tk), lambda b, h, i: (0, 0)),
        ],
        out_specs=pl.BlockSpec((None, tq, LANES), lambda b, h, i: (b, i, h)),
        out_shape=jax.ShapeDtypeStruct((B, T, W), F32),
        compiler_params=_cparams("parallel", "parallel", "arbitrary"),
        name="sb_prompt",
    )(q, kvt, kvt, _strict_upper(tk))


def _sb_decode_kernel(pt_ref, q_ref, kvn_ref, u_ref, *rest, n_pages, page):
    pages, o_ref, e_scr = rest[:n_pages], rest[n_pages], rest[n_pages + 1]
    nq, W = q_ref.shape
    n_heads = W // HEAD_DIM
    rows = n_heads * nq
    u = u_ref[...]
    qbd = _rows_blockdiag(q_ref[...] * HEAD_DIM ** -0.5, n_heads, HEAD_DIM)
    kvn = _pad_rows(kvn_ref[...], page).astype(BF16)
    ri = lax.broadcasted_iota(jnp.int32, (rows, page), 0) % nq
    ci = lax.broadcasted_iota(jnp.int32, (rows, page), 1)
    new_mask = ci < ri
    e_new, r = _sb_local(_nt_dot(qbd, kvn[:, :W]), u, new_mask)
    tots = []
    for p in range(n_pages):
        kt = pages[p][0].reshape(W, page).astype(BF16)
        e, tot = _sb_local(jnp.dot(qbd, kt, preferred_element_type=F32), u)
        e_scr[:, p * page:(p + 1) * page] = e
        tots.append(tot)
    acc = jnp.dot(_sb_weights(e_new, 0.0, new_mask), kvn[:, W:], preferred_element_type=F32)
    for p in reversed(range(n_pages)):
        vt = pages[p][1].reshape(W, page).astype(BF16)
        acc = acc + _nt_dot(_sb_weights(e_scr[:, p * page:(p + 1) * page], r), vt)
        r = r + tots[p]
    col = lax.broadcasted_iota(jnp.int32, (nq, W), 1) // HEAD_DIM
    o = jnp.zeros((nq, W), F32)
    for h in range(n_heads):
        o = o + jnp.where(col == h, acc[h * nq:(h + 1) * nq, :], 0.0)
    o_ref[...] = o


def sb_decode_attention(q, kvn, kvt_pool, page_table):
    S, nq, W = q.shape
    n_pages = page_table.shape[1]
    page = kvt_pool.shape[-1]
    grid_spec = pltpu.PrefetchScalarGridSpec(
        num_scalar_prefetch=1,
        grid=(S,),
        in_specs=[pl.BlockSpec((None, nq, W), lambda b, pt: (b, 0, 0)),
                  pl.BlockSpec((None, nq, 2 * W), lambda b, pt: (b, 0, 0)),
                  pl.BlockSpec((page, page), lambda b, pt: (0, 0))]
        + _page_specs(n_pages, (None,) + kvt_pool.shape[1:], lambda pg: (pg, 0, 0, 0, 0)),
        out_specs=pl.BlockSpec((None, nq, W), lambda b, pt: (b, 0, 0)),
        scratch_shapes=[pltpu.VMEM((W // HEAD_DIM * nq, n_pages * page), F32)],
    )
    return pl.pallas_call(
        functools.partial(_sb_decode_kernel, n_pages=n_pages, page=page),
        grid_spec=grid_spec,
        out_shape=jax.ShapeDtypeStruct((S, nq, W), F32),
        compiler_params=_cparams("arbitrary"),
        name="sb_decode",
    )(page_table, q, kvn, _strict_upper(page), *([kvt_pool] * n_pages))


N_CMP_TILES = 2 * NSA_KV_HEADS * HEAD_DIM // LANES


def _gelu_tanh(x):
    return 0.5 * x * (1.0 + jnp.tanh(math.sqrt(2.0 / math.pi) * (x + 0.044715 * x * x * x)))


def _compress_body(chunk_rows, w1_ref, pos_ref, w2_ref, o_ref, n_chunks):
    n_kinds = w1_ref.shape[0]
    tiles_per_kind = NSA_KV_HEADS * HEAD_DIM // LANES
    for kind in range(n_kinds):
        xcat = jnp.concatenate(
            [jnp.concatenate([chunk_rows(l, kind * tiles_per_kind + t) for l in range(CMP_STRIDE)], axis=1)
             for t in range(tiles_per_kind)], axis=0)
        u = []
        for a in range(2):
            xa = (xcat + pos_ref[kind, a]).astype(BF16)
            u.append(jnp.dot(xa, w1_ref[kind, a], preferred_element_type=F32))
        for t in range(tiles_per_kind):
            u0 = u[0][t * n_chunks:(t + 1) * n_chunks]
            u1 = pltpu.roll(u[1][t * n_chunks:(t + 1) * n_chunks], n_chunks - 1, 0)
            hid = _gelu_tanh(u0 + u1)
            out = jnp.dot(hid.astype(BF16), w2_ref[kind], preferred_element_type=F32)
            col = (kind * tiles_per_kind + t) * LANES
            o_ref[:, col:col + LANES] = out


def _compress_seq_kernel(*refs, n_chunks):
    x_refs = refs[:N_CMP_TILES]
    w1_ref, pos_ref, w2_ref, o_ref = refs[N_CMP_TILES:]

    def chunk_rows(l, tile):
        return x_refs[tile][pl.ds(l, n_chunks, stride=CMP_STRIDE), :]
    _compress_body(chunk_rows, w1_ref, pos_ref, w2_ref, o_ref, n_chunks)


def _compress_paged_kernel(pt_ref, w1_ref, pos_ref, w2_ref, *rest, n_pages, page):
    pages, o_ref, x_scr = rest[:n_pages], rest[n_pages], rest[n_pages + 1]
    tiles_per_kind = NSA_KV_HEADS * HEAD_DIM // LANES
    for p in range(n_pages):
        for t in range(N_CMP_TILES):
            kind, pair = t // tiles_per_kind, t % tiles_per_kind
            xt = pages[p][kind, 2 * pair:2 * pair + 2].reshape(LANES, page)
            x_scr[t, p * page:(p + 1) * page, :] = xt.T
    n_chunks = n_pages * page // CMP_STRIDE

    def chunk_rows(l, tile):
        return x_scr[tile, pl.ds(l, n_chunks, stride=CMP_STRIDE), :]
    _compress_body(chunk_rows, w1_ref, pos_ref, w2_ref, o_ref, n_chunks)


def _compress_weights(cmp_pos, cmp_w1, cmp_w2):
    eye2 = jnp.eye(2, dtype=F32)
    w1r = cmp_w1.reshape(2, 2, CMP_STRIDE, HEAD_DIM, -1)
    w1e = jnp.einsum('kaldj,hg->kalhdgj', w1r, eye2)
    w1e = w1e.reshape(2, 2, CMP_STRIDE * LANES, 2 * cmp_w1.shape[-1]).astype(BF16)
    pos = jnp.broadcast_to(cmp_pos.reshape(2, 2, CMP_STRIDE, 1, HEAD_DIM), (2, 2, CMP_STRIDE, 2, HEAD_DIM))
    pos = pos.reshape(2, 2, 1, CMP_STRIDE * LANES).astype(F32)
    w2e = jnp.einsum('kjd,hg->khjgd', cmp_w2, eye2).reshape(2, 2 * cmp_w2.shape[1], LANES).astype(BF16)
    return w1e, pos, w2e


def compress_seq(kv, cmp_w):
    B, L, _ = kv.shape
    w1e, pos, w2e = cmp_w
    n_chunks = L // CMP_STRIDE
    W = N_CMP_TILES * LANES
    return pl.pallas_call(
        functools.partial(_compress_seq_kernel, n_chunks=n_chunks),
        grid=(B,),
        in_specs=[pl.BlockSpec((None, L, LANES), functools.partial(lambda b, t: (b, 0, t), t=t)) for t in range(N_CMP_TILES)]
        + [pl.BlockSpec(w1e.shape, lambda b: (0, 0, 0, 0)),
           pl.BlockSpec(pos.shape, lambda b: (0, 0, 0, 0)),
           pl.BlockSpec(w2e.shape, lambda b: (0, 0, 0))],
        out_specs=pl.BlockSpec((None, n_chunks, W), lambda b: (b, 0, 0)),
        out_shape=jax.ShapeDtypeStruct((B, n_chunks, W), F32),
        compiler_params=_cparams("parallel"),
        name="nsa_compress_seq",
    )(*([kv] * N_CMP_TILES), w1e, pos, w2e)


def compress_paged(pool_t, layer, page_table, cmp_w):
    S, n_pages = page_table.shape
    page = pool_t.shape[-1]
    w1e, pos, w2e = cmp_w
    n_chunks = n_pages * page // CMP_STRIDE
    W = N_CMP_TILES * LANES
    blk = (None, None, 2) + pool_t.shape[3:]
    grid_spec = pltpu.PrefetchScalarGridSpec(
        num_scalar_prefetch=1,
        grid=(S,),
        in_specs=[pl.BlockSpec(w1e.shape, lambda b, pt: (0, 0, 0, 0)),
                  pl.BlockSpec(pos.shape, lambda b, pt: (0, 0, 0, 0)),
                  pl.BlockSpec(w2e.shape, lambda b, pt: (0, 0, 0))]
        + _page_specs(n_pages, blk, lambda pg: (layer, pg, 0, 0, 0, 0)),
        out_specs=pl.BlockSpec((None, n_chunks, W), lambda b, pt: (b, 0, 0)),
        scratch_shapes=[pltpu.VMEM((N_CMP_TILES, n_pages * page, LANES), F32)],
    )
    return pl.pallas_call(
        functools.partial(_compress_paged_kernel, n_pages=n_pages, page=page),
        grid_spec=grid_spec,
        out_shape=jax.ShapeDtypeStruct((S, n_chunks, W), F32),
        compiler_params=_cparams("arbitrary"),
        name="nsa_compress_paged",
    )(page_table, w1e, pos, w2e, *([pool_t] * n_pages))


def _masked_softmax(s, mask, exp=jnp.exp):
    s = jnp.where(mask, s, NEG)
    m = jnp.max(s, axis=-1, keepdims=True)
    e = jnp.where(mask, exp(s - m), 0.0)
    return e * (1.0 / jnp.maximum(jnp.sum(e, axis=-1, keepdims=True), 1e-30))


def _importance_t(ov_t, psum):
    hi = psum.astype(BF16)
    lo = (psum - hi.astype(F32)).astype(BF16)
    return _nt_dot(ov_t, hi) + _nt_dot(ov_t, lo)


def _select_topk_t(imp, n_rows):
    jj = lax.broadcasted_iota(jnp.int32, imp.shape, 0)
    cnt = jnp.zeros(imp.shape, F32)
    for i in range(n_rows):
        row = imp[i:i + 1, :]
        ahead = (row > imp) | ((row == imp) & (jj > i))
        cnt = cnt + jnp.where(ahead, 1.0, 0.0)
    return jnp.where(cnt < N_SEL, 1.0, 0.0)


def _select_topk_ref(imp_ref, n_active):
    imp = imp_ref[...]
    jj = lax.broadcasted_iota(jnp.int32, imp.shape, 0)

    def body(i, cnt):
        row = imp_ref[pl.ds(i, 1), :]
        ahead = (row > imp) | ((row == imp) & (jj > i))
        return cnt + jnp.where(ahead, 1.0, 0.0)

    cnt = lax.fori_loop(0, n_active, body, jnp.zeros(imp.shape, F32))
    return jnp.where(cnt < N_SEL, 1.0, 0.0)


def _block_expand(first_block, n_blocks, tk):
    j = lax.broadcasted_iota(jnp.int32, (n_blocks, tk), 0)
    col = lax.broadcasted_iota(jnp.int32, (n_blocks, tk), 1)
    return jnp.where(j == first_block + col // SEL_BLOCK, 1.0, 0.0).astype(BF16)


CMP_NEAR_BACK = -(-(CMP_LEN - 1 + FAR_DIST) // CMP_STRIDE)


def _nsa_prompt_kernel(q_ref, kst_ref, vst_ref, kwt_ref, vwt_ref, kc_ref, vc_ref, gate_ref, pat_ref, toe_ref,
                       ovt_ref, o_ref, imp_scr, *, tq, n_cmp):
    g = pl.program_id(1)
    qt = pl.program_id(2)
    tk = tq
    R = NSA_HPG
    par = g % 2
    lane = lax.broadcasted_iota(jnp.int32, (tq, LANES), 1)
    own = (lane // HEAD_DIM) == par
    q = q_ref[...] * (HEAD_DIM ** -0.5 * LOG2E)
    parts = []
    for r in range(R):
        t = q[:, (r // 2) * LANES:(r // 2 + 1) * LANES]
        src = jnp.where(par == r % 2, t, pltpu.roll(t, HEAD_DIM, 1))
        parts.append(jnp.where(own, src, 0.0))
    q4 = jnp.concatenate(parts, axis=0).astype(BF16)
    rows = R * tq

    def toe(off):
        return toe_ref[:, off].reshape(rows, tk)

    n_c = kc_ref.shape[0]
    cc = lax.broadcasted_iota(jnp.int32, (rows, n_c), 1)
    tt = qt * tq + lax.broadcasted_iota(jnp.int32, (rows, n_c), 0) % tq
    mask_c = (cc * CMP_STRIDE + CMP_LEN - 1 <= tt) & (cc < n_cmp)
    pat = pat_ref[...].reshape(rows, LANES)
    uu = lax.broadcasted_iota(jnp.int32, (LANES, n_c), 0)
    shift = jnp.where(lax.broadcasted_iota(jnp.int32, (LANES, n_c), 1) == qt * (tq // CMP_STRIDE) - CMP_NEAR_BACK + uu,
                      1.0, 0.0).astype(BF16)
    s_c = _nt_dot(q4, kc_ref[...].astype(BF16)) + jnp.dot(pat.astype(BF16), shift, preferred_element_type=F32)
    p_c = _masked_softmax(s_c, mask_c, jnp.exp2)
    o_c = jnp.dot(p_c.astype(BF16), vc_ref[...].astype(BF16), preferred_element_type=F32)
    psum = p_c[0:tq]
    for r in range(1, R):
        psum = psum + p_c[r * tq:(r + 1) * tq]
    n_blk = ovt_ref.shape[0]
    imp = _importance_t(ovt_ref[...], psum)
    jj = lax.broadcasted_iota(jnp.int32, (n_blk, tq), 0)
    tq_pos = qt * tq + lax.broadcasted_iota(jnp.int32, (n_blk, tq), 1)
    t_blk = tq_pos // SEL_BLOCK
    forced = (jj == 0) | (jj == t_blk) | (jj == t_blk - 1)
    imp_scr[...] = jnp.where(jj * SEL_BLOCK <= tq_pos, imp + jnp.where(forced, SEL_FORCE, 0.0), -1.0)
    sel_t = _select_topk_ref(imp_scr, jnp.minimum((qt + 1) * (tq // SEL_BLOCK), n_blk))
    sel = _pad_rows(sel_t, LANES).T
    sel_neg = ((sel - 1.0) * -NEG).astype(BF16)

    own_rows = (lax.broadcasted_iota(jnp.int32, (LANES, 1), 0) // HEAD_DIM) == par

    def branch(lhs, rhs, v_ref, first, tile_bias):
        def qk(kt):
            return jnp.dot(lhs, rhs(kt), preferred_element_type=F32)

        def finish_tile(kt, carry, s, toe_off, extra):
            b = tile_bias(kt)
            if extra is not None:
                b = extra if b is None else b + extra
            if toe_off is not None:
                s = s + toe(toe_off)
            if b is not None:
                s = _add_per_head(s, b, R)
            vt = jnp.where(own_rows, _cols(v_ref, kt * tk, tk), 1.0)
            return _flash_step(carry, s, lambda p: _nt_dot(p, vt))

        def step(kt, carry, toe_off):
            return finish_tile(kt, carry, qk(kt), toe_off, None)

        carry = _flash_init(rows, LANES, with_l=False)
        carry = lax.fori_loop(first, near0, lambda kt, c: step(kt, c, None), carry)
        carry = lax.fori_loop(jnp.maximum(near0, first), qt, lambda kt, c: step(kt, c, 1), carry)
        _, _, acc = finish_tile(qt, carry, qk(qt), 0, _causal_bias(tq, tk))
        return acc / pltpu.roll(acc, HEAD_DIM, 1)

    near0 = jnp.maximum(qt - 1, 0)

    sel_lhs = jnp.concatenate([q4, jnp.concatenate([sel_neg] * R, axis=0)], axis=1)

    def sel_rhs(kt):
        return jnp.concatenate([_cols(kst_ref, kt * tk, tk), _block_expand(kt * (tk // SEL_BLOCK), LANES, tk)], axis=0)

    o_s = branch(sel_lhs, sel_rhs, vst_ref, 0, lambda kt: None)

    n_back = WINDOW // tk
    ri = lax.broadcasted_iota(jnp.int32, (tq, tk), 0)
    ci = lax.broadcasted_iota(jnp.int32, (tq, tk), 1)

    def win_bias(kt):
        return jnp.where((ci > ri) | (kt != qt - n_back), 0.0, NEG)

    o_w = branch(q4, lambda kt: _cols(kwt_ref, kt * tk, tk), vwt_ref, jnp.maximum(qt - n_back, 0), win_bias)

    gs = jax.nn.sigmoid(gate_ref[...])
    n_h = NSA_KV_HEADS * R
    outs = []
    for r in range(R):
        o_r = jnp.zeros((tq, LANES), F32)
        for br, o_b in enumerate((o_c, o_s, o_w)):
            gcol = jnp.sum(jnp.where(lane == br * n_h + g * R + r, gs, 0.0), axis=-1, keepdims=True)
            o_r = o_r + gcol * o_b[r * tq:(r + 1) * tq]
        outs.append(jnp.where(par == r % 2, o_r, pltpu.roll(o_r, HEAD_DIM, 1)))
    for u in range(R // 2):
        o_ref[:, u * LANES:(u + 1) * LANES] = jnp.where(lane < HEAD_DIM, outs[2 * u], outs[2 * u + 1])


def _overlap_t(n_blk_rows, n_cmp_cols, n_cmp, n_slc):
    c0 = np.arange(n_cmp_cols)[None, :] * CMP_STRIDE
    j0 = np.arange(n_blk_rows)[:, None] * SEL_BLOCK
    ov = (c0 < j0 + SEL_BLOCK) & (c0 + CMP_LEN > j0)
    ov &= (np.arange(n_cmp_cols)[None, :] < n_cmp) & (np.arange(n_blk_rows)[:, None] < n_slc)
    return jnp.asarray(ov, BF16)


def nsa_prompt_attention(q, kvt, wint, cmp, gates, rel_bias, tq=512):
    B, T, W = q.shape
    G, R = NSA_KV_HEADS, NSA_HPG
    n_cmp = (T - CMP_LEN) // CMP_STRIDE + 1
    n_c = cmp.shape[1]
    n_slc = -(-T // SEL_BLOCK)
    n_blk = -(-n_slc // 8) * 8
    assert WINDOW % tq == 0 and n_blk <= LANES
    toe = _toeplitz_rel(rel_bias, tq, tq, 2)
    c31 = rel_bias[REL_BUCKETS - 1].astype(F32)
    n_near = CMP_NEAR_BACK + (tq - CMP_LEN) // CMP_STRIDE + 1
    assert n_near <= LANES and tq % CMP_STRIDE == 0
    dist = np.arange(tq)[:, None] - (CMP_LEN - 1) - CMP_STRIDE * (np.arange(LANES)[None, :] - CMP_NEAR_BACK)
    pat = (_bias_rows(rel_bias, dist) - c31[:, None, None]) * jnp.asarray((np.arange(LANES) < n_near) * LOG2E, F32)
    ovt = _overlap_t(n_blk, n_c, n_cmp, n_slc)
    once = pl.Buffered(1)
    pair_rows = lambda base: pl.BlockSpec((None, LANES, T), lambda b, g, i: (b, base + g // 2, 0), pipeline_mode=once)
    pair_cols = lambda base: pl.BlockSpec((None, n_c, LANES), lambda b, g, i: (b, 0, base + g // 2))
    return pl.pallas_call(
        functools.partial(_nsa_prompt_kernel, tq=tq, n_cmp=n_cmp),
        grid=(B, G, T // tq),
        in_specs=[
            pl.BlockSpec((None, tq, R * HEAD_DIM), lambda b, g, i: (b, i, g)),
            pair_rows(4), pair_rows(6), pair_rows(0), pair_rows(2),
            pair_cols(0), pair_cols(2),
            pl.BlockSpec((None, tq, LANES), lambda b, g, i: (b, i, 0)),
            pl.BlockSpec((R, tq, LANES), lambda b, g, i: (g, 0, 0), pipeline_mode=once),
            pl.BlockSpec((R, 2, tq, tq), lambda b, g, i: (g, 0, 0, 0), pipeline_mode=once),
            pl.BlockSpec(ovt.shape, lambda b, g, i: (0, 0)),
        ],
        out_specs=pl.BlockSpec((None, tq, R * HEAD_DIM), lambda b, g, i: (b, i, g)),
        out_shape=jax.ShapeDtypeStruct((B, T, W), F32),
        scratch_shapes=[pltpu.VMEM((n_blk, tq), F32)],
        compiler_params=_cparams("parallel", "parallel", "arbitrary"),
        name="nsa_prompt",
    )(q, kvt, kvt, wint, wint, cmp, cmp, gates, pat, toe, ovt)


def _nsa_decode_kernel(pt_ref, q_ref, kvn_ref, wn_ref, gate_ref, cmp_ref, win_ref, bc_ref, bs_ref, bw_ref, ovt_ref,
                       *rest, n_pages, page, n_cmp):
    pages = rest[:n_pages]
    o_ref, wout_ref, s_scr = rest[n_pages], rest[n_pages + 1], rest[n_pages + 2]
    G, R = NSA_KV_HEADS, NSA_HPG
    nq = q_ref.shape[0]
    n_h = G * R
    rows = n_h * nq
    GW = G * HEAD_DIM
    P = n_pages * page
    n_win = win_ref.shape[-1]
    lane8 = lax.broadcasted_iota(jnp.int32, (nq, LANES), 1)

    q = q_ref[...] * HEAD_DIM ** -0.5
    blocks = []
    for h in range(n_h):
        g = h // R
        t = q[:, (h // 2) * LANES:(h // 2 + 1) * LANES]
        if h % 2 != g % 2:
            t = pltpu.roll(t, HEAD_DIM, 1)
        t = jnp.where((lane8 // HEAD_DIM) == g % 2, t, 0.0)
        z = jnp.zeros((nq, LANES), F32)
        blocks.append(jnp.concatenate([t, z] if g // 2 == 0 else [z, t], axis=1))
    qg = jnp.concatenate(blocks, axis=0).astype(BF16)

    ri = lax.broadcasted_iota(jnp.int32, (rows, page), 0) % nq
    ci = lax.broadcasted_iota(jnp.int32, (rows, page), 1)
    new_mask = ci <= ri

    n_c = cmp_ref.shape[0]
    cmpv = cmp_ref[...]
    cc = lax.broadcasted_iota(jnp.int32, (rows, n_c), 1)
    s_c = _nt_dot(qg, cmpv[:, :GW].astype(BF16)) + bc_ref[...]
    p_c = _masked_softmax(s_c, cc < n_cmp)
    o_c = jnp.dot(p_c.astype(BF16), cmpv[:, GW:].astype(BF16), preferred_element_type=F32)
    ps = []
    for g in range(G):
        acc = p_c[g * R * nq:(g * R + 1) * nq]
        for r in range(1, R):
            acc = acc + p_c[(g * R + r) * nq:(g * R + r + 1) * nq]
        ps.append(acc)
    psum = _pad_rows(jnp.concatenate(ps, axis=0), LANES)
    n_blk = ovt_ref.shape[0]
    imp = _importance_t(ovt_ref[...], psum)
    jj = lax.broadcasted_iota(jnp.int32, (n_blk, LANES), 0)
    t_pos = P + lax.broadcasted_iota(jnp.int32, (n_blk, LANES), 1) % nq
    t_blk = t_pos // SEL_BLOCK
    forced = (jj == 0) | (jj == t_blk) | (jj == t_blk - 1)
    imp = jnp.where(jj * SEL_BLOCK <= t_pos, imp + jnp.where(forced, SEL_FORCE, 0.0), -1.0)
    sel_t = _select_topk_t(imp, n_blk)
    sel = _pad_rows(sel_t, LANES).T
    sel_rows = jnp.concatenate([sel[g * nq:(g + 1) * nq] for g in range(G) for _ in range(R)], axis=0).astype(BF16)

    per_tile = page // SEL_BLOCK
    m = jnp.full((rows, 1), NEG, F32)
    for p in range(n_pages):
        kt = pages[p][0].reshape(GW, page).astype(BF16)
        msk = jnp.dot(sel_rows, _block_expand(p * per_tile, LANES, page), preferred_element_type=F32) > 0.5
        s = jnp.where(msk, jnp.dot(qg, kt, preferred_element_type=F32) + bs_ref[:, p * page:(p + 1) * page], NEG)
        s_scr[:, p * page:(p + 1) * page] = s
        m = jnp.maximum(m, jnp.max(s, axis=-1, keepdims=True))
    kvn = _pad_rows(kvn_ref[...], page)
    msk = (jnp.dot(sel_rows, _block_expand(n_pages * per_tile, LANES, page), preferred_element_type=F32) > 0.5) & new_mask
    sn = jnp.where(msk, _nt_dot(qg, kvn[:, 2 * GW:3 * GW].astype(BF16)) + bs_ref[:, P:], NEG)
    m = jnp.maximum(m, jnp.max(sn, axis=-1, keepdims=True))
    pn = jnp.where(msk, jnp.exp(sn - m), 0.0)
    l = jnp.sum(pn, axis=-1, keepdims=True)
    acc = jnp.dot(pn.astype(BF16), kvn[:, 3 * GW:].astype(BF16), preferred_element_type=F32)
    for p in range(n_pages):
        pr = jnp.exp(s_scr[:, p * page:(p + 1) * page] - m)
        l = l + jnp.sum(pr, axis=-1, keepdims=True)
        acc = acc + _nt_dot(pr.astype(BF16), pages[p][1].reshape(GW, page).astype(BF16))
    o_s = acc / l

    kwt = win_ref[0].reshape(GW, n_win)
    vwt = win_ref[1].reshape(GW, n_win)
    rw = lax.broadcasted_iota(jnp.int32, (rows, n_win), 0) % nq
    cw = lax.broadcasted_iota(jnp.int32, (rows, n_win), 1)
    mask_w = (n_win + rw - cw) < WINDOW
    s_w = jnp.where(mask_w, jnp.dot(qg, kwt.astype(BF16), preferred_element_type=F32) + bw_ref[:, :n_win], NEG)
    wn = _pad_rows(wn_ref[...], page)
    s_n = jnp.where(new_mask, _nt_dot(qg, wn[:, :GW].astype(BF16)) + bw_ref[:, n_win:], NEG)
    m = jnp.maximum(jnp.max(s_w, axis=-1, keepdims=True), jnp.max(s_n, axis=-1, keepdims=True))
    p_w = jnp.where(mask_w, jnp.exp(s_w - m), 0.0)
    p_n = jnp.where(new_mask, jnp.exp(s_n - m), 0.0)
    l = jnp.sum(p_w, axis=-1, keepdims=True) + jnp.sum(p_n, axis=-1, keepdims=True)
    o_w = (_nt_dot(p_w.astype(BF16), vwt.astype(BF16))
           + jnp.dot(p_n.astype(BF16), wn[:, GW:].astype(BF16), preferred_element_type=F32)) / l

    gs = jax.nn.sigmoid(gate_ref[...])
    grep = jnp.concatenate([gs] * n_h, axis=0)
    glane = lax.broadcasted_iota(jnp.int32, (rows, LANES), 1)
    ghead = lax.broadcasted_iota(jnp.int32, (rows, LANES), 0) // nq
    o = jnp.zeros((rows, GW), F32)
    for br, o_b in enumerate((o_c, o_s, o_w)):
        o = o + jnp.sum(jnp.where(glane == br * n_h + ghead, grep, 0.0), axis=-1, keepdims=True) * o_b
    pieces = []
    for h in range(n_h):
        g = h // R
        t = o[h * nq:(h + 1) * nq, (g // 2) * LANES:(g // 2 + 1) * LANES]
        pieces.append(t if h % 2 == g % 2 else pltpu.roll(t, HEAD_DIM, 1))
    o_ref[...] = jnp.concatenate(
        [jnp.where(lane8 < HEAD_DIM, pieces[2 * u], pieces[2 * u + 1]) for u in range(n_h // 2)], axis=1)

    wt = win_ref[...].reshape(2 * GW, n_win)
    wnt = wn.T
    wout_ref[...] = jnp.concatenate([wt[:, nq:], wnt[:, :nq]], axis=1).reshape(wout_ref.shape)


def nsa_decode_attention(q, kvn, wn, gates, cmp, win_t, layer, pool_t, page_table, rel_bias):
    S, nq, W = q.shape
    G, R = NSA_KV_HEADS, NSA_HPG
    n_pages = page_table.shape[1]
    page = pool_t.shape[-1]
    P = n_pages * page
    n_win = win_t.shape[-1]
    assert nq < CMP_STRIDE and n_win == WINDOW and nq <= 8
    L = P + nq
    n_cmp = (L - CMP_LEN) // CMP_STRIDE + 1
    n_c = cmp.shape[1]
    n_slc = -(-L // SEL_BLOCK)
    n_blk = -(-n_slc // 8) * 8
    heads = range(G * R)
    qpos = P + np.arange(nq)
    bias_c = _dense_bias(rel_bias, qpos, np.arange(n_c) * CMP_STRIDE + CMP_LEN - 1, heads)
    bias_s = _dense_bias(rel_bias, qpos, np.arange(P + page), heads)
    bias_w = _dense_bias(rel_bias, qpos, P - n_win + np.arange(n_win + page), heads)
    ovt = _overlap_t(n_blk, n_c, n_cmp, n_slc)
    rows = G * R * nq
    const = lambda a: pl.BlockSpec(a.shape, lambda b, pt: (0,) * a.ndim)
    seq = lambda a: pl.BlockSpec((None,) + a.shape[1:], lambda b, pt: (b,) + (0,) * (a.ndim - 1))
    win_blk = (None, None) + win_t.shape[2:]
    grid_spec = pltpu.PrefetchScalarGridSpec(
        num_scalar_prefetch=1,
        grid=(S,),
        in_specs=[seq(q), seq(kvn), seq(wn), seq(gates), seq(cmp),
                  pl.BlockSpec(win_blk, lambda b, pt: (layer, b, 0, 0, 0, 0)),
                  const(bias_c), const(bias_s), const(bias_w), const(ovt)]
        + _page_specs(n_pages, (None, None, 2) + pool_t.shape[3:], lambda pg: (layer, pg, 1, 0, 0, 0)),
        out_specs=[seq(q), pl.BlockSpec((None,) + win_t.shape[2:], lambda b, pt: (b, 0, 0, 0, 0))],
        scratch_shapes=[pltpu.VMEM((rows, P), F32)],
    )
    return pl.pallas_call(
        functools.partial(_nsa_decode_kernel, n_pages=n_pages, page=page, n_cmp=n_cmp),
        grid_spec=grid_spec,
        out_shape=[jax.ShapeDtypeStruct(q.shape, F32), jax.ShapeDtypeStruct(win_t.shape[1:], F32)],
        compiler_params=_cparams("arbitrary"),
        name="nsa_decode",
    )(page_table, q, kvn, wn, gates, cmp, win_t, bias_c, bias_s, bias_w, ovt, *([pool_t] * n_pages))


def _w_cols(w, c0, width):
    piece = w[:, c0:c0 + width]
    if width % LANES:
        piece = jnp.pad(piece, ((0, 0), (0, LANES - width % LANES)))
    return piece.astype(BF16)


def _w_rows(w, c0, width):
    return w[:, c0:c0 + width].T.astype(BF16)


def _seq_major(a, nq, S):
    return jnp.transpose(a.reshape(nq, S, -1), (1, 0, 2))


def kernel(x_prompt, x_sample, cache_nsa_kv, state_nsa_win, cache_diff_k, cache_diff_v, cache_sb_kv, page_table, rel_bias, nsa_w_in, nsa_cmp_pos, nsa_cmp_w1, nsa_cmp_w2, nsa_w_out, diff_w_in, diff_lambda, diff_norm_g, diff_w_out, sb_w_in, sb_w_out, mlp_w_up, mlp_w_down, ln_g, ln_b):
    B, T, D = x_prompt.shape
    S, nq, _ = x_sample.shape
    depth = mlp_w_up.shape[0]
    alpha = (2 * depth) ** 0.25
    G, R, dk = NSA_KV_HEADS, NSA_HPG, HEAD_DIM
    n_pool, page = cache_nsa_kv.shape[1], cache_nsa_kv.shape[2]
    Hd, H = D // (2 * dk), D // dk
    xp = x_prompt.reshape(B * T, D)
    xs = jnp.transpose(x_sample, (1, 0, 2)).reshape(nq * S, D)
    nsa_pool_t = jnp.transpose(cache_nsa_kv, (0, 1, 3, 4, 5, 2))
    nsa_win_t = jnp.transpose(state_nsa_win, (0, 1, 3, 4, 5, 2))
    diff_k_t = jnp.transpose(cache_diff_k, (0, 1, 3, 4, 5, 2))
    sb_pool_t = jnp.transpose(cache_sb_kv, (0, 1, 3, 4, 5, 2))
    res = {k: [] for k in ("nsa_kv_p", "nsa_kv_s", "nsa_win_p", "nsa_win_s", "diff_k_p", "diff_k_s",
                           "diff_v_p", "diff_v_s", "sb_kv_p", "sb_kv_s")}
    tm_p = 512

    def to_tokens(a_t, lead):
        n, _, t = a_t.shape
        nd = len(lead)
        return jnp.transpose(a_t.reshape((n,) + lead + (t,)), (0, nd + 1) + tuple(range(1, nd + 1)))

    for i in range(depth):
        kind, j = i % N_MIXERS, i // N_MIXERS
        if kind == 0:
            w = nsa_w_in[j]
            nq_c, kv_c, win_c = G * R * dk, 4 * G * dk, 2 * G * dk
            row_p = [_w_cols(w, 0, nq_c), _w_cols(w, nq_c, 2 * G * dk), _w_cols(w, nq_c + kv_c + win_c, 3 * G * R)]
            t_ws = [_w_rows(w, nq_c, kv_c), _w_rows(w, nq_c + kv_c, win_c)]
            cmp_w = _compress_weights(nsa_cmp_pos[j], nsa_cmp_w1[j], nsa_cmp_w2[j])
            q, kc_rows, gates, kvt, wint = project(xp, row_p, t_ws, B, tm_p)
            cmp_p = compress_seq(kc_rows.reshape(B, T, -1), cmp_w)
            op = nsa_prompt_attention(q.reshape(B, T, -1), kvt, wint, cmp_p, gates.reshape(B, T, -1), rel_bias)
            row_s = [row_p[0], _w_cols(w, nq_c, kv_c), _w_cols(w, nq_c + kv_c, win_c), row_p[2]]
            qs, kvs, wns, gts, kvst = project(xs, row_s, t_ws[:1], nq, S)
            cmp_s = compress_paged(nsa_pool_t, j, page_table, cmp_w)
            os_, win_s = nsa_decode_attention(_seq_major(qs, nq, S), _seq_major(kvs, nq, S), _seq_major(wns, nq, S),
                                              _seq_major(gts, nq, S), cmp_s, nsa_win_t, j, nsa_pool_t, page_table,
                                              rel_bias)
            n_keep = min(WINDOW, T)
            res["nsa_kv_p"].append(to_tokens(kvt, (4, G, dk)))
            res["nsa_kv_s"].append(jnp.transpose(kvst.reshape(nq, 4, G, dk, S), (4, 0, 1, 2, 3)))
            res["nsa_win_p"].append(to_tokens(wint[:, :, T - n_keep:], (2, G, dk)))
            res["nsa_win_s"].append(jnp.transpose(win_s, (0, 4, 1, 2, 3)))
            w_out = nsa_w_out[j]
        elif kind == 1:
            lam_init = 0.8 - 0.6 * math.exp(-0.3 * i)
            w = diff_w_in[j]
            row_ws, t_ws = [_w_cols(w, 0, D), _w_cols(w, 2 * D, D)], [_w_rows(w, D, D)]
            q, v, kt = project(xp, row_ws, t_ws, B, tm_p)
            op = diff_prompt_attention(q.reshape(B, T, D), kt, v.reshape(B, T, D), rel_bias,
                                       diff_lambda[j], diff_norm_g[j], lam_init)
            qs, vsn, ksn, kst = project(xs, row_ws + [_w_cols(w, D, D)], t_ws, nq, S)
            vsn = _seq_major(vsn, nq, S)
            os_ = diff_decode_attention(_seq_major(qs, nq, S), _seq_major(ksn, nq, S), vsn, diff_k_t[j],
                                        cache_diff_v[j].reshape(n_pool, page * Hd, 2 * dk), page_table, rel_bias,
                                        diff_lambda[j], diff_norm_g[j], lam_init)
            res["diff_k_p"].append(to_tokens(kt, (Hd, 2, dk)))
            res["diff_k_s"].append(jnp.transpose(kst.reshape(nq, Hd, 2, dk, S), (4, 0, 1, 2, 3)))
            res["diff_v_p"].append(v.reshape(B, T, Hd, 2 * dk))
            res["diff_v_s"].append(vsn.reshape(S, nq, Hd, 2 * dk))
            w_out = diff_w_out[j]
        else:
            w = sb_w_in[j]
            row_ws, t_ws = [_w_cols(w, 0, D)], [_w_rows(w, D, 2 * D)]
            q, kvt = project(xp, row_ws, t_ws, B, tm_p)
            op = sb_prompt_attention(q.reshape(B, T, D), kvt)
            qs, kvs, kvst = project(xs, row_ws + [_w_cols(w, D, 2 * D)], t_ws, nq, S)
            os_ = sb_decode_attention(_seq_major(qs, nq, S), _seq_major(kvs, nq, S), sb_pool_t[j], page_table)
            res["sb_kv_p"].append(to_tokens(kvt, (2, H, dk)))
            res["sb_kv_s"].append(jnp.transpose(kvst.reshape(nq, 2, H, dk, S), (4, 0, 1, 2, 3)))
            w_out = sb_w_out[j]
        ln = jnp.stack([ln_g[i, 0], ln_b[i, 0], ln_g[i, 1], ln_b[i, 1]])
        w_out, w_up, w_down = w_out.astype(BF16), mlp_w_up[i].astype(BF16), mlp_w_down[i].astype(BF16)
        xp = post_mixer(op.reshape(B * T, D), w_out, xp, ln, w_up, w_down, alpha)
        os_ = jnp.transpose(os_, (1, 0, 2)).reshape(nq * S, D)
        xs = post_mixer(os_, w_out, xs, ln, w_up, w_down, alpha)
    return (xp.reshape(B, T, D), _seq_major(xs, nq, S),
            jnp.stack(res["nsa_kv_p"]), jnp.stack(res["nsa_kv_s"]), jnp.stack(res["nsa_win_p"]),
            jnp.stack(res["nsa_win_s"]), jnp.stack(res["diff_k_p"]), jnp.stack(res["diff_k_s"]),
            jnp.stack(res["diff_v_p"]), jnp.stack(res["diff_v_s"]), jnp.stack(res["sb_kv_p"]),
            jnp.stack(res["sb_kv_s"]))
```

```python
import functools
import math

import numpy as np
import jax
import jax.numpy as jnp
from jax import lax
from jax.experimental import pallas as pl
from jax.experimental.pallas import tpu as pltpu

F32 = jnp.float32
BF16 = jnp.bfloat16

HEAD_DIM = 64
NSA_KV_HEADS = 4
NSA_HPG = 4
CMP_LEN = 32
CMP_STRIDE = 16
SEL_BLOCK = 64
N_SEL = 16
SEL_FORCE = 1000.0
WINDOW = 512
REL_BUCKETS = 32
REL_MAX_DIST = 128
LN_EPS = 1e-5
NEG = -1e30
N_MIXERS = 3
SB_DEAD = 104.0

LANES = 128
VMEM_LIMIT = 48 * 1024 * 1024


def _cparams(*sem):
    return pltpu.CompilerParams(dimension_semantics=sem, vmem_limit_bytes=VMEM_LIMIT)


def _t5_bucket_np(dist):
    n = np.maximum(dist, 0)
    max_exact = REL_BUCKETS // 2
    nf = np.maximum(n, max_exact).astype(np.float32)
    large = max_exact + (np.log(nf / np.float32(max_exact)) / np.float32(math.log(REL_MAX_DIST / max_exact))
                         * np.float32(REL_BUCKETS - max_exact)).astype(np.int32)
    return np.where(n < max_exact, n, np.minimum(large, REL_BUCKETS - 1)).astype(np.int32)


FAR_DIST = int(np.min(np.nonzero(_t5_bucket_np(np.arange(4 * REL_MAX_DIST)) == REL_BUCKETS - 1)[0]))


def _bias_rows(rel_bias, dist):
    idx = jnp.asarray(_t5_bucket_np(dist))[None]
    rel = rel_bias.astype(F32)
    out = jnp.zeros((rel.shape[1],) + dist.shape, F32)
    for b in range(REL_BUCKETS):
        out = jnp.where(idx == b, rel[b].reshape((-1,) + (1,) * dist.ndim), out)
    return out


def _toeplitz_bias(rel_bias, tq, tk, n_off):
    i = np.arange(tq)[None, :, None]
    j = np.arange(tk)[None, None, :]
    off = np.arange(n_off)[:, None, None]
    return _bias_rows(rel_bias, off * tk + i - j)


def _dense_bias(rel_bias, qpos, kpos, cols):
    tab = _bias_rows(rel_bias, qpos[:, None] - kpos[None, :])[np.asarray(cols)]
    return tab.reshape(len(cols) * len(qpos), len(kpos))


def _layer_norm(z, g, b):
    mu = jnp.mean(z, axis=-1, keepdims=True)
    zc = z - mu
    var = jnp.mean(zc * zc, axis=-1, keepdims=True)
    return zc * lax.rsqrt(var + LN_EPS) * g + b


def _nt_dot(a, b):
    return lax.dot_general(a, b, (((1,), (1,)), ((), ())), preferred_element_type=F32)


def _project_kernel(*refs, n_row, n_t):
    x_ref = refs[0]
    row_w, t_w = refs[1:1 + n_row], refs[1 + n_row:1 + n_row + n_t]
    row_o, t_o = refs[1 + n_row + n_t:1 + 2 * n_row + n_t], refs[1 + 2 * n_row + n_t:]
    xb = x_ref[...].astype(BF16)
    for w_ref, o_ref in zip(row_w, row_o):
        o_ref[...] = jnp.dot(xb, w_ref[...], preferred_element_type=F32)
    for w_ref, o_ref in zip(t_w, t_o):
        o_ref[...] = _nt_dot(w_ref[...], xb)


def project(x, row_ws, t_ws, n_seq, tm):
    M, K = x.shape
    T = M // n_seq
    nt = T // tm
    n_row, n_t = len(row_ws), len(t_ws)
    const = lambda w: pl.BlockSpec(w.shape, lambda b, i: (0, 0))
    return pl.pallas_call(
        functools.partial(_project_kernel, n_row=n_row, n_t=n_t),
        grid=(n_seq, nt),
        in_specs=[pl.BlockSpec((tm, K), lambda b, i: (b * nt + i, 0))] + [const(w) for w in row_ws + t_ws],
        out_specs=[pl.BlockSpec((tm, w.shape[1]), lambda b, i: (b * nt + i, 0)) for w in row_ws]
        + [pl.BlockSpec((None, w.shape[0], tm), lambda b, i: (b, 0, i)) for w in t_ws],
        out_shape=[jax.ShapeDtypeStruct((M, w.shape[1]), F32) for w in row_ws]
        + [jax.ShapeDtypeStruct((n_seq, w.shape[0], T), F32) for w in t_ws],
        compiler_params=_cparams("parallel", "parallel"),
        name="project",
    )(x, *row_ws, *t_ws)


def _post_mixer_kernel(o_ref, wo_ref, x_ref, ln_ref, wu_ref, wd_ref, y_ref, x1_scr, xb_scr, acc_scr, *, alpha):
    f = pl.program_id(1)

    @pl.when(f == 0)
    def _():
        y = jnp.dot(o_ref[...].astype(BF16), wo_ref[...], preferred_element_type=F32)
        x1 = _layer_norm(alpha * x_ref[...] + y, ln_ref[0:1, :], ln_ref[1:2, :])
        x1_scr[...] = x1
        xb_scr[...] = x1.astype(BF16)
        acc_scr[...] = jnp.zeros_like(acc_scr)

    h = jnp.dot(xb_scr[...], wu_ref[...], preferred_element_type=F32)
    h = jnp.square(jnp.maximum(h, 0.0))
    acc_scr[...] += jnp.dot(h.astype(BF16), wd_ref[...], preferred_element_type=F32)

    @pl.when(f == pl.num_programs(1) - 1)
    def _():
        y_ref[...] = _layer_norm(alpha * x1_scr[...] + acc_scr[...], ln_ref[2:3, :], ln_ref[3:4, :])


def post_mixer(o, w_out, x, ln, w_up, w_down, alpha, tm=512, tf=1024):
    M, D = x.shape
    Fd = w_up.shape[1]
    return pl.pallas_call(
        functools.partial(_post_mixer_kernel, alpha=alpha),
        grid=(M // tm, Fd // tf),
        in_specs=[
            pl.BlockSpec((tm, D), lambda i, f: (i, 0)),
            pl.BlockSpec((D, D), lambda i, f: (0, 0)),
            pl.BlockSpec((tm, D), lambda i, f: (i, 0)),
            pl.BlockSpec((4, D), lambda i, f: (0, 0)),
            pl.BlockSpec((D, tf), lambda i, f: (0, f)),
            pl.BlockSpec((tf, D), lambda i, f: (f, 0)),
        ],
        out_specs=pl.BlockSpec((tm, D), lambda i, f: (i, 0)),
        out_shape=jax.ShapeDtypeStruct((M, D), F32),
        scratch_shapes=[pltpu.VMEM((tm, D), F32), pltpu.VMEM((tm, D), BF16), pltpu.VMEM((tm, D), F32)],
        compiler_params=_cparams("parallel", "arbitrary"),
        name="post_mixer",
    )(o, w_out, x, ln, w_up, w_down)


LOG2E = math.log2(math.e)


def _flash_step(carry, s, pv):
    m, l, acc = carry
    m_new = jnp.maximum(m, jnp.max(s, axis=-1, keepdims=True))
    alpha = jnp.exp2(m - m_new)
    p = jnp.exp2(s - m_new)
    if l is not None:
        l = alpha * l + jnp.sum(p, axis=-1, keepdims=True)
    acc = alpha * acc + pv(p.astype(BF16))
    return m_new, l, acc


def _flash_init(rows, width, with_l=True):
    return jnp.full((rows, 1), NEG, F32), jnp.zeros((rows, 1), F32) if with_l else None, jnp.zeros((rows, width), F32)


def _split_heads(q, tq):
    lane = lax.broadcasted_iota(jnp.int32, (tq, LANES), 1)
    lo = jnp.where(lane < HEAD_DIM, q, 0.0)
    hi = jnp.where(lane >= HEAD_DIM, q, 0.0)
    return jnp.concatenate([lo, hi], axis=0).astype(BF16)


def _cols(ref, start, width):
    return ref[:, pl.ds(pl.multiple_of(start, width), width)].astype(BF16)


def _rows(ref, start, height):
    return ref[pl.ds(pl.multiple_of(start, height), height), :].astype(BF16)


def _rows_blockdiag(q, n_blk, blk_w):
    rep = jnp.concatenate([q] * n_blk, axis=0)
    r = lax.broadcasted_iota(jnp.int32, rep.shape, 0) // q.shape[0]
    c = lax.broadcasted_iota(jnp.int32, rep.shape, 1) // blk_w
    return jnp.where(r == c, rep, 0.0).astype(BF16)


def _pad_rows(a, rows):
    return jnp.concatenate([a, jnp.zeros((rows - a.shape[0], a.shape[1]), a.dtype)], axis=0)


def _page_specs(n_pages, block, index):
    return [pl.BlockSpec(block, functools.partial(lambda b, pt, p: index(pt[b, p]), p=p)) for p in range(n_pages)]


def _diff_lambda(lam_ref, lam_init):
    lf = lam_ref[...]
    a = jnp.sum(lf[0:1, :] * lf[1:2, :], axis=-1, keepdims=True)
    b = jnp.sum(lf[2:3, :] * lf[3:4, :], axis=-1, keepdims=True)
    return jnp.exp(a) - jnp.exp(b) + lam_init


def _diff_finish(o1, o2, lam_full, g, lam_init):
    o = o1 - lam_full * o2
    o = o * lax.rsqrt(jnp.mean(o * o, axis=-1, keepdims=True) + LN_EPS) * g
    return o * (1.0 - lam_init)


def _causal_bias(tq, tk):
    ri = lax.broadcasted_iota(jnp.int32, (tq, tk), 0)
    ci = lax.broadcasted_iota(jnp.int32, (tq, tk), 1)
    return jnp.where(ci <= ri, 0.0, NEG)


def _add_per_head(s, b, n_blocks):
    tq, tk = b.shape
    return (s.reshape(n_blocks, tq, tk) + b[None]).reshape(n_blocks * tq, tk)


def _diff_prompt_kernel(q_ref, kt_ref, v_ref, toe_ref, lam_ref, g_ref, o_ref, *, tq, lam_init):
    qt = pl.program_id(2)
    tk = tq
    rows = 2 * tq
    q2 = _split_heads(q_ref[...] * (HEAD_DIM ** -0.5 * LOG2E), tq)

    def qk(kt):
        return jnp.dot(q2, _cols(kt_ref, kt * tk, tk), preferred_element_type=F32)

    def step(kt, carry, bias):
        s = qk(kt)
        if bias is not None:
            s = s + bias
        v = _rows(v_ref, kt * tk, tk)
        return _flash_step(carry, s, lambda p: jnp.dot(p, v, preferred_element_type=F32))

    near0 = jnp.maximum(qt - 1, 0)
    carry = lax.fori_loop(0, near0, lambda kt, c: step(kt, c, None), _flash_init(rows, LANES))
    carry = lax.fori_loop(near0, qt, lambda kt, c: step(kt, c, toe_ref[1]), carry)
    _, l, acc = step(qt, carry, _add_per_head(toe_ref[0], _causal_bias(tq, tk), 2))
    o = acc / l
    o_ref[...] = _diff_finish(o[:tq], o[tq:], _diff_lambda(lam_ref, lam_init), g_ref[...], lam_init)


def _toeplitz_rel(rel_bias, tq, tk, n_off):
    far = rel_bias[REL_BUCKETS - 1].astype(F32)[:, None, None, None]
    return (_toeplitz_bias(rel_bias, tq, tk, n_off) - far) * LOG2E


def diff_prompt_attention(q, kt, v, rel_bias, lam, norm_g, lam_init, tq=512):
    B, T, W = q.shape
    n_heads = W // LANES
    toe = _toeplitz_rel(rel_bias, tq, tq, 2)
    toe = jnp.concatenate([toe[:n_heads], toe[n_heads:]], axis=2)
    return pl.pallas_call(
        functools.partial(_diff_prompt_kernel, tq=tq, lam_init=lam_init),
        grid=(B, n_heads, T // tq),
        in_specs=[
            pl.BlockSpec((None, tq, LANES), lambda b, h, i: (b, i, h)),
            pl.BlockSpec((None, LANES, T), lambda b, h, i: (b, h, 0)),
            pl.BlockSpec((None, T, LANES), lambda b, h, i: (b, 0, h)),
            pl.BlockSpec((None, 2, 2 * tq, tq), lambda b, h, i: (h, 0, 0, 0)),
            pl.BlockSpec((4, HEAD_DIM), lambda b, h, i: (0, 0)),
            pl.BlockSpec((1, LANES), lambda b, h, i: (0, 0)),
        ],
        out_specs=pl.BlockSpec((None, tq, LANES), lambda b, h, i: (b, i, h)),
        out_shape=jax.ShapeDtypeStruct((B, T, W), F32),
        compiler_params=_cparams("parallel", "parallel", "arbitrary"),
        name="diff_prompt",
    )(q, kt, v, toe, lam, norm_g.reshape(1, LANES))


def _diff_decode_kernel(pt_ref, q_ref, kn_ref, vn_ref, bias_ref, lam_ref, g_ref, *rest, n_pages, page, lam_init):
    kp, vp = rest[:n_pages], rest[n_pages:2 * n_pages]
    o_ref, s_scr = rest[2 * n_pages], rest[2 * n_pages + 1]
    nq, W = q_ref.shape
    n_heads = W // LANES
    hr = 2 * nq
    rows = n_heads * hr
    qbd = _rows_blockdiag(q_ref[...] * HEAD_DIM ** -0.5, 2 * n_heads, HEAD_DIM)

    hblk = lax.broadcasted_iota(jnp.int32, (rows, page), 0) // hr

    def pv(p, v_of_head):
        p_bd = jnp.concatenate([jnp.where(hblk == h, p, jnp.zeros_like(p)) for h in range(n_heads)], axis=1)
        v_hk = jnp.concatenate([v_of_head(h) for h in range(n_heads)], axis=0)
        return jnp.dot(p_bd, v_hk, preferred_element_type=F32)

    m = jnp.full((rows, 1), NEG, F32)
    for p in range(n_pages):
        kt = kp[p][...].reshape(W, page).astype(BF16)
        s = jnp.dot(qbd, kt, preferred_element_type=F32) + bias_ref[:, p * page:(p + 1) * page]
        s_scr[:, p * page:(p + 1) * page] = s
        m = jnp.maximum(m, jnp.max(s, axis=-1, keepdims=True))
    ri = lax.broadcasted_iota(jnp.int32, (rows, page), 0) % nq
    ci = lax.broadcasted_iota(jnp.int32, (rows, page), 1)
    mask = ci <= ri
    kn = _pad_rows(kn_ref[...], page).astype(BF16)
    vn = _pad_rows(vn_ref[...], page).astype(BF16)
    sn = jnp.where(mask, _nt_dot(qbd, kn) + bias_ref[:, n_pages * page:], NEG)
    m = jnp.maximum(m, jnp.max(sn, axis=-1, keepdims=True))
    pn = jnp.where(mask, jnp.exp(sn - m), 0.0)
    l = jnp.sum(pn, axis=-1, keepdims=True)
    acc = pv(pn.astype(BF16), lambda h: vn[:, h * LANES:(h + 1) * LANES])
    for p in range(n_pages):
        pr = jnp.exp(s_scr[:, p * page:(p + 1) * page] - m)
        l = l + jnp.sum(pr, axis=-1, keepdims=True)
        acc = acc + pv(pr.astype(BF16), lambda h: vp[p][pl.ds(h, page, stride=n_heads), :].astype(BF16))
    o = acc / l
    lam_full = _diff_lambda(lam_ref, lam_init)
    outs = [_diff_finish(o[h * hr:h * hr + nq], o[h * hr + nq:(h + 1) * hr], lam_full, g_ref[...], lam_init)
            for h in range(n_heads)]
    o_ref[...] = jnp.concatenate(outs, axis=1)


def diff_decode_attention(q, kn, vn, kt_pool, v_pool, page_table, rel_bias, lam, norm_g, lam_init):
    S, nq, W = q.shape
    n_pages = page_table.shape[1]
    page = kt_pool.shape[-1]
    n_heads = W // LANES
    P = n_pages * page
    cols = [mp * n_heads + h for h in range(n_heads) for mp in range(2)]
    bias = _dense_bias(rel_bias, P + np.arange(nq), np.arange(P + page), cols)
    rows = bias.shape[0]
    seq = pl.BlockSpec((None, nq, W), lambda b, pt: (b, 0, 0))
    grid_spec = pltpu.PrefetchScalarGridSpec(
        num_scalar_prefetch=1,
        grid=(S,),
        in_specs=[seq, seq, seq,
                  pl.BlockSpec(bias.shape, lambda b, pt: (0, 0)),
                  pl.BlockSpec((4, HEAD_DIM), lambda b, pt: (0, 0)),
                  pl.BlockSpec((1, LANES), lambda b, pt: (0, 0))]
        + _page_specs(n_pages, (None,) + kt_pool.shape[1:], lambda pg: (pg, 0, 0, 0, 0))
        + _page_specs(n_pages, (None,) + v_pool.shape[1:], lambda pg: (pg, 0, 0)),
        out_specs=seq,
        scratch_shapes=[pltpu.VMEM((rows, P), F32)],
    )
    return pl.pallas_call(
        functools.partial(_diff_decode_kernel, n_pages=n_pages, page=page, lam_init=lam_init),
        grid_spec=grid_spec,
        out_shape=jax.ShapeDtypeStruct((S, nq, W), F32),
        compiler_params=_cparams("arbitrary"),
        name="diff_decode",
    )(page_table, q, kn, vn, bias, lam, norm_g.reshape(1, LANES), *([kt_pool] * n_pages), *([v_pool] * n_pages))


def _softplus(z):
    return jnp.maximum(z, 0.0) + jnp.log(1.0 + jnp.exp(-jnp.abs(z)))


def _suffix_sum(c, u):
    hi = c.astype(BF16)
    lo = (c - hi.astype(F32)).astype(BF16)
    return jnp.dot(hi, u, preferred_element_type=F32) + jnp.dot(lo, u, preferred_element_type=F32)


def _sb_local(z, u, mask=None):
    c = _softplus(z)
    if mask is not None:
        c = jnp.where(mask, c, 0.0)
    return z - c - _suffix_sum(c, u), jnp.sum(c, axis=-1, keepdims=True)


def _sb_weights(e, r, mask=None):
    a = jnp.exp(e - r)
    if mask is not None:
        a = jnp.where(mask, a, 0.0)
    return a.astype(BF16)


def _strict_upper(n):
    return jnp.asarray(np.arange(n)[:, None] > np.arange(n)[None, :], BF16)


def _sb_prompt_kernel(q_ref, kt_ref, vt_ref, u_ref, o_ref, *, tq, tk):
    qt = pl.program_id(2)
    n_sub = tq // tk
    rows = 2 * tq
    q2 = _split_heads(q_ref[...] * HEAD_DIM ** -0.5, tq)
    u = u_ref[...]

    def block(kt, carry, diag):
        r, acc = carry
        local = []
        for sub in range(n_sub):
            mask = None
            if diag:
                ri = lax.broadcasted_iota(jnp.int32, (rows, tk), 0) % tq
                ci = lax.broadcasted_iota(jnp.int32, (rows, tk), 1) + sub * tk
                mask = ci < ri
            z = jnp.dot(q2, _cols(kt_ref, kt * tq + sub * tk, tk), preferred_element_type=F32)
            local.append(_sb_local(z, u, mask) + (mask,))
        for sub in reversed(range(n_sub)):
            e, tot, mask = local[sub]
            acc = acc + _nt_dot(_sb_weights(e, r, mask), _cols(vt_ref, kt * tq + sub * tk, tk))
            r = r + tot
        return r, acc

    r, acc = block(qt, (jnp.zeros((rows, 1), F32), jnp.zeros((rows, LANES), F32)), True)

    def cond(c):
        return jnp.logical_and(c[0] >= 0, jnp.min(c[1]) < SB_DEAD)

    def body(c):
        r, acc = block(c[0], (c[1], c[2]), False)
        return c[0] - 1, r, acc

    _, _, acc = lax.while_loop(cond, body, (qt - 1, r, acc))
    lane = lax.broadcasted_iota(jnp.int32, (tq, LANES), 1)
    o_ref[...] = jnp.where(lane < HEAD_DIM, acc[:tq], acc[tq:])


def sb_prompt_attention(q, kvt, tq=256, tk=256):
    B, T, W = q.shape
    n_pairs = W // LANES
    return pl.pallas_call(
        functools.partial(_sb_prompt_kernel, tq=tq, tk=tk),
        grid=(B, n_pairs, T // tq),
        in_specs=[
            pl.BlockSpec((None, tq, LANES), lambda b, h, i: (b, i, h)),
            pl.BlockSpec((None, LANES, T), lambda b, h, i: (b, h, 0)),
            pl.BlockSpec((None, LANES, T), lambda b, h, i: (b, n_pairs + h, 0)),
            pl.BlockSpec((tk, tk), lambda b, h, i: (0, 0)),
        ],
        out_specs=pl.BlockSpec((None, tq, LANES), lambda b, h, i: (b, i, h)),
        out_shape=jax.ShapeDtypeStruct((B, T, W), F32),
        compiler_params=_cparams("parallel", "parallel", "arbitrary"),
        name="sb_prompt",
    )(q, kvt, kvt, _strict_upper(tk))


def _sb_decode_kernel(pt_ref, q_ref, kvn_ref, u_ref, *rest, n_pages, page):
    pages, o_ref, e_scr = rest[:n_pages], rest[n_pages], rest[n_pages + 1]
    nq, W = q_ref.shape
    n_heads = W // HEAD_DIM
    rows = n_heads * nq
    u = u_ref[...]
    qbd = _rows_blockdiag(q_ref[...] * HEAD_DIM ** -0.5, n_heads, HEAD_DIM)
    kvn = _pad_rows(kvn_ref[...], page).astype(BF16)
    ri = lax.broadcasted_iota(jnp.int32, (rows, page), 0) % nq
    ci = lax.broadcasted_iota(jnp.int32, (rows, page), 1)
    new_mask = ci < ri
    e_new, r = _sb_local(_nt_dot(qbd, kvn[:, :W]), u, new_mask)
    tots = []
    for p in range(n_pages):
        kt = pages[p][0].reshape(W, page).astype(BF16)
        e, tot = _sb_local(jnp.dot(qbd, kt, preferred_element_type=F32), u)
        e_scr[:, p * page:(p + 1) * page] = e
        tots.append(tot)
    acc = jnp.dot(_sb_weights(e_new, 0.0, new_mask), kvn[:, W:], preferred_element_type=F32)
    for p in reversed(range(n_pages)):
        vt = pages[p][1].reshape(W, page).astype(BF16)
        acc = acc + _nt_dot(_sb_weights(e_scr[:, p * page:(p + 1) * page], r), vt)
        r = r + tots[p]
    col = lax.broadcasted_iota(jnp.int32, (nq, W), 1) // HEAD_DIM
    o = jnp.zeros((nq, W), F32)
    for h in range(n_heads):
        o = o + jnp.where(col == h, acc[h * nq:(h + 1) * nq, :], 0.0)
    o_ref[...] = o


def sb_decode_attention(q, kvn, kvt_pool, page_table):
    S, nq, W = q.shape
    n_pages = page_table.shape[1]
    page = kvt_pool.shape[-1]
    grid_spec = pltpu.PrefetchScalarGridSpec(
        num_scalar_prefetch=1,
        grid=(S,),
        in_specs=[pl.BlockSpec((None, nq, W), lambda b, pt: (b, 0, 0)),
                  pl.BlockSpec((None, nq, 2 * W), lambda b, pt: (b, 0, 0)),
                  pl.BlockSpec((page, page), lambda b, pt: (0, 0))]
        + _page_specs(n_pages, (None,) + kvt_pool.shape[1:], lambda pg: (pg, 0, 0, 0, 0)),
        out_specs=pl.BlockSpec((None, nq, W), lambda b, pt: (b, 0, 0)),
        scratch_shapes=[pltpu.VMEM((W // HEAD_DIM * nq, n_pages * page), F32)],
    )
    return pl.pallas_call(
        functools.partial(_sb_decode_kernel, n_pages=n_pages, page=page),
        grid_spec=grid_spec,
        out_shape=jax.ShapeDtypeStruct((S, nq, W), F32),
        compiler_params=_cparams("arbitrary"),
        name="sb_decode",
    )(page_table, q, kvn, _strict_upper(page), *([kvt_pool] * n_pages))


N_CMP_TILES = 2 * NSA_KV_HEADS * HEAD_DIM // LANES


def _gelu_tanh(x):
    return 0.5 * x * (1.0 + jnp.tanh(math.sqrt(2.0 / math.pi) * (x + 0.044715 * x * x * x)))


def _compress_body(chunk_rows, w1_ref, pos_ref, w2_ref, o_ref, n_chunks):
    n_kinds = w1_ref.shape[0]
    tiles_per_kind = NSA_KV_HEADS * HEAD_DIM // LANES
    for kind in range(n_kinds):
        xcat = jnp.concatenate(
            [jnp.concatenate([chunk_rows(l, kind * tiles_per_kind + t) for l in range(CMP_STRIDE)], axis=1)
             for t in range(tiles_per_kind)], axis=0)
        u = []
        for a in range(2):
            xa = (xcat + pos_ref[kind, a]).astype(BF16)
            u.append(jnp.dot(xa, w1_ref[kind, a], preferred_element_type=F32))
        for t in range(tiles_per_kind):
            u0 = u[0][t * n_chunks:(t + 1) * n_chunks]
            u1 = pltpu.roll(u[1][t * n_chunks:(t + 1) * n_chunks], n_chunks - 1, 0)
            hid = _gelu_tanh(u0 + u1)
            out = jnp.dot(hid.astype(BF16), w2_ref[kind], preferred_element_type=F32)
            col = (kind * tiles_per_kind + t) * LANES
            o_ref[:, col:col + LANES] = out


def _compress_seq_kernel(*refs, n_chunks):
    x_refs = refs[:N_CMP_TILES]
    w1_ref, pos_ref, w2_ref, o_ref = refs[N_CMP_TILES:]

    def chunk_rows(l, tile):
        return x_refs[tile][pl.ds(l, n_chunks, stride=CMP_STRIDE), :]
    _compress_body(chunk_rows, w1_ref, pos_ref, w2_ref, o_ref, n_chunks)


def _compress_paged_kernel(pt_ref, w1_ref, pos_ref, w2_ref, *rest, n_pages, page):
    pages, o_ref, x_scr = rest[:n_pages], rest[n_pages], rest[n_pages + 1]
    tiles_per_kind = NSA_KV_HEADS * HEAD_DIM // LANES
    for p in range(n_pages):
        for t in range(N_CMP_TILES):
            kind, pair = t // tiles_per_kind, t % tiles_per_kind
            xt = pages[p][kind, 2 * pair:2 * pair + 2].reshape(LANES, page)
            x_scr[t, p * page:(p + 1) * page, :] = xt.T
    n_chunks = n_pages * page // CMP_STRIDE

    def chunk_rows(l, tile):
        return x_scr[tile, pl.ds(l, n_chunks, stride=CMP_STRIDE), :]
    _compress_body(chunk_rows, w1_ref, pos_ref, w2_ref, o_ref, n_chunks)


def _compress_weights(cmp_pos, cmp_w1, cmp_w2):
    eye2 = jnp.eye(2, dtype=F32)
    w1r = cmp_w1.reshape(2, 2, CMP_STRIDE, HEAD_DIM, -1)
    w1e = jnp.einsum('kaldj,hg->kalhdgj', w1r, eye2)
    w1e = w1e.reshape(2, 2, CMP_STRIDE * LANES, 2 * cmp_w1.shape[-1]).astype(BF16)
    pos = jnp.broadcast_to(cmp_pos.reshape(2, 2, CMP_STRIDE, 1, HEAD_DIM), (2, 2, CMP_STRIDE, 2, HEAD_DIM))
    pos = pos.reshape(2, 2, 1, CMP_STRIDE * LANES).astype(F32)
    w2e = jnp.einsum('kjd,hg->khjgd', cmp_w2, eye2).reshape(2, 2 * cmp_w2.shape[1], LANES).astype(BF16)
    return w1e, pos, w2e


def compress_seq(kv, cmp_w):
    B, L, _ = kv.shape
    w1e, pos, w2e = cmp_w
    n_chunks = L // CMP_STRIDE
    W = N_CMP_TILES * LANES
    return pl.pallas_call(
        functools.partial(_compress_seq_kernel, n_chunks=n_chunks),
        grid=(B,),
        in_specs=[pl.BlockSpec((None, L, LANES), functools.partial(lambda b, t: (b, 0, t), t=t)) for t in range(N_CMP_TILES)]
        + [pl.BlockSpec(w1e.shape, lambda b: (0, 0, 0, 0)),
           pl.BlockSpec(pos.shape, lambda b: (0, 0, 0, 0)),
           pl.BlockSpec(w2e.shape, lambda b: (0, 0, 0))],
        out_specs=pl.BlockSpec((None, n_chunks, W), lambda b: (b, 0, 0)),
        out_shape=jax.ShapeDtypeStruct((B, n_chunks, W), F32),
        compiler_params=_cparams("parallel"),
        name="nsa_compress_seq",
    )(*([kv] * N_CMP_TILES), w1e, pos, w2e)


def compress_paged(pool_t, layer, page_table, cmp_w):
    S, n_pages = page_table.shape
    page = pool_t.shape[-1]
    w1e, pos, w2e = cmp_w
    n_chunks = n_pages * page // CMP_STRIDE
    W = N_CMP_TILES * LANES
    blk = (None, None, 2) + pool_t.shape[3:]
    n_seq = 2 if S % 2 == 0 else 1
    specs = [pl.BlockSpec(blk, functools.partial(lambda b, pt, s, p: (layer, pt[b * n_seq + s, p], 0, 0, 0, 0), s=s, p=p))
             for s in range(n_seq) for p in range(n_pages)]
    grid_spec = pltpu.PrefetchScalarGridSpec(
        num_scalar_prefetch=1,
        grid=(S // n_seq,),
        in_specs=[pl.BlockSpec(w1e.shape, lambda b, pt: (0, 0, 0, 0)),
                  pl.BlockSpec(pos.shape, lambda b, pt: (0, 0, 0, 0)),
                  pl.BlockSpec(w2e.shape, lambda b, pt: (0, 0, 0))] + specs,
        out_specs=pl.BlockSpec((None, n_seq * n_chunks, W), lambda b, pt: (b, 0, 0)),
        scratch_shapes=[pltpu.VMEM((N_CMP_TILES, n_seq * n_pages * page, LANES), F32)],
    )
    out = pl.pallas_call(
        functools.partial(_compress_paged_kernel, n_pages=n_seq * n_pages, page=page),
        grid_spec=grid_spec,
        out_shape=jax.ShapeDtypeStruct((S // n_seq, n_seq * n_chunks, W), F32),
        compiler_params=_cparams("arbitrary"),
        name="nsa_compress_paged",
    )(page_table, w1e, pos, w2e, *([pool_t] * (n_seq * n_pages)))
    return out.reshape(S, n_chunks, W)


def _masked_softmax(s, mask, exp=jnp.exp):
    s = jnp.where(mask, s, NEG)
    m = jnp.max(s, axis=-1, keepdims=True)
    e = jnp.where(mask, exp(s - m), 0.0)
    return e * (1.0 / jnp.maximum(jnp.sum(e, axis=-1, keepdims=True), 1e-30))


def _importance_t(ov_t, psum):
    hi = psum.astype(BF16)
    lo = (psum - hi.astype(F32)).astype(BF16)
    return _nt_dot(ov_t, hi) + _nt_dot(ov_t, lo)


def _select_topk_t(imp, n_rows):
    jj = lax.broadcasted_iota(jnp.int32, imp.shape, 0)
    cnt = jnp.zeros(imp.shape, F32)
    for i in range(n_rows):
        row = imp[i:i + 1, :]
        ahead = (row > imp) | ((row == imp) & (jj > i))
        cnt = cnt + jnp.where(ahead, 1.0, 0.0)
    return jnp.where(cnt < N_SEL, 1.0, 0.0)


def _select_topk_ref(imp_ref, n_active):
    imp = imp_ref[...]
    jj = lax.broadcasted_iota(jnp.int32, imp.shape, 0)

    def body(i, cnt):
        row = imp_ref[pl.ds(i, 1), :]
        ahead = (row > imp) | ((row == imp) & (jj > i))
        return cnt + jnp.where(ahead, 1.0, 0.0)

    cnt = lax.fori_loop(0, n_active, body, jnp.zeros(imp.shape, F32))
    return jnp.where(cnt < N_SEL, 1.0, 0.0)


def _block_expand(first_block, n_blocks, tk):
    j = lax.broadcasted_iota(jnp.int32, (n_blocks, tk), 0)
    col = lax.broadcasted_iota(jnp.int32, (n_blocks, tk), 1)
    return jnp.where(j == first_block + col // SEL_BLOCK, 1.0, 0.0).astype(BF16)


CMP_NEAR_BACK = -(-(CMP_LEN - 1 + FAR_DIST) // CMP_STRIDE)


def _nsa_prompt_kernel(q_ref, kst_ref, vst_ref, kwt_ref, vwt_ref, kc_ref, vc_ref, gate_ref, pat_ref, toe_ref,
                       ovt_ref, o_ref, imp_scr, *, tq, n_cmp):
    g = pl.program_id(1)
    qt = pl.program_id(2)
    tk = tq
    R = NSA_HPG
    par = g % 2
    lane = lax.broadcasted_iota(jnp.int32, (tq, LANES), 1)
    own = (lane // HEAD_DIM) == par
    q = q_ref[...] * (HEAD_DIM ** -0.5 * LOG2E)
    parts = []
    for r in range(R):
        t = q[:, (r // 2) * LANES:(r // 2 + 1) * LANES]
        src = jnp.where(par == r % 2, t, pltpu.roll(t, HEAD_DIM, 1))
        parts.append(jnp.where(own, src, 0.0))
    q4 = jnp.concatenate(parts, axis=0).astype(BF16)
    rows = R * tq

    def toe(off):
        return toe_ref[:, off].reshape(rows, tk)

    n_c = kc_ref.shape[0]
    cc = lax.broadcasted_iota(jnp.int32, (rows, n_c), 1)
    tt = qt * tq + lax.broadcasted_iota(jnp.int32, (rows, n_c), 0) % tq
    mask_c = (cc * CMP_STRIDE + CMP_LEN - 1 <= tt) & (cc < n_cmp)
    pat = pat_ref[...].reshape(rows, LANES)
    uu = lax.broadcasted_iota(jnp.int32, (LANES, n_c), 0)
    shift = jnp.where(lax.broadcasted_iota(jnp.int32, (LANES, n_c), 1) == qt * (tq // CMP_STRIDE) - CMP_NEAR_BACK + uu,
                      1.0, 0.0).astype(BF16)
    s_c = _nt_dot(q4, kc_ref[...].astype(BF16)) + jnp.dot(pat.astype(BF16), shift, preferred_element_type=F32)
    p_c = _masked_softmax(s_c, mask_c, jnp.exp2)
    o_c = jnp.dot(p_c.astype(BF16), vc_ref[...].astype(BF16), preferred_element_type=F32)
    psum = p_c[0:tq]
    for r in range(1, R):
        psum = psum + p_c[r * tq:(r + 1) * tq]
    n_blk = ovt_ref.shape[0]
    imp = _importance_t(ovt_ref[...], psum)
    jj = lax.broadcasted_iota(jnp.int32, (n_blk, tq), 0)
    tq_pos = qt * tq + lax.broadcasted_iota(jnp.int32, (n_blk, tq), 1)
    t_blk = tq_pos // SEL_BLOCK
    forced = (jj == 0) | (jj == t_blk) | (jj == t_blk - 1)
    imp_scr[...] = jnp.where(jj * SEL_BLOCK <= tq_pos, imp + jnp.where(forced, SEL_FORCE, 0.0), -1.0)
    sel_t = _select_topk_ref(imp_scr, jnp.minimum((qt + 1) * (tq // SEL_BLOCK), n_blk))
    sel = _pad_rows(sel_t, LANES).T
    sel_neg = ((sel - 1.0) * -NEG).astype(BF16)

    own_rows = (lax.broadcasted_iota(jnp.int32, (LANES, 1), 0) // HEAD_DIM) == par

    def branch(lhs, rhs, v_ref, first, tile_bias):
        def qk(kt):
            return jnp.dot(lhs, rhs(kt), preferred_element_type=F32)

        def finish_tile(kt, carry, s, toe_off, extra):
            b = tile_bias(kt)
            if extra is not None:
                b = extra if b is None else b + extra
            if toe_off is not None:
                s = s + toe(toe_off)
            if b is not None:
                s = _add_per_head(s, b, R)
            vt = jnp.where(own_rows, _cols(v_ref, kt * tk, tk), 1.0)
            return _flash_step(carry, s, lambda p: _nt_dot(p, vt))

        def step(kt, carry, toe_off):
            return finish_tile(kt, carry, qk(kt), toe_off, None)

        carry = _flash_init(rows, LANES, with_l=False)
        carry = lax.fori_loop(first, near0, lambda kt, c: step(kt, c, None), carry)
        carry = lax.fori_loop(jnp.maximum(near0, first), qt, lambda kt, c: step(kt, c, 1), carry)
        _, _, acc = finish_tile(qt, carry, qk(qt), 0, _causal_bias(tq, tk))
        return acc / pltpu.roll(acc, HEAD_DIM, 1)

    near0 = jnp.maximum(qt - 1, 0)

    sel_lhs = jnp.concatenate([q4, jnp.concatenate([sel_neg] * R, axis=0)], axis=1)

    def sel_rhs(kt):
        return jnp.concatenate([_cols(kst_ref, kt * tk, tk), _block_expand(kt * (tk // SEL_BLOCK), LANES, tk)], axis=0)

    o_s = branch(sel_lhs, sel_rhs, vst_ref, 0, lambda kt: None)

    n_back = WINDOW // tk
    ri = lax.broadcasted_iota(jnp.int32, (tq, tk), 0)
    ci = lax.broadcasted_iota(jnp.int32, (tq, tk), 1)

    def win_bias(kt):
        return jnp.where((ci > ri) | (kt != qt - n_back), 0.0, NEG)

    o_w = branch(q4, lambda kt: _cols(kwt_ref, kt * tk, tk), vwt_ref, jnp.maximum(qt - n_back, 0), win_bias)

    gs = jax.nn.sigmoid(gate_ref[...])
    n_h = NSA_KV_HEADS * R
    outs = []
    for r in range(R):
        o_r = jnp.zeros((tq, LANES), F32)
        for br, o_b in enumerate((o_c, o_s, o_w)):
            gcol = jnp.sum(jnp.where(lane == br * n_h + g * R + r, gs, 0.0), axis=-1, keepdims=True)
            o_r = o_r + gcol * o_b[r * tq:(r + 1) * tq]
        outs.append(jnp.where(par == r % 2, o_r, pltpu.roll(o_r, HEAD_DIM, 1)))
    for u in range(R // 2):
        o_ref[:, u * LANES:(u + 1) * LANES] = jnp.where(lane < HEAD_DIM, outs[2 * u], outs[2 * u + 1])


def _overlap_t(n_blk_rows, n_cmp_cols, n_cmp, n_slc):
    c0 = np.arange(n_cmp_cols)[None, :] * CMP_STRIDE
    j0 = np.arange(n_blk_rows)[:, None] * SEL_BLOCK
    ov = (c0 < j0 + SEL_BLOCK) & (c0 + CMP_LEN > j0)
    ov &= (np.arange(n_cmp_cols)[None, :] < n_cmp) & (np.arange(n_blk_rows)[:, None] < n_slc)
    return jnp.asarray(ov, BF16)


def nsa_prompt_attention(q, kvt, wint, cmp, gates, rel_bias, tq=512):
    B, T, W = q.shape
    G, R = NSA_KV_HEADS, NSA_HPG
    n_cmp = (T - CMP_LEN) // CMP_STRIDE + 1
    n_c = cmp.shape[1]
    n_slc = -(-T // SEL_BLOCK)
    n_blk = -(-n_slc // 8) * 8
    assert WINDOW % tq == 0 and n_blk <= LANES
    toe = _toeplitz_rel(rel_bias, tq, tq, 2)
    c31 = rel_bias[REL_BUCKETS - 1].astype(F32)
    n_near = CMP_NEAR_BACK + (tq - CMP_LEN) // CMP_STRIDE + 1
    assert n_near <= LANES and tq % CMP_STRIDE == 0
    dist = np.arange(tq)[:, None] - (CMP_LEN - 1) - CMP_STRIDE * (np.arange(LANES)[None, :] - CMP_NEAR_BACK)
    pat = (_bias_rows(rel_bias, dist) - c31[:, None, None]) * jnp.asarray((np.arange(LANES) < n_near) * LOG2E, F32)
    ovt = _overlap_t(n_blk, n_c, n_cmp, n_slc)
    once = pl.Buffered(1)
    pair_rows = lambda base: pl.BlockSpec((None, LANES, T), lambda b, g, i: (b, base + g // 2, 0), pipeline_mode=once)
    pair_cols = lambda base: pl.BlockSpec((None, n_c, LANES), lambda b, g, i: (b, 0, base + g // 2))
    return pl.pallas_call(
        functools.partial(_nsa_prompt_kernel, tq=tq, n_cmp=n_cmp),
        grid=(B, G, T // tq),
        in_specs=[
            pl.BlockSpec((None, tq, R * HEAD_DIM), lambda b, g, i: (b, i, g)),
            pair_rows(4), pair_rows(6), pair_rows(0), pair_rows(2),
            pair_cols(0), pair_cols(2),
            pl.BlockSpec((None, tq, LANES), lambda b, g, i: (b, i, 0)),
            pl.BlockSpec((R, tq, LANES), lambda b, g, i: (g, 0, 0), pipeline_mode=once),
            pl.BlockSpec((R, 2, tq, tq), lambda b, g, i: (g, 0, 0, 0), pipeline_mode=once),
            pl.BlockSpec(ovt.shape, lambda b, g, i: (0, 0)),
        ],
        out_specs=pl.BlockSpec((None, tq, R * HEAD_DIM), lambda b, g, i: (b, i, g)),
        out_shape=jax.ShapeDtypeStruct((B, T, W), F32),
        scratch_shapes=[pltpu.VMEM((n_blk, tq), F32)],
        compiler_params=_cparams("parallel", "parallel", "arbitrary"),
        name="nsa_prompt",
    )(q, kvt, kvt, wint, wint, cmp, cmp, gates, pat, toe, ovt)


def _nsa_decode_kernel(pt_ref, q_ref, kvn_ref, wn_ref, gate_ref, cmp_ref, win_ref, bc_ref, bs_ref, bw_ref, ovt_ref,
                       *rest, n_pages, page, n_cmp):
    pages = rest[:n_pages]
    o_ref, wout_ref, s_scr = rest[n_pages], rest[n_pages + 1], rest[n_pages + 2]
    G, R = NSA_KV_HEADS, NSA_HPG
    nq = q_ref.shape[0]
    n_h = G * R
    rows = n_h * nq
    GW = G * HEAD_DIM
    P = n_pages * page
    n_win = win_ref.shape[-1]
    lane8 = lax.broadcasted_iota(jnp.int32, (nq, LANES), 1)

    q = q_ref[...] * HEAD_DIM ** -0.5
    blocks = []
    for h in range(n_h):
        g = h // R
        t = q[:, (h // 2) * LANES:(h // 2 + 1) * LANES]
        if h % 2 != g % 2:
            t = pltpu.roll(t, HEAD_DIM, 1)
        t = jnp.where((lane8 // HEAD_DIM) == g % 2, t, 0.0)
        z = jnp.zeros((nq, LANES), F32)
        blocks.append(jnp.concatenate([t, z] if g // 2 == 0 else [z, t], axis=1))
    qg = jnp.concatenate(blocks, axis=0).astype(BF16)

    ri = lax.broadcasted_iota(jnp.int32, (rows, page), 0) % nq
    ci = lax.broadcasted_iota(jnp.int32, (rows, page), 1)
    new_mask = ci <= ri

    n_c = cmp_ref.shape[0]
    cmpv = cmp_ref[...]
    cc = lax.broadcasted_iota(jnp.int32, (rows, n_c), 1)
    s_c = _nt_dot(qg, cmpv[:, :GW].astype(BF16)) + bc_ref[...]
    p_c = _masked_softmax(s_c, cc < n_cmp)
    o_c = jnp.dot(p_c.astype(BF16), cmpv[:, GW:].astype(BF16), preferred_element_type=F32)
    ps = []
    for g in range(G):
        acc = p_c[g * R * nq:(g * R + 1) * nq]
        for r in range(1, R):
            acc = acc + p_c[(g * R + r) * nq:(g * R + r + 1) * nq]
        ps.append(acc)
    psum = _pad_rows(jnp.concatenate(ps, axis=0), LANES)
    n_blk = ovt_ref.shape[0]
    imp = _importance_t(ovt_ref[...], psum)
    jj = lax.broadcasted_iota(jnp.int32, (n_blk, LANES), 0)
    t_pos = P + lax.broadcasted_iota(jnp.int32, (n_blk, LANES), 1) % nq
    t_blk = t_pos // SEL_BLOCK
    forced = (jj == 0) | (jj == t_blk) | (jj == t_blk - 1)
    imp = jnp.where(jj * SEL_BLOCK <= t_pos, imp + jnp.where(forced, SEL_FORCE, 0.0), -1.0)
    sel_t = _select_topk_t(imp, n_blk)
    sel = _pad_rows(sel_t, LANES).T
    sel_rows = jnp.concatenate([sel[g * nq:(g + 1) * nq] for g in range(G) for _ in range(R)], axis=0).astype(BF16)

    per_tile = page // SEL_BLOCK
    m = jnp.full((rows, 1), NEG, F32)
    for p in range(n_pages):
        kt = pages[p][0].reshape(GW, page).astype(BF16)
        msk = jnp.dot(sel_rows, _block_expand(p * per_tile, LANES, page), preferred_element_type=F32) > 0.5
        s = jnp.where(msk, jnp.dot(qg, kt, preferred_element_type=F32) + bs_ref[:, p * page:(p + 1) * page], NEG)
        s_scr[:, p * page:(p + 1) * page] = s
        m = jnp.maximum(m, jnp.max(s, axis=-1, keepdims=True))
    kvn = _pad_rows(kvn_ref[...], page)
    msk = (jnp.dot(sel_rows, _block_expand(n_pages * per_tile, LANES, page), preferred_element_type=F32) > 0.5) & new_mask
    sn = jnp.where(msk, _nt_dot(qg, kvn[:, 2 * GW:3 * GW].astype(BF16)) + bs_ref[:, P:], NEG)
    m = jnp.maximum(m, jnp.max(sn, axis=-1, keepdims=True))
    pn = jnp.where(msk, jnp.exp(sn - m), 0.0)
    l = jnp.sum(pn, axis=-1, keepdims=True)
    acc = jnp.dot(pn.astype(BF16), kvn[:, 3 * GW:].astype(BF16), preferred_element_type=F32)
    for p in range(n_pages):
        pr = jnp.exp(s_scr[:, p * page:(p + 1) * page] - m)
        l = l + jnp.sum(pr, axis=-1, keepdims=True)
        acc = acc + _nt_dot(pr.astype(BF16), pages[p][1].reshape(GW, page).astype(BF16))
    o_s = acc / l

    kwt = win_ref[0].reshape(GW, n_win)
    vwt = win_ref[1].reshape(GW, n_win)
    rw = lax.broadcasted_iota(jnp.int32, (rows, n_win), 0) % nq
    cw = lax.broadcasted_iota(jnp.int32, (rows, n_win), 1)
    mask_w = (n_win + rw - cw) < WINDOW
    s_w = jnp.where(mask_w, jnp.dot(qg, kwt.astype(BF16), preferred_element_type=F32) + bw_ref[:, :n_win], NEG)
    wn = _pad_rows(wn_ref[...], page)
    s_n = jnp.where(new_mask, _nt_dot(qg, wn[:, :GW].astype(BF16)) + bw_ref[:, n_win:], NEG)
    m = jnp.maximum(jnp.max(s_w, axis=-1, keepdims=True), jnp.max(s_n, axis=-1, keepdims=True))
    p_w = jnp.where(mask_w, jnp.exp(s_w - m), 0.0)
    p_n = jnp.where(new_mask, jnp.exp(s_n - m), 0.0)
    l = jnp.sum(p_w, axis=-1, keepdims=True) + jnp.sum(p_n, axis=-1, keepdims=True)
    o_w = (_nt_dot(p_w.astype(BF16), vwt.astype(BF16))
           + jnp.dot(p_n.astype(BF16), wn[:, GW:].astype(BF16), preferred_element_type=F32)) / l

    gs = jax.nn.sigmoid(gate_ref[...])
    grep = jnp.concatenate([gs] * n_h, axis=0)
    glane = lax.broadcasted_iota(jnp.int32, (rows, LANES), 1)
    ghead = lax.broadcasted_iota(jnp.int32, (rows, LANES), 0) // nq
    o = jnp.zeros((rows, GW), F32)
    for br, o_b in enumerate((o_c, o_s, o_w)):
        o = o + jnp.sum(jnp.where(glane == br * n_h + ghead, grep, 0.0), axis=-1, keepdims=True) * o_b
    pieces = []
    for h in range(n_h):
        g = h // R
        t = o[h * nq:(h + 1) * nq, (g // 2) * LANES:(g // 2 + 1) * LANES]
        pieces.append(t if h % 2 == g % 2 else pltpu.roll(t, HEAD_DIM, 1))
    o_ref[...] = jnp.concatenate(
        [jnp.where(lane8 < HEAD_DIM, pieces[2 * u], pieces[2 * u + 1]) for u in range(n_h // 2)], axis=1)

    wt = win_ref[...].reshape(2 * GW, n_win)
    wnt = wn.T
    wout_ref[...] = jnp.concatenate([wt[:, nq:], wnt[:, :nq]], axis=1).reshape(wout_ref.shape)


def nsa_decode_attention(q, kvn, wn, gates, cmp, win_t, layer, pool_t, page_table, rel_bias):
    S, nq, W = q.shape
    G, R = NSA_KV_HEADS, NSA_HPG
    n_pages = page_table.shape[1]
    page = pool_t.shape[-1]
    P = n_pages * page
    n_win = win_t.shape[-1]
    assert nq < CMP_STRIDE and n_win == WINDOW and nq <= 8
    L = P + nq
    n_cmp = (L - CMP_LEN) // CMP_STRIDE + 1
    n_c = cmp.shape[1]
    n_slc = -(-L // SEL_BLOCK)
    n_blk = -(-n_slc // 8) * 8
    heads = range(G * R)
    qpos = P + np.arange(nq)
    bias_c = _dense_bias(rel_bias, qpos, np.arange(n_c) * CMP_STRIDE + CMP_LEN - 1, heads)
    bias_s = _dense_bias(rel_bias, qpos, np.arange(P + page), heads)
    bias_w = _dense_bias(rel_bias, qpos, P - n_win + np.arange(n_win + page), heads)
    ovt = _overlap_t(n_blk, n_c, n_cmp, n_slc)
    rows = G * R * nq
    const = lambda a: pl.BlockSpec(a.shape, lambda b, pt: (0,) * a.ndim)
    seq = lambda a: pl.BlockSpec((None,) + a.shape[1:], lambda b, pt: (b,) + (0,) * (a.ndim - 1))
    win_blk = (None, None) + win_t.shape[2:]
    grid_spec = pltpu.PrefetchScalarGridSpec(
        num_scalar_prefetch=1,
        grid=(S,),
        in_specs=[seq(q), seq(kvn), seq(wn), seq(gates), seq(cmp),
                  pl.BlockSpec(win_blk, lambda b, pt: (layer, b, 0, 0, 0, 0)),
                  const(bias_c), const(bias_s), const(bias_w), const(ovt)]
        + _page_specs(n_pages, (None, None, 2) + pool_t.shape[3:], lambda pg: (layer, pg, 1, 0, 0, 0)),
        out_specs=[seq(q), pl.BlockSpec((None,) + win_t.shape[2:], lambda b, pt: (b, 0, 0, 0, 0))],
        scratch_shapes=[pltpu.VMEM((rows, P), F32)],
    )
    return pl.pallas_call(
        functools.partial(_nsa_decode_kernel, n_pages=n_pages, page=page, n_cmp=n_cmp),
        grid_spec=grid_spec,
        out_shape=[jax.ShapeDtypeStruct(q.shape, F32), jax.ShapeDtypeStruct(win_t.shape[1:], F32)],
        compiler_params=_cparams("arbitrary"),
        name="nsa_decode",
    )(page_table, q, kvn, wn, gates, cmp, win_t, bias_c, bias_s, bias_w, ovt, *([pool_t] * n_pages))


def _w_cols(w, c0, width):
    piece = w[:, c0:c0 + width]
    if width % LANES:
        piece = jnp.pad(piece, ((0, 0), (0, LANES - width % LANES)))
    return piece.astype(BF16)


def _w_rows(w, c0, width):
    return w[:, c0:c0 + width].T.astype(BF16)


def _seq_major(a, nq, S):
    return jnp.transpose(a.reshape(nq, S, -1), (1, 0, 2))


def kernel(x_prompt, x_sample, cache_nsa_kv, state_nsa_win, cache_diff_k, cache_diff_v, cache_sb_kv, page_table, rel_bias, nsa_w_in, nsa_cmp_pos, nsa_cmp_w1, nsa_cmp_w2, nsa_w_out, diff_w_in, diff_lambda, diff_norm_g, diff_w_out, sb_w_in, sb_w_out, mlp_w_up, mlp_w_down, ln_g, ln_b):
    B, T, D = x_prompt.shape
    S, nq, _ = x_sample.shape
    depth = mlp_w_up.shape[0]
    alpha = (2 * depth) ** 0.25
    G, R, dk = NSA_KV_HEADS, NSA_HPG, HEAD_DIM
    n_pool, page = cache_nsa_kv.shape[1], cache_nsa_kv.shape[2]
    Hd, H = D // (2 * dk), D // dk
    xp = x_prompt.reshape(B * T, D)
    xs = jnp.transpose(x_sample, (1, 0, 2)).reshape(nq * S, D)
    nsa_pool_t = jnp.transpose(cache_nsa_kv, (0, 1, 3, 4, 5, 2))
    nsa_win_t = jnp.transpose(state_nsa_win, (0, 1, 3, 4, 5, 2))
    diff_k_t = jnp.transpose(cache_diff_k, (0, 1, 3, 4, 5, 2))
    sb_pool_t = jnp.transpose(cache_sb_kv, (0, 1, 3, 4, 5, 2))
    res = {k: [] for k in ("nsa_kv_p", "nsa_kv_s", "nsa_win_p", "nsa_win_s", "diff_k_p", "diff_k_s",
                           "diff_v_p", "diff_v_s", "sb_kv_p", "sb_kv_s")}
    tm_p = 512

    def to_tokens(a_t, lead):
        n, _, t = a_t.shape
        nd = len(lead)
        return jnp.transpose(a_t.reshape((n,) + lead + (t,)), (0, nd + 1) + tuple(range(1, nd + 1)))

    for i in range(depth):
        kind, j = i % N_MIXERS, i // N_MIXERS
        if kind == 0:
            w = nsa_w_in[j]
            nq_c, kv_c, win_c = G * R * dk, 4 * G * dk, 2 * G * dk
            row_p = [_w_cols(w, 0, nq_c), _w_cols(w, nq_c, 2 * G * dk), _w_cols(w, nq_c + kv_c + win_c, 3 * G * R)]
            t_ws = [_w_rows(w, nq_c, kv_c), _w_rows(w, nq_c + kv_c, win_c)]
            cmp_w = _compress_weights(nsa_cmp_pos[j], nsa_cmp_w1[j], nsa_cmp_w2[j])
            q, kc_rows, gates, kvt, wint = project(xp, row_p, t_ws, B, tm_p)
            cmp_p = compress_seq(kc_rows.reshape(B, T, -1), cmp_w)
            op = nsa_prompt_attention(q.reshape(B, T, -1), kvt, wint, cmp_p, gates.reshape(B, T, -1), rel_bias)
            row_s = [row_p[0], _w_cols(w, nq_c, kv_c), _w_cols(w, nq_c + kv_c, win_c), row_p[2]]
            qs, kvs, wns, gts, kvst = project(xs, row_s, t_ws[:1], nq, S)
            cmp_s = compress_paged(nsa_pool_t, j, page_table, cmp_w)
            os_, win_s = nsa_decode_attention(_seq_major(qs, nq, S), _seq_major(kvs, nq, S), _seq_major(wns, nq, S),
                                              _seq_major(gts, nq, S), cmp_s, nsa_win_t, j, nsa_pool_t, page_table,
                                              rel_bias)
            n_keep = min(WINDOW, T)
            res["nsa_kv_p"].append(to_tokens(kvt, (4, G, dk)))
            res["nsa_kv_s"].append(jnp.transpose(kvst.reshape(nq, 4, G, dk, S), (4, 0, 1, 2, 3)))
            res["nsa_win_p"].append(to_tokens(wint[:, :, T - n_keep:], (2, G, dk)))
            res["nsa_win_s"].append(jnp.transpose(win_s, (0, 4, 1, 2, 3)))
            w_out = nsa_w_out[j]
        elif kind == 1:
            lam_init = 0.8 - 0.6 * math.exp(-0.3 * i)
            w = diff_w_in[j]
            row_ws, t_ws = [_w_cols(w, 0, D), _w_cols(w, 2 * D, D)], [_w_rows(w, D, D)]
            q, v, kt = project(xp, row_ws, t_ws, B, tm_p)
            op = diff_prompt_attention(q.reshape(B, T, D), kt, v.reshape(B, T, D), rel_bias,
                                       diff_lambda[j], diff_norm_g[j], lam_init)
            qs, vsn, ksn, kst = project(xs, row_ws + [_w_cols(w, D, D)], t_ws, nq, S)
            vsn = _seq_major(vsn, nq, S)
            os_ = diff_decode_attention(_seq_major(qs, nq, S), _seq_major(ksn, nq, S), vsn, diff_k_t[j],
                                        cache_diff_v[j].reshape(n_pool, page * Hd, 2 * dk), page_table, rel_bias,
                                        diff_lambda[j], diff_norm_g[j], lam_init)
            res["diff_k_p"].append(to_tokens(kt, (Hd, 2, dk)))
            res["diff_k_s"].append(jnp.transpose(kst.reshape(nq, Hd, 2, dk, S), (4, 0, 1, 2, 3)))
            res["diff_v_p"].append(v.reshape(B, T, Hd, 2 * dk))
            res["diff_v_s"].append(vsn.reshape(S, nq, Hd, 2 * dk))
            w_out = diff_w_out[j]
        else:
            w = sb_w_in[j]
            row_ws, t_ws = [_w_cols(w, 0, D)], [_w_rows(w, D, 2 * D)]
            q, kvt = project(xp, row_ws, t_ws, B, tm_p)
            op = sb_prompt_attention(q.reshape(B, T, D), kvt)
            qs, kvs, kvst = project(xs, row_ws + [_w_cols(w, D, 2 * D)], t_ws, nq, S)
            os_ = sb_decode_attention(_seq_major(qs, nq, S), _seq_major(kvs, nq, S), sb_pool_t[j], page_table)
            res["sb_kv_p"].append(to_tokens(kvt, (2, H, dk)))
            res["sb_kv_s"].append(jnp.transpose(kvst.reshape(nq, 2, H, dk, S), (4, 0, 1, 2, 3)))
            w_out = sb_w_out[j]
        ln = jnp.stack([ln_g[i, 0], ln_b[i, 0], ln_g[i, 1], ln_b[i, 1]])
        w_out, w_up, w_down = w_out.astype(BF16), mlp_w_up[i].astype(BF16), mlp_w_down[i].astype(BF16)
        xp = post_mixer(op.reshape(B * T, D), w_out, xp, ln, w_up, w_down, alpha)
        os_ = jnp.transpose(os_, (1, 0, 2)).reshape(nq * S, D)
        xs = post_mixer(os_, w_out, xs, ln, w_up, w_down, alpha)
    return (xp.reshape(B, T, D), _seq_major(xs, nq, S),
            jnp.stack(res["nsa_kv_p"]), jnp.stack(res["nsa_kv_s"]), jnp.stack(res["nsa_win_p"]),
            jnp.stack(res["nsa_win_s"]), jnp.stack(res["diff_k_p"]), jnp.stack(res["diff_k_s"]),
            jnp.stack(res["diff_v_p"]), jnp.stack(res["diff_v_s"]), jnp.stack(res["sb_kv_p"]),
            jnp.stack(res["sb_kv_s"]))
```
